```python
import jax
import jax.numpy as jnp
from jax import lax
import numpy as np

D_MODEL = 1024
BATCH = 16
SEQ = 2048
DEPTH = 1
DEC_BATCH = 128
DEC_SEQ = 8
PAST_LEN = 8192
PAGE_SIZE = 128

HG_HEADS = 4
HG_DK = D_MODEL // 8
HG_DV = D_MODEL // 8
HG_WIDTH = HG_HEADS * HG_DV
HG_CHUNK = 64
AT_HEADS = 4
AT_HD = D_MODEL // 8
AT_WIDTH = AT_HEADS * AT_HD
MOBA_BLOCK = 256
MOBA_TOPK = 3
MOBA_QGROUP = 16
N_EXPERTS = 32
TOP_K = 4
D_FF = D_MODEL
SWIGLU_LIMIT = 7.0
SWIGLU_ALPHA = 1.702
MOE_BLOCK = 256
PLE_DIM = 256
EPS = 1e-6
IN_SPLITS = (HG_HEADS * HG_DK, HG_HEADS * HG_DK, HG_WIDTH, HG_WIDTH, AT_WIDTH, AT_WIDTH, AT_WIDTH, D_MODEL, D_MODEL)
N_IN = 2 * HG_HEADS * HG_DK + 2 * HG_WIDTH + 3 * AT_WIDTH + 2 * D_MODEL

kernel_name = "hgrn2_moba_moe_ple_step"


def _rmsnorm(x, g):
    xf = x.astype(jnp.float32)
    y = xf * lax.rsqrt(jnp.mean(xf * xf, axis=-1, keepdims=True) + EPS)
    return (y * g.astype(jnp.float32)).astype(x.dtype)


def _split_in(z):
    parts = []
    off = 0
    for width in IN_SPLITS:
        parts.append(z[..., off:off + width])
        off += width
    return parts


def _hgrn2_chunked(q, k, log_f, v, s0):
    B, T, H, DK = q.shape
    DV = v.shape[-1]
    c = min(HG_CHUNK, T)
    n_c = -(-T // c)
    pad = n_c * c - T

    def chunks(a):
        a = jnp.pad(a, ((0, 0), (0, pad), (0, 0), (0, 0)))
        return a.reshape(B, n_c, c, H, a.shape[-1]).transpose(1, 0, 2, 3, 4)

    causal = jnp.tril(jnp.ones((c, c), dtype=bool))

    def step(S, inp):
        qc, kc, lfc, vc = inp
        b = jnp.cumsum(lfc, axis=1)
        qd = qc * jnp.exp(b)
        kd = kc * jnp.exp(-b)
        att = jnp.where(causal, jnp.einsum('bthk,bshk->bhts', qd, kd), 0.0)
        o = jnp.einsum('bhts,bshv->bthv', att, vc) + jnp.einsum('bthk,bhkv->bthv', qd, S)
        b_end = b[:, -1:]
        S = jnp.exp(b_end[:, 0])[..., None] * S + jnp.einsum('bshk,bshv->bhkv', kc * jnp.exp(b_end - b), vc)
        return S, o

    s_fin, o = lax.scan(step, s0, (chunks(q), chunks(k), chunks(log_f), chunks(v)))
    o = o.transpose(1, 0, 2, 3, 4).reshape(B, n_c * c, H, DV)[:, :T]
    return o, s_fin


def _moba_prompt(q, k, v):
    B, S, H, HD = q.shape
    L = MOBA_BLOCK
    n_b = -(-S // L)
    pad = n_b * L - S
    kb = jnp.pad(k, ((0, 0), (0, pad), (0, 0), (0, 0))).reshape(B, n_b, L, H, HD).transpose(0, 3, 1, 2, 4)
    vb = jnp.pad(v, ((0, 0), (0, pad), (0, 0), (0, 0))).reshape(B, n_b, L, H, HD).transpose(0, 3, 1, 2, 4)
    qh = q.transpose(0, 2, 1, 3)
    scale = AT_HD ** -0.5
    QG = MOBA_QGROUP
    n_g = S // QG
    k_sel = min(MOBA_TOPK, n_b - 1)
    qg = qh.reshape(B, H, n_g, QG, HD).transpose(2, 0, 1, 3, 4)
    gidx = jnp.arange(n_g)
    bi = jnp.arange(B)[:, None, None, None]
    hi = jnp.arange(H)[None, :, None, None]
    if k_sel > 0:
        kmean = jnp.mean(kb.astype(jnp.float32), axis=3)
        gsc = jnp.einsum('bhsd,bhnd->bhsn', qh.astype(jnp.float32), kmean)
        fully_past = jnp.arange(n_b)[None, :] < (jnp.arange(S) // L)[:, None]
        gsc = jnp.where(fully_past, gsc, -jnp.inf)
        _, sel = lax.top_k(gsc, k_sel)
        xs = (qg, sel.reshape(B, H, n_g, QG, k_sel).transpose(2, 0, 1, 3, 4), gidx)
    else:
        xs = (qg, gidx)

    def step(args):
        if k_sel > 0:
            qt, st, g = args
        else:
            qt, g = args
        qpos = g * QG + jnp.arange(QG)
        blk = (g * QG) // L
        k_own = lax.dynamic_index_in_dim(kb, blk, axis=2, keepdims=False)
        v_own = lax.dynamic_index_in_dim(vb, blk, axis=2, keepdims=False)
        kpos = blk * L + jnp.arange(L)
        s_own = jnp.einsum('bhqd,bhld->bhql', qt, k_own).astype(jnp.float32) * scale
        s_own = jnp.where(kpos[None, :] <= qpos[:, None], s_own, -jnp.inf)
        if k_sel > 0:
            kg = kb[bi, hi, st]
            vg = vb[bi, hi, st]
            s_past = jnp.einsum('bhqd,bhqkld->bhqkl', qt, kg).astype(jnp.float32) * scale
            valid = jnp.arange(k_sel)[None, :] < (qpos // L)[:, None]
            s_past = jnp.where(valid[:, :, None], s_past, -jnp.inf).reshape(B, H, QG, k_sel * L)
            pr = jax.nn.softmax(jnp.concatenate([s_past, s_own], axis=-1), axis=-1).astype(v.dtype)
            pp = pr[..., :k_sel * L].reshape(B, H, QG, k_sel, L)
            return (jnp.einsum('bhqkl,bhqkld->bhqd', pp, vg)
                    + jnp.einsum('bhql,bhld->bhqd', pr[..., k_sel * L:], v_own))
        pr = jax.nn.softmax(s_own, axis=-1).astype(v.dtype)
        return jnp.einsum('bhql,bhld->bhqd', pr, v_own)

    o = lax.map(step, xs)
    return o.transpose(1, 3, 0, 2, 4).reshape(B, H, S, HD).transpose(0, 2, 1, 3)


def _moba_sample(q, k, v, cache_k, cache_v, page_table, li):
    DB, T, H, HD = q.shape
    PS = cache_k.shape[3]
    n_pg = page_table.shape[1]
    past = n_pg * PS
    L = MOBA_BLOCK
    ppb = L // PS
    n_full = past // L
    r = past - n_full * L
    scale = AT_HD ** -0.5
    qh = q.transpose(0, 2, 1, 3)
    kn = k.transpose(0, 2, 1, 3)
    vn = v.transpose(0, 2, 1, 3)
    new_ok = jnp.arange(T)[None, :] <= jnp.arange(T)[:, None]
    if r > 0:
        own_pages = page_table[:, n_full * ppb:n_pg]
        ko = cache_k[li, own_pages].transpose(0, 2, 1, 3, 4).reshape(DB, H, r, HD)
        vo = cache_v[li, own_pages].transpose(0, 2, 1, 3, 4).reshape(DB, H, r, HD)
        k_own = jnp.concatenate([ko.astype(k.dtype), kn], axis=2)
        v_own = jnp.concatenate([vo.astype(v.dtype), vn], axis=2)
        own_valid = jnp.concatenate([jnp.ones((T, r), dtype=bool), new_ok], axis=1)
    else:
        k_own, v_own, own_valid = kn, vn, new_ok
    k_sel = min(MOBA_TOPK, n_full)
    bi = jnp.arange(DB)[:, None, None, None]
    hi = jnp.arange(H)[None, :, None, None]
    q_t = qh.transpose(2, 0, 1, 3)
    if k_sel > 0:
        kmean = lax.map(
            lambda pt: jnp.mean(cache_k[li, pt[:n_full * ppb]].astype(jnp.float32)
                                .reshape(n_full, ppb, H, PS, HD), axis=(1, 3)),
            page_table)
        gsc = jnp.einsum('bhtd,bnhd->bhtn', qh.astype(jnp.float32), kmean)
        _, sel = lax.top_k(gsc, k_sel)
        xs = (q_t, sel.transpose(2, 0, 1, 3), own_valid)
    else:
        xs = (q_t, own_valid)

    def step(args):
        if k_sel > 0:
            qt, st, ov = args
        else:
            qt, ov = args
        s_own = jnp.einsum('bhd,bhld->bhl', qt, k_own).astype(jnp.float32) * scale
        s_own = jnp.where(ov, s_own, -jnp.inf)
        if k_sel > 0:
            lp = st[..., None] * ppb + jnp.arange(ppb)
            phys = page_table[bi, lp]
            kg = cache_k[li, phys, hi].reshape(DB, H, k_sel * L, HD)
            vg = cache_v[li, phys, hi].reshape(DB, H, k_sel * L, HD)
            s_past = jnp.einsum('bhd,bhnd->bhn', qt, kg).astype(jnp.float32) * scale
            pr = jax.nn.softmax(jnp.concatenate([s_past, s_own], axis=-1), axis=-1).astype(v.dtype)
            return (jnp.einsum('bhn,bhnd->bhd', pr[..., :k_sel * L], vg.astype(v.dtype))
                    + jnp.einsum('bhl,bhld->bhd', pr[..., k_sel * L:], v_own))
        pr = jax.nn.softmax(s_own, axis=-1).astype(v.dtype)
        return jnp.einsum('bhl,bhld->bhd', pr, v_own)

    o = lax.map(step, xs)
    return o.transpose(1, 0, 2, 3)


def _moe(h, w_router, b_router, w_gate_up, b_gate_up, w_down, b_down):
    n_tok, d = h.shape
    n_assign = n_tok * TOP_K
    logits = jnp.dot(h.astype(jnp.float32), w_router.astype(jnp.float32)) + b_router.astype(jnp.float32)
    top_val, top_e = lax.top_k(logits, TOP_K)
    gate = jax.nn.softmax(top_val, axis=-1)
    flat_e = top_e.reshape(-1)
    flat_tok = jnp.repeat(jnp.arange(n_tok, dtype=jnp.int32), TOP_K)
    order = jnp.argsort(flat_e)
    e_sorted = flat_e[order]
    counts = jnp.bincount(flat_e, length=N_EXPERTS)
    starts = jnp.cumsum(counts) - counts
    padded = (counts + MOE_BLOCK - 1) // MOE_BLOCK * MOE_BLOCK
    pad_end = jnp.cumsum(padded)
    pad_start = pad_end - padded
    dest = pad_start[e_sorted] + jnp.arange(n_assign) - starts[e_sorted]
    n_blocks = -(-(n_assign + N_EXPERTS * (MOE_BLOCK - 1)) // MOE_BLOCK)
    n_slots = n_blocks * MOE_BLOCK
    slot_tok = jnp.zeros((n_slots,), jnp.int32).at[dest].set(flat_tok[order])
    slot_w = jnp.zeros((n_slots,), jnp.float32).at[dest].set(gate.reshape(-1)[order])
    block_e = jnp.minimum(jnp.searchsorted(pad_end, jnp.arange(n_blocks) * MOE_BLOCK, side='right'),
                          N_EXPERTS - 1)

    def expert_block(args):
        tok, e = args
        xb = h[tok]
        gu = xb @ w_gate_up[e] + b_gate_up[e]
        g_ = jnp.minimum(gu[:, :D_FF], SWIGLU_LIMIT)
        u_ = jnp.clip(gu[:, D_FF:], -SWIGLU_LIMIT, SWIGLU_LIMIT)
        act = (u_ + 1.0) * g_ * jax.nn.sigmoid(SWIGLU_ALPHA * g_)
        return act @ w_down[e] + b_down[e]

    y_blocks = lax.map(expert_block, (slot_tok.reshape(n_blocks, MOE_BLOCK), block_e))
    y = y_blocks.reshape(n_slots, d) * slot_w[:, None].astype(y_blocks.dtype)
    return jnp.zeros((n_tok, d), h.dtype).at[slot_tok].add(y.astype(h.dtype))


def _layer(x, pe_in, s0, attn_fn, lw):
    (g_mix, w_in, lb, g_hg_out, g_q, g_k, w_lift_hgrn, w_lift_moba, w_out, g_ffn, w_router, b_router,
     w_gate_up, b_gate_up, w_down, b_down, g_ple, w_ple, w_ple_gate) = lw
    B, T, D = x.shape
    h = _rmsnorm(x, g_mix)
    z = h @ w_in
    qa, fa, ia, ga, qb, kb, vb, gate_h, gate_m = _split_in(z)
    lbh = lb.reshape(HG_HEADS, HG_DK)
    fa32 = fa.astype(jnp.float32).reshape(B, T, HG_HEADS, HG_DK)
    log_f = jnp.log(lbh + (1.0 - lbh) * jax.nn.sigmoid(fa32))
    k_in = (1.0 - lbh) * jax.nn.sigmoid(-fa32)
    q_h = jax.nn.silu(qa.astype(jnp.float32)).reshape(B, T, HG_HEADS, HG_DK)
    v_h = ia.astype(jnp.float32).reshape(B, T, HG_HEADS, HG_DV)
    o_h, s_new = _hgrn2_chunked(q_h, k_in, log_f, v_h, s0.astype(jnp.float32))
    o_h = _rmsnorm(o_h, g_hg_out) * jax.nn.silu(ga.astype(jnp.float32).reshape(B, T, HG_HEADS, HG_DV))
    branch_h = o_h.reshape(B, T, HG_WIDTH).astype(x.dtype)
    q_m = _rmsnorm(qb.reshape(B, T, AT_HEADS, AT_HD), g_q)
    k_m = _rmsnorm(kb.reshape(B, T, AT_HEADS, AT_HD), g_k)
    v_m = vb.reshape(B, T, AT_HEADS, AT_HD)
    branch_m = attn_fn(q_m, k_m, v_m).reshape(B, T, AT_WIDTH)
    mixed = (jax.nn.sigmoid(gate_h) * (branch_h @ w_lift_hgrn)
             + jax.nn.sigmoid(gate_m) * (branch_m @ w_lift_moba))
    x1 = x + mixed @ w_out
    h2 = _rmsnorm(x1, g_ffn)
    x2 = x1 + _moe(h2.reshape(B * T, D), w_router, b_router, w_gate_up, b_gate_up, w_down, b_down).reshape(B, T, D)
    ple = (pe_in.astype(x.dtype) @ w_ple) * jax.nn.sigmoid(_rmsnorm(x2, g_ple) @ w_ple_gate)
    return x2 + ple, k_m, v_m, s_new.astype(x.dtype)


def setup_inputs(seed: int = 0) -> dict:
    key = jax.random.key(seed)
    ks = jax.random.split(key, 32)
    f32 = jnp.float32
    n_pages = PAST_LEN // PAGE_SIZE
    n_used = DEC_BATCH * n_pages
    n_phys = n_used + max(1, n_used // 4)

    def nrm(k, shape, scale=1.0):
        return jax.random.normal(k, shape, f32) * scale

    def gain(k, shape):
        return 1.0 + 0.05 * jax.random.normal(k, shape, f32)

    page_table = jax.random.permutation(ks[0], n_phys)[:n_used].reshape(DEC_BATCH, n_pages).astype(jnp.int32)
    return {
        'x_prompt': nrm(ks[1], (BATCH, SEQ, D_MODEL)),
        'x_sample': nrm(ks[2], (DEC_BATCH, DEC_SEQ, D_MODEL)),
        'cache_k': nrm(ks[3], (DEPTH, n_phys, AT_HEADS, PAGE_SIZE, AT_HD)),
        'cache_v': nrm(ks[4], (DEPTH, n_phys, AT_HEADS, PAGE_SIZE, AT_HD)),
        'state_hgrn': nrm(ks[5], (DEPTH, DEC_BATCH, HG_HEADS, HG_DK, HG_DV), 0.5),
        'page_table': page_table,
        'p_prompt': nrm(ks[6], (DEPTH, BATCH, SEQ, PLE_DIM)),
        'p_sample': nrm(ks[7], (DEPTH, DEC_BATCH, DEC_SEQ, PLE_DIM)),
        'g_mix': gain(ks[8], (DEPTH, D_MODEL)),
        'w_in': nrm(ks[9], (DEPTH, D_MODEL, N_IN), D_MODEL ** -0.5),
        'lb_logits': nrm(ks[10], (DEPTH + 1, HG_HEADS * HG_DK), 0.5),
        'g_hg_out': gain(ks[11], (DEPTH, HG_DV)),
        'g_q': gain(ks[12], (DEPTH, AT_HD)),
        'g_k': gain(ks[13], (DEPTH, AT_HD)),
        'w_lift_hgrn': nrm(ks[14], (DEPTH, HG_WIDTH, D_MODEL), HG_WIDTH ** -0.5),
        'w_lift_moba': nrm(ks[15], (DEPTH, AT_WIDTH, D_MODEL), AT_WIDTH ** -0.5),
        'w_out': nrm(ks[16], (DEPTH, D_MODEL, D_MODEL), D_MODEL ** -0.5),
        'g_ffn': gain(ks[17], (DEPTH, D_MODEL)),
        'w_router': nrm(ks[18], (DEPTH, D_MODEL, N_EXPERTS), D_MODEL ** -0.5),
        'b_router': nrm(ks[19], (DEPTH, N_EXPERTS), 0.01),
        'w_gate_up': nrm(ks[20], (DEPTH, N_EXPERTS, D_MODEL, 2 * D_FF), D_MODEL ** -0.5),
        'b_gate_up': nrm(ks[21], (DEPTH, N_EXPERTS, 2 * D_FF), 0.01),
        'w_down': nrm(ks[22], (DEPTH, N_EXPERTS, D_FF, D_MODEL), D_FF ** -0.5),
        'b_down': nrm(ks[23], (DEPTH, N_EXPERTS, D_MODEL), 0.01),
        'g_ple': gain(ks[24], (DEPTH, D_MODEL)),
        'w_ple': nrm(ks[25], (DEPTH, PLE_DIM, D_MODEL), PLE_DIM ** -0.5),
        'w_ple_gate': nrm(ks[26], (DEPTH, D_MODEL, D_MODEL), D_MODEL ** -0.5),
    }


def reference(x_prompt, x_sample, cache_k, cache_v, state_hgrn, page_table, p_prompt, p_sample,
              g_mix, w_in, lb_logits, g_hg_out, g_q, g_k, w_lift_hgrn, w_lift_moba, w_out, g_ffn,
              w_router, b_router, w_gate_up, b_gate_up, w_down, b_down, g_ple, w_ple, w_ple_gate):
    PS = cache_k.shape[3]
    lb_all = jnp.cumsum(jax.nn.softmax(lb_logits.astype(jnp.float32), axis=0), axis=0)
    xp, xs = x_prompt, x_sample
    kp_l, vp_l, ks_l, vs_l, sp_l, ss_l = [], [], [], [], [], []
    for li in range(DEPTH):
        lw = (g_mix[li], w_in[li], lb_all[li], g_hg_out[li], g_q[li], g_k[li], w_lift_hgrn[li],
              w_lift_moba[li], w_out[li], g_ffn[li], w_router[li], b_router[li], w_gate_up[li],
              b_gate_up[li], w_down[li], b_down[li], g_ple[li], w_ple[li], w_ple_gate[li])
        B, S, _ = xp.shape
        s0p = jnp.zeros((B, HG_HEADS, HG_DK, HG_DV), xp.dtype)
        xp, kp, vp, sp = _layer(xp, p_prompt[li], s0p, _moba_prompt, lw)
        sample_attn = lambda q, k, v, li=li: _moba_sample(q, k, v, cache_k, cache_v, page_table, li)
        xs, kn, vn, sn = _layer(xs, p_sample[li], state_hgrn[li], sample_attn, lw)
        kp_l.append(kp.reshape(B, S // PS, PS, AT_HEADS, AT_HD).transpose(0, 1, 3, 2, 4))
        vp_l.append(vp.reshape(B, S // PS, PS, AT_HEADS, AT_HD).transpose(0, 1, 3, 2, 4))
        ks_l.append(kn.transpose(0, 2, 1, 3))
        vs_l.append(vn.transpose(0, 2, 1, 3))
        sp_l.append(sp)
        ss_l.append(sn)
    k_prompt = jnp.stack(kp_l)
    v_prompt = jnp.stack(vp_l)
    k_sample = jnp.stack(ks_l)
    v_sample = jnp.stack(vs_l)
    state_prompt = jnp.stack(sp_l)
    state_sample = jnp.stack(ss_l)
    return (xp, xs, k_prompt, v_prompt, k_sample, v_sample, state_prompt, state_sample)
```

```python
import functools

import jax
import jax.numpy as jnp
from jax import lax
from jax.experimental import pallas as pl
from jax.experimental.pallas import tpu as pltpu

F32 = jnp.float32
BF16 = jnp.bfloat16
HIGHEST = lax.Precision.HIGHEST

EPS = 1e-6
N_HEADS = 4
HEAD_DIM = 128
HG_CHUNK = 64
MOBA_BLOCK = 256
MOBA_TOPK = 3
MOBA_QGROUP = 16
N_EXPERTS = 32
TOP_K = 4
SWIGLU_LIMIT = 7.0
SWIGLU_ALPHA = 1.702
NEG = -1e30

LANES = 128
SUBLANES = 8
TOKEN_TILE = 256
MOE_TILE = 512
VMEM_LIMIT = 56 * 1024 * 1024


def _params(*sem):
    return pltpu.CompilerParams(dimension_semantics=sem, vmem_limit_bytes=VMEM_LIMIT)


def _sigmoid(x):
    return 1.0 / (1.0 + jnp.exp(-x))


def _silu(x):
    return x * _sigmoid(x)


def _rms(x, g):
    return x * lax.rsqrt(jnp.mean(x * x, axis=-1, keepdims=True) + EPS) * g


def _dot(a, b):
    return jnp.dot(a.astype(BF16), b.astype(BF16), preferred_element_type=F32)


def _dot_nt(a, b):
    return lax.dot_general(a.astype(BF16), b.astype(BF16), (((1,), (1,)), ((), ())),
                           preferred_element_type=F32)


def _dot_tn(a, b):
    return lax.dot_general(a.astype(BF16), b.astype(BF16), (((0,), (0,)), ((), ())),
                           preferred_element_type=F32)


def _token_tile(n):
    return TOKEN_TILE if n % TOKEN_TILE == 0 else n


def _full(shape):
    return pl.BlockSpec(shape, lambda *_: (0,) * len(shape))


def _inproj_kernel(x_ref, gmix_ref, w_ref, lb_ref, gq_ref, gk_ref,
                   zh_ref, q_ref, k_ref, v_ref, gate_ref):
    x = x_ref[...]
    h = _rms(x, gmix_ref[...]).astype(BF16)
    hw = N_HEADS * HEAD_DIM
    d = x.shape[-1]
    tm = x.shape[0]

    def proj(c0, width):
        return jnp.dot(h, w_ref[:, c0:c0 + width], preferred_element_type=F32)

    lb = lb_ref[...]
    zh_ref[:, 0:hw] = _silu(proj(0, hw))
    fa = proj(hw, hw)
    zh_ref[:, hw:2 * hw] = jnp.log(lb + (1.0 - lb) * _sigmoid(fa))
    zh_ref[:, 2 * hw:3 * hw] = (1.0 - lb) * _sigmoid(-fa)
    zh_ref[:, 3 * hw:4 * hw] = proj(2 * hw, hw)
    zh_ref[:, 4 * hw:5 * hw] = _silu(proj(3 * hw, hw))

    qb = proj(4 * hw, hw)
    kb = proj(5 * hw, hw)
    vb = proj(6 * hw, hw)
    rows = k_ref.shape[2]
    for hd in range(N_HEADS):
        sl = slice(hd * HEAD_DIM, (hd + 1) * HEAD_DIM)
        q_ref[:, sl] = _rms(qb[:, sl], gq_ref[...])
        k_ref[:, hd, :, :] = _rms(kb[:, sl], gk_ref[...]).reshape(tm // rows, rows, HEAD_DIM)
        v_ref[:, hd, :, :] = vb[:, sl].reshape(tm // rows, rows, HEAD_DIM)

    gate_ref[:, 0:d] = _sigmoid(proj(7 * hw, d))
    gate_ref[:, d:2 * d] = _sigmoid(proj(7 * hw + d, d))


def _inproj(x2d, g_mix, w_in_bf, lb, g_q, g_k, rows_per_page):
    n, d = x2d.shape
    hw = N_HEADS * HEAD_DIM
    tm = _token_tile(n)
    r = rows_per_page
    n_in = w_in_bf.shape[1]
    out_shape = (
        jax.ShapeDtypeStruct((n, 5 * hw), F32),
        jax.ShapeDtypeStruct((n, hw), F32),
        jax.ShapeDtypeStruct((n // r, N_HEADS, r, HEAD_DIM), F32),
        jax.ShapeDtypeStruct((n // r, N_HEADS, r, HEAD_DIM), F32),
        jax.ShapeDtypeStruct((n, 2 * d), F32),
    )
    row = lambda w: pl.BlockSpec((tm, w), lambda i: (i, 0))
    page = pl.BlockSpec((tm // r, N_HEADS, r, HEAD_DIM), lambda i: (i, 0, 0, 0))
    return pl.pallas_call(
        _inproj_kernel,
        grid=(n // tm,),
        in_specs=[row(d), _full((1, d)),
                  pl.BlockSpec((d, n_in), lambda i: (0, 0), pipeline_mode=pl.Buffered(1)),
                  _full((1, hw)), _full((1, HEAD_DIM)), _full((1, HEAD_DIM))],
        out_specs=(row(5 * hw), row(hw), page, page, row(2 * d)),
        out_shape=out_shape,
        compiler_params=_params("parallel"),
        name="inproj",
    )(x2d, g_mix.reshape(1, d), w_in_bf, lb.reshape(1, hw), g_q.reshape(1, HEAD_DIM),
      g_k.reshape(1, HEAD_DIM))


def _hgrn_kernel(*refs, chunk, has_s0, exact_mxu):
    if has_s0:
        zh_ref, s0_ref, g_ref, o_ref, sout_ref, st_ref = refs
    else:
        zh_ref, g_ref, o_ref, sout_ref, st_ref = refs
        s0_ref = None
    t = pl.program_id(1)
    hw = N_HEADS * HEAD_DIM
    tb = zh_ref.shape[0]

    @pl.when(t == 0)
    def _():
        for hd in range(N_HEADS):
            if has_s0:
                st_ref[hd] = s0_ref[0, hd].T
            else:
                st_ref[hd] = jnp.zeros((HEAD_DIM, HEAD_DIM), F32)

    if exact_mxu:
        mm_nt = lambda a, b: lax.dot_general(a, b, (((1,), (1,)), ((), ())), precision=HIGHEST,
                                             preferred_element_type=F32)
        mm = lambda a, b: jnp.dot(a, b, precision=HIGHEST, preferred_element_type=F32)
        mm_tn = lambda a, b: lax.dot_general(a, b, (((0,), (0,)), ((), ())), precision=HIGHEST,
                                             preferred_element_type=F32)
    else:
        mm_nt, mm, mm_tn = _dot_nt, _dot, _dot_tn

    ri = lax.broadcasted_iota(jnp.int32, (chunk, chunk), 0)
    ci = lax.broadcasted_iota(jnp.int32, (chunk, chunk), 1)
    causal = ci <= ri
    tril = jnp.where(causal, 1.0, 0.0).astype(F32)

    for c in range(tb // chunk):
        rows = slice(c * chunk, (c + 1) * chunk)
        for hd in range(N_HEADS):
            col = lambda seg: slice(seg * hw + hd * HEAD_DIM, seg * hw + (hd + 1) * HEAD_DIM)
            q = zh_ref[rows, col(0)]
            lf = zh_ref[rows, col(1)]
            k = zh_ref[rows, col(2)]
            v = zh_ref[rows, col(3)]
            og = zh_ref[rows, col(4)]
            b = jnp.dot(tril, lf, precision=HIGHEST, preferred_element_type=F32)
            qd = q * jnp.exp(b)
            kd = k * jnp.exp(-b)
            att = jnp.where(causal, mm_nt(qd, kd), 0.0)
            st = st_ref[hd]
            o = mm(att, v) + mm_nt(qd, st)
            b_end = b[chunk - 1:chunk, :]
            st_ref[hd] = jnp.exp(b_end) * st + mm_tn(v, k * jnp.exp(b_end - b))
            o_ref[rows, hd * HEAD_DIM:(hd + 1) * HEAD_DIM] = _rms(o, g_ref[...]) * og

    @pl.when(t == pl.num_programs(1) - 1)
    def _():
        for hd in range(N_HEADS):
            sout_ref[0, hd] = st_ref[hd].T


def _hgrn(zh, s0, g_hg_out, n_seq, seq_len):
    hw = N_HEADS * HEAD_DIM
    chunk = min(HG_CHUNK, seq_len)
    assert seq_len % chunk == 0
    tb = min(TOKEN_TILE, seq_len)
    assert seq_len % tb == 0 and tb % chunk == 0
    n_t = seq_len // tb
    has_s0 = s0 is not None
    state_spec = pl.BlockSpec((1, N_HEADS, HEAD_DIM, HEAD_DIM), lambda b, t: (b, 0, 0, 0))
    in_specs = [pl.BlockSpec((tb, 5 * hw), lambda b, t: (b * n_t + t, 0))]
    args = [zh]
    if has_s0:
        in_specs.append(state_spec)
        args.append(s0)
    in_specs.append(_full((1, HEAD_DIM)))
    args.append(g_hg_out.reshape(1, HEAD_DIM))
    return pl.pallas_call(
        functools.partial(_hgrn_kernel, chunk=chunk, has_s0=has_s0, exact_mxu=chunk < 16),
        grid=(n_seq, n_t),
        in_specs=in_specs,
        out_specs=(pl.BlockSpec((tb, hw), lambda b, t: (b * n_t + t, 0)), state_spec),
        out_shape=(jax.ShapeDtypeStruct((n_seq * seq_len, hw), F32),
                   jax.ShapeDtypeStruct((n_seq, N_HEADS, HEAD_DIM, HEAD_DIM), F32)),
        scratch_shapes=[pltpu.VMEM((N_HEADS, HEAD_DIM, HEAD_DIM), F32)],
        compiler_params=_params("parallel", "arbitrary"),
        name="hgrn_prompt" if not has_s0 else "hgrn_sample",
    )(*args)


def _topk_rank(gsc, n_valid):
    rows, n_col = gsc.shape
    blk = lax.broadcasted_iota(jnp.int32, (rows, n_col), 1)
    rank = jnp.zeros((rows, n_col), jnp.int32)
    for m in range(n_col):
        sm = gsc[:, m:m + 1]
        beats = jnp.where(sm > gsc, 1, jnp.where(sm == gsc, jnp.where(blk > m, 1, 0), 0))
        rank = rank + jnp.where(m < n_valid, beats, 0)
    return blk, rank


def _moba_prompt_kernel(q_ref, k_ref, v_ref, o_ref, kmean_ref, osc_ref):
    n = pl.program_id(2)
    n_b = kmean_ref.shape[0]
    ppb = MOBA_BLOCK // k_ref.shape[1]
    scale = HEAD_DIM ** -0.5

    @pl.when(n == 0)
    def _():
        for m in range(n_b):
            kb = k_ref[m * ppb:(m + 1) * ppb].reshape(MOBA_BLOCK, HEAD_DIM)
            kmean_ref[m:m + 1, :] = jnp.mean(kb, axis=0, keepdims=True)

    q = q_ref[...]
    gsc = lax.dot_general(q, kmean_ref[...], (((1,), (1,)), ((), ())), precision=HIGHEST,
                          preferred_element_type=F32)
    blk, rank = _topk_rank(gsc, n)
    sel = jnp.where((blk < n) & (rank < MOBA_TOPK), 1.0, 0.0)

    def block(m):
        kb = k_ref[pl.ds(m * ppb, ppb)].reshape(MOBA_BLOCK, HEAD_DIM)
        vb = v_ref[pl.ds(m * ppb, ppb)].reshape(MOBA_BLOCK, HEAD_DIM)
        return kb, vb

    qb = q.astype(BF16)
    k_own, v_own = block(n)
    ri = lax.broadcasted_iota(jnp.int32, (MOBA_BLOCK, MOBA_BLOCK), 0)
    ci = lax.broadcasted_iota(jnp.int32, (MOBA_BLOCK, MOBA_BLOCK), 1)
    s = jnp.where(ci <= ri, _dot_nt(qb, k_own) * scale, NEG)
    m0 = jnp.max(s, axis=-1, keepdims=True)
    p = jnp.exp(s - m0)
    l0 = jnp.sum(p, axis=-1, keepdims=True)
    acc0 = _dot(p, v_own)

    def body(m, carry):
        m_i, l_i, acc = carry
        kb, vb = block(m)
        picked = jnp.sum(jnp.where(blk == m, sel, 0.0), axis=-1, keepdims=True)
        s = jnp.where(picked > 0.0, _dot_nt(qb, kb) * scale, NEG)
        m_new = jnp.maximum(m_i, jnp.max(s, axis=-1, keepdims=True))
        alpha = jnp.exp(m_i - m_new)
        p = jnp.exp(s - m_new)
        return (m_new, alpha * l_i + jnp.sum(p, axis=-1, keepdims=True),
                alpha * acc + _dot(p, vb))

    _, l_f, acc_f = lax.fori_loop(0, n, body, (m0, l0, acc0))
    osc_ref[...] = acc_f / l_f
    n_grp = MOBA_BLOCK // MOBA_QGROUP
    for qg in range(MOBA_QGROUP):
        qh, ql = divmod(qg, N_HEADS)
        o_ref[ql, :, qh * HEAD_DIM:(qh + 1) * HEAD_DIM] = osc_ref[pl.ds(qg, n_grp, stride=MOBA_QGROUP), :]


def _moba_prompt(q, k_pages, v_pages, n_seq, seq_len):
    assert seq_len % MOBA_BLOCK == 0 and MOBA_QGROUP == N_HEADS * N_HEADS
    n_b = seq_len // MOBA_BLOCK
    ps = k_pages.shape[2]
    assert MOBA_BLOCK % ps == 0
    pps = seq_len // ps
    hw = N_HEADS * HEAD_DIM
    n_grp = MOBA_BLOCK // MOBA_QGROUP
    qspec = pl.BlockSpec((MOBA_BLOCK, HEAD_DIM), lambda b, h, n: (b * n_b + n, h))
    kvspec = pl.BlockSpec((pps, None, ps, HEAD_DIM), lambda b, h, n: (b, h, 0, 0))
    out = pl.pallas_call(
        _moba_prompt_kernel,
        grid=(n_seq, N_HEADS, n_b),
        in_specs=[qspec, kvspec, kvspec],
        out_specs=pl.BlockSpec((None, N_HEADS, n_grp, hw), lambda b, h, n: (b, 0, n, h)),
        out_shape=jax.ShapeDtypeStruct((n_seq, N_HEADS, seq_len // MOBA_QGROUP, N_HEADS * hw), F32),
        scratch_shapes=[pltpu.VMEM((n_b, HEAD_DIM), F32), pltpu.VMEM((MOBA_BLOCK, HEAD_DIM), F32)],
        compiler_params=_params("parallel", "parallel", "arbitrary"),
        name="moba_prompt",
    )(q, k_pages, v_pages)
    return out.reshape(n_seq * seq_len, hw)


PAGES_PER_STEP = 16


def _moba_select_kernel(pt_ref, q_ref, *refs):
    page_refs = refs[:PAGES_PER_STEP]
    sel_ref = refs[PAGES_PER_STEP]
    kmean_ref = refs[PAGES_PER_STEP + 1]
    j = pl.program_id(1)
    ps = page_refs[0].shape[1]
    ppb = MOBA_BLOCK // ps
    bps = PAGES_PER_STEP // ppb
    n_blk = kmean_ref.shape[1]
    t_new = q_ref.shape[0]

    for i in range(bps):
        for hd in range(N_HEADS):
            tot = jnp.zeros((1, HEAD_DIM), F32)
            for pp in range(ppb):
                tot = tot + jnp.sum(page_refs[i * ppb + pp][hd], axis=0, keepdims=True)
            kmean_ref[hd, pl.ds(j * bps + i, 1), :] = tot / float(MOBA_BLOCK)

    @pl.when(j == pl.num_programs(1) - 1)
    def _():
        for hd in range(N_HEADS):
            q = q_ref[:, hd * HEAD_DIM:(hd + 1) * HEAD_DIM]
            gsc = lax.dot_general(q, kmean_ref[hd], (((1,), (1,)), ((), ())), precision=HIGHEST,
                                  preferred_element_type=F32)
            blk, rank = _topk_rank(gsc, n_blk)
            lane = lax.broadcasted_iota(jnp.int32, (t_new, LANES), 1)
            out = jnp.zeros((t_new, LANES), jnp.int32)
            for r in range(MOBA_TOPK):
                idx = jnp.sum(jnp.where(rank == r, blk.astype(F32), 0.0), axis=-1, keepdims=True)
                out = jnp.where(lane == r, idx.astype(jnp.int32), out)
            sel_ref[0, hd] = out


def _moba_select(q_s, cache_k, page_table, t_new):
    n_seq, n_pg = page_table.shape
    _, _, _, ps, _ = cache_k.shape
    assert (n_pg * ps) % MOBA_BLOCK == 0, "cached rows must end on a MoBA block boundary"
    assert n_pg % PAGES_PER_STEP == 0 and MOBA_BLOCK % ps == 0
    n_blk = n_pg * ps // MOBA_BLOCK
    assert n_blk >= MOBA_TOPK
    hw = N_HEADS * HEAD_DIM

    def page_spec(i):
        return pl.BlockSpec((None, None, N_HEADS, ps, HEAD_DIM),
                            lambda b, j, pt: (0, pt[b, j * PAGES_PER_STEP + i], 0, 0, 0))

    grid_spec = pltpu.PrefetchScalarGridSpec(
        num_scalar_prefetch=1,
        grid=(n_seq, n_pg // PAGES_PER_STEP),
        in_specs=[pl.BlockSpec((t_new, hw), lambda b, j, pt: (b, 0))]
        + [page_spec(i) for i in range(PAGES_PER_STEP)],
        out_specs=pl.BlockSpec((1, N_HEADS, t_new, LANES), lambda b, j, pt: (b, 0, 0, 0)),
        scratch_shapes=[pltpu.VMEM((N_HEADS, n_blk, HEAD_DIM), F32)],
    )
    return pl.pallas_call(
        _moba_select_kernel,
        grid_spec=grid_spec,
        out_shape=jax.ShapeDtypeStruct((n_seq, N_HEADS, t_new, LANES), jnp.int32),
        compiler_params=_params("parallel", "arbitrary"),
        name="moba_select",
    )(page_table, q_s, *([cache_k] * PAGES_PER_STEP))


def _moba_sample_kernel(pages_ref, q_ref, kn_ref, vn_ref, ck_hbm, cv_hbm, o_ref,
                        kbuf, vbuf, sem, *, n_fetch):
    g = pl.program_id(0)
    n_g = pl.num_programs(0)
    ps = kbuf.shape[2]
    t_new = q_ref.shape[0]
    per_q = n_fetch // t_new * ps
    scale = HEAD_DIM ** -0.5

    def copies(step, slot):
        hd = step % N_HEADS
        out = []
        for i in range(n_fetch):
            pg = pages_ref[step * n_fetch + i]
            out.append(pltpu.make_async_copy(ck_hbm.at[0, pg, hd], kbuf.at[slot, i], sem.at[0, slot]))
            out.append(pltpu.make_async_copy(cv_hbm.at[0, pg, hd], vbuf.at[slot, i], sem.at[1, slot]))
        return out

    @pl.when(g == 0)
    def _():
        for c in copies(0, 0):
            c.start()

    @pl.when(g + 1 < n_g)
    def _():
        for c in copies(g + 1, (g + 1) % 2):
            c.start()

    slot = g % 2
    for c in copies(g, slot):
        c.wait()

    q = q_ref[...]
    pad = jnp.zeros((2 * SUBLANES - t_new, HEAD_DIM), F32)
    qp = jnp.concatenate([q, pad], axis=0).astype(BF16)
    kc = kbuf[slot].reshape(n_fetch * ps, HEAD_DIM)
    vc = vbuf[slot].reshape(n_fetch * ps, HEAD_DIM)
    s = _dot_nt(qp, kc) * scale
    ri = lax.broadcasted_iota(jnp.int32, s.shape, 0)
    ci = lax.broadcasted_iota(jnp.int32, s.shape, 1)
    lo = ri * per_q
    s = jnp.where((ci >= lo) & (ci < lo + per_q), s, NEG)
    kn = jnp.concatenate([kn_ref[...], pad], axis=0)
    vn = jnp.concatenate([vn_ref[...], pad], axis=0)
    s_new = _dot_nt(qp, kn) * scale
    rn = lax.broadcasted_iota(jnp.int32, s_new.shape, 0)
    cn = lax.broadcasted_iota(jnp.int32, s_new.shape, 1)
    s_new = jnp.where((cn <= rn) & (cn < t_new), s_new, NEG)
    m = jnp.maximum(jnp.max(s, axis=-1, keepdims=True), jnp.max(s_new, axis=-1, keepdims=True))
    p = jnp.exp(s - m)
    p_new = jnp.exp(s_new - m)
    den = jnp.sum(p, axis=-1, keepdims=True) + jnp.sum(p_new, axis=-1, keepdims=True)
    o = (_dot(p, vc) + _dot(p_new, vn)) / den
    o_ref[...] = o[:t_new]


def _moba_sample(q_s, k_new, v_new, cache_k, cache_v, phys_pages, n_seq, t_new):
    ps = cache_k.shape[3]
    n_fetch = phys_pages.shape[0] // (n_seq * N_HEADS)
    grid_spec = pltpu.PrefetchScalarGridSpec(
        num_scalar_prefetch=1,
        grid=(n_seq * N_HEADS,),
        in_specs=[
            pl.BlockSpec((t_new, HEAD_DIM), lambda g, pg: (g // N_HEADS, g % N_HEADS)),
            pl.BlockSpec((None, None, t_new, HEAD_DIM), lambda g, pg: (g // N_HEADS, g % N_HEADS, 0, 0)),
            pl.BlockSpec((None, None, t_new, HEAD_DIM), lambda g, pg: (g // N_HEADS, g % N_HEADS, 0, 0)),
            pl.BlockSpec(memory_space=pl.ANY),
            pl.BlockSpec(memory_space=pl.ANY),
        ],
        out_specs=pl.BlockSpec((t_new, HEAD_DIM), lambda g, pg: (g // N_HEADS, g % N_HEADS)),
        scratch_shapes=[
            pltpu.VMEM((2, n_fetch, ps, HEAD_DIM), F32),
            pltpu.VMEM((2, n_fetch, ps, HEAD_DIM), F32),
            pltpu.SemaphoreType.DMA((2, 2)),
        ],
    )
    return pl.pallas_call(
        functools.partial(_moba_sample_kernel, n_fetch=n_fetch),
        grid_spec=grid_spec,
        out_shape=jax.ShapeDtypeStruct(q_s.shape, F32),
        compiler_params=_params("arbitrary"),
        name="moba_sample",
    )(phys_pages, q_s, k_new, v_new, cache_k, cache_v)


def _merge_kernel(x_ref, bh_ref, bm_ref, gate_ref, wlh_ref, wlm_ref, wout_ref, gffn_ref,
                  wr_ref, br_ref, cnt0_ref, x1_ref, h2_ref, ri_ref, rw_ref, cnt_ref, run_ref):
    i = pl.program_id(0)
    tm, d = x_ref.shape

    @pl.when(i == 0)
    def _():
        run_ref[...] = cnt0_ref[...]

    mixed = (gate_ref[:, 0:d] * _dot(bh_ref[...], wlh_ref[...])
             + gate_ref[:, d:2 * d] * _dot(bm_ref[...], wlm_ref[...]))
    x1 = x_ref[...] + _dot(mixed, wout_ref[...])
    x1_ref[...] = x1
    h2 = _rms(x1, gffn_ref[...])
    n_chunk = d // LANES
    for j in range(n_chunk):
        h2_ref[pl.ds(j, tm, stride=n_chunk), :] = h2[:, j * LANES:(j + 1) * LANES]

    logits = jnp.dot(h2, wr_ref[...], precision=HIGHEST, preferred_element_type=F32) + br_ref[...]
    lane_e = lax.broadcasted_iota(jnp.int32, (tm, N_EXPERTS), 1).astype(F32)
    lg = logits
    vals, hots = [], []
    for _ in range(TOP_K):
        mx = jnp.max(lg, axis=-1, keepdims=True)
        idx = jnp.min(jnp.where(lg == mx, lane_e, float(N_EXPERTS)), axis=-1, keepdims=True)
        hot = lane_e == idx
        vals.append(mx)
        hots.append(hot)
        lg = jnp.where(hot, -jnp.inf, lg)
    exps = [jnp.exp(v - vals[0]) for v in vals]
    den = exps[0] + exps[1] + exps[2] + exps[3]

    chosen = jnp.zeros((tm, N_EXPERTS), F32)
    for hot in hots:
        chosen = chosen + jnp.where(hot, 1.0, 0.0)
    r_i = lax.broadcasted_iota(jnp.int32, (tm, tm), 0)
    c_i = lax.broadcasted_iota(jnp.int32, (tm, tm), 1)
    before = jnp.where(c_i < r_i, 1.0, 0.0)
    base = _dot(before, chosen) + run_ref[...]
    run_ref[...] = run_ref[...] + jnp.sum(chosen, axis=0, keepdims=True)
    cnt_ref[...] = run_ref[...]

    lane = lax.broadcasted_iota(jnp.int32, (tm, LANES), 1)
    out_i = jnp.zeros((tm, LANES), jnp.int32)
    out_w = jnp.zeros((tm, LANES), F32)
    for k in range(TOP_K):
        e_k = jnp.sum(jnp.where(hots[k], lane_e, 0.0), axis=-1, keepdims=True).astype(jnp.int32)
        r_k = jnp.sum(jnp.where(hots[k], base, 0.0), axis=-1, keepdims=True).astype(jnp.int32)
        out_i = jnp.where(lane == k, e_k, out_i)
        out_i = jnp.where(lane == TOP_K + k, r_k, out_i)
        out_w = jnp.where(lane == k, exps[k] / den, out_w)
    ri_ref[...] = out_i
    rw_ref[...] = out_w


def _merge(x2d, bh, bm, gates, wlh, wlm, wout, g_ffn, w_router, b_router, cnt0):
    n, d = x2d.shape
    hw = N_HEADS * HEAD_DIM
    tm = _token_tile(n)
    n_chunk = d // LANES
    row = lambda w: pl.BlockSpec((tm, w), lambda i: (i, 0))
    return pl.pallas_call(
        _merge_kernel,
        grid=(n // tm,),
        in_specs=[row(d), row(hw), row(hw), row(2 * d), _full((hw, d)), _full((hw, d)),
                  _full((d, d)), _full((1, d)), _full((d, N_EXPERTS)), _full((1, N_EXPERTS)),
                  _full((1, N_EXPERTS))],
        out_specs=(row(d), pl.BlockSpec((tm * n_chunk, LANES), lambda i: (i, 0)),
                   row(LANES), row(LANES), _full((1, N_EXPERTS))),
        out_shape=(jax.ShapeDtypeStruct((n, d), F32),
                   jax.ShapeDtypeStruct((n * n_chunk, LANES), F32),
                   jax.ShapeDtypeStruct((n, LANES), jnp.int32),
                   jax.ShapeDtypeStruct((n, LANES), F32),
                   jax.ShapeDtypeStruct((1, N_EXPERTS), F32)),
        scratch_shapes=[pltpu.VMEM((1, N_EXPERTS), F32)],
        compiler_params=_params("arbitrary"),
        name="merge_router",
    )(x2d, bh, bm, gates, wlh, wlm, wout, g_ffn.reshape(1, d), w_router,
      b_router.reshape(1, N_EXPERTS), cnt0)


def _dispatch_kernel(dest_ref, h2_hbm, xs_in, xs_hbm, sem, *, tm, rows_per_tok):
    del xs_in
    i = pl.program_id(0)
    r = rows_per_tok

    def body(t, carry):
        src = h2_hbm.at[pl.ds(pl.multiple_of((i * tm + t) * r, r), r)]
        for k in range(TOP_K):
            dst = xs_hbm.at[pl.ds(pl.multiple_of(dest_ref[t * TOP_K + k] * r, r), r)]
            pltpu.make_async_copy(src, dst, sem).start()
        return carry

    lax.fori_loop(0, tm, body, 0)
    n_rows = tm * TOP_K * r
    pltpu.make_async_copy(xs_hbm.at[pl.ds(0, n_rows)], xs_hbm.at[pl.ds(0, n_rows)], sem).wait()


def _dispatch(dest_flat, h2_tiles, xs, rows_per_tok):
    n = dest_flat.shape[0] // TOP_K
    tm = _token_tile(n)
    return pl.pallas_call(
        functools.partial(_dispatch_kernel, tm=tm, rows_per_tok=rows_per_tok),
        grid=(n // tm,),
        in_specs=[pl.BlockSpec((tm * TOP_K,), lambda i: (i,), memory_space=pltpu.SMEM),
                  pl.BlockSpec(memory_space=pl.ANY), pl.BlockSpec(memory_space=pl.ANY)],
        out_specs=pl.BlockSpec(memory_space=pl.ANY),
        out_shape=jax.ShapeDtypeStruct(xs.shape, xs.dtype),
        scratch_shapes=[pltpu.SemaphoreType.DMA(())],
        input_output_aliases={2: 0},
        compiler_params=_params("arbitrary"),
        name="moe_dispatch",
    )(dest_flat, h2_tiles, xs)


def _moe_kernel(te_ref, nu_ref, xs_ref, wgu_ref, bgu_ref, wdn_ref, bdn_ref, ys_ref):
    i = pl.program_id(0)
    n_chunk = xs_ref.shape[0] // MOE_TILE
    d_ff = wdn_ref.shape[0]

    @pl.when(i < nu_ref[0])
    def _():
        x = jnp.concatenate(
            [xs_ref[pl.ds(j, MOE_TILE, stride=n_chunk), :] for j in range(n_chunk)], axis=-1)
        gu = _dot(x, wgu_ref[...]) + bgu_ref[...]
        g_ = jnp.minimum(gu[:, :d_ff], SWIGLU_LIMIT)
        u_ = jnp.clip(gu[:, d_ff:], -SWIGLU_LIMIT, SWIGLU_LIMIT)
        act = (u_ + 1.0) * g_ * _sigmoid(SWIGLU_ALPHA * g_)
        y = _dot(act, wdn_ref[...]) + bdn_ref[...]
        for j in range(n_chunk):
            ys_ref[pl.ds(j, MOE_TILE, stride=n_chunk), :] = y[:, j * LANES:(j + 1) * LANES]

    @pl.when(i >= nu_ref[0])
    def _():
        ys_ref[...] = jnp.zeros(ys_ref.shape, F32)


def _moe(tile_e, n_used, xs, wgu_bf, b_gu, wdn_bf, b_dn, rows_per_tok):
    n_tiles = xs.shape[0] // (MOE_TILE * rows_per_tok)
    d, two_ff = wgu_bf.shape[1:]
    d_ff = wdn_bf.shape[1]
    last = lambda i, nu: jnp.minimum(i, nu[0] - 1)
    grid_spec = pltpu.PrefetchScalarGridSpec(
        num_scalar_prefetch=2,
        grid=(n_tiles,),
        in_specs=[
            pl.BlockSpec((MOE_TILE * rows_per_tok, LANES), lambda i, te, nu: (last(i, nu), 0)),
            pl.BlockSpec((None, d, two_ff), lambda i, te, nu: (te[last(i, nu)], 0, 0)),
            pl.BlockSpec((None, 1, two_ff), lambda i, te, nu: (te[last(i, nu)], 0, 0)),
            pl.BlockSpec((None, d_ff, d), lambda i, te, nu: (te[last(i, nu)], 0, 0)),
            pl.BlockSpec((None, 1, d), lambda i, te, nu: (te[last(i, nu)], 0, 0)),
        ],
        out_specs=pl.BlockSpec((MOE_TILE * rows_per_tok, LANES), lambda i, te, nu: (i, 0)),
    )
    return pl.pallas_call(
        _moe_kernel,
        grid_spec=grid_spec,
        out_shape=jax.ShapeDtypeStruct(xs.shape, F32),
        compiler_params=_params("arbitrary"),
        name="moe_ffn",
    )(tile_e, n_used, xs, wgu_bf, b_gu.reshape(N_EXPERTS, 1, two_ff), wdn_bf,
      b_dn.reshape(N_EXPERTS, 1, d))


def _combine_kernel(dest_ref, ys_hbm, x1_ref, rw_ref, p_ref, gple_ref, wple_ref, wgate_ref,
                    y_ref, buf, sem, *, rows_per_tok):
    tm, d = x1_ref.shape
    r = rows_per_tok

    def body(t, carry):
        for k in range(TOP_K):
            src = ys_hbm.at[pl.ds(pl.multiple_of(dest_ref[t * TOP_K + k] * r, r), r)]
            dst = buf.at[pl.ds(pl.multiple_of((k * tm + t) * r, r), r)]
            pltpu.make_async_copy(src, dst, sem).start()
        return carry

    lax.fori_loop(0, tm, body, 0)
    pe = _dot(p_ref[...], wple_ref[...])
    pltpu.make_async_copy(buf, buf, sem).wait()

    rw = rw_ref[...]
    cols = []
    for j in range(r):
        acc = jnp.zeros((tm, LANES), F32)
        for k in range(TOP_K):
            acc = acc + rw[:, k:k + 1] * buf[pl.ds(k * tm * r + j, tm, stride=r), :]
        cols.append(acc)
    x2 = x1_ref[...] + jnp.concatenate(cols, axis=-1)
    gate = _sigmoid(_dot(_rms(x2, gple_ref[...]), wgate_ref[...]))
    y_ref[...] = x2 + pe * gate


def _combine(dest_flat, ys, x1, rw, p2d, g_ple, wple_bf, wgate_bf, rows_per_tok):
    n, d = x1.shape
    tm = _token_tile(n)
    ple = p2d.shape[1]
    row = lambda w: pl.BlockSpec((tm, w), lambda i: (i, 0))
    return pl.pallas_call(
        functools.partial(_combine_kernel, rows_per_tok=rows_per_tok),
        grid=(n // tm,),
        in_specs=[pl.BlockSpec((tm * TOP_K,), lambda i: (i,), memory_space=pltpu.SMEM),
                  pl.BlockSpec(memory_space=pl.ANY), row(d), row(LANES), row(ple),
                  _full((1, d)), _full((ple, d)), _full((d, d))],
        out_specs=row(d),
        out_shape=jax.ShapeDtypeStruct((n, d), F32),
        scratch_shapes=[pltpu.VMEM((TOP_K * tm * rows_per_tok, LANES), F32),
                        pltpu.SemaphoreType.DMA(())],
        compiler_params=_params("arbitrary"),
        name="combine_ple",
    )(dest_flat, ys, x1, rw, p2d, g_ple.reshape(1, d), wple_bf, wgate_bf)


def kernel(x_prompt, x_sample, cache_k, cache_v, state_hgrn, page_table, p_prompt, p_sample,
           g_mix, w_in, lb_logits, g_hg_out, g_q, g_k, w_lift_hgrn, w_lift_moba, w_out, g_ffn,
           w_router, b_router, w_gate_up, b_gate_up, w_down, b_down, g_ple, w_ple, w_ple_gate):
    depth = g_mix.shape[0]
    assert depth == 1, "single-layer step"
    li = 0
    n_b, seq, d = x_prompt.shape
    n_db, t_new, _ = x_sample.shape
    ps = cache_k.shape[3]
    rows_per_tok = d // LANES
    assert rows_per_tok == SUBLANES and seq % ps == 0

    lb = jnp.cumsum(jax.nn.softmax(lb_logits.astype(F32), axis=0), axis=0)[li]
    w_in_bf = w_in[li].astype(BF16)
    wlh, wlm, wout = (w_lift_hgrn[li].astype(BF16), w_lift_moba[li].astype(BF16),
                      w_out[li].astype(BF16))
    wgu_bf, wdn_bf = w_gate_up[li].astype(BF16), w_down[li].astype(BF16)
    wple_bf, wgate_bf = w_ple[li].astype(BF16), w_ple_gate[li].astype(BF16)

    xp = x_prompt.reshape(n_b * seq, d)
    xs_ = x_sample.reshape(n_db * t_new, d)

    zh_p, q_p, k_p, v_p, gates_p = _inproj(xp, g_mix[li], w_in_bf, lb, g_q[li], g_k[li], ps)
    bh_p, st_p = _hgrn(zh_p, None, g_hg_out[li], n_b, seq)
    bm_p = _moba_prompt(q_p, k_p, v_p, n_b, seq)

    zh_s, q_s, k_s, v_s, gates_s = _inproj(xs_, g_mix[li], w_in_bf, lb, g_q[li], g_k[li], t_new)
    bh_s, st_s = _hgrn(zh_s, state_hgrn[li], g_hg_out[li], n_db, t_new)
    sel = _moba_select(q_s, cache_k, page_table, t_new)[..., :MOBA_TOPK]
    ppb = MOBA_BLOCK // ps
    logical = sel[..., None] * ppb + jnp.arange(ppb, dtype=jnp.int32)
    phys = jnp.take_along_axis(page_table, logical.reshape(n_db, -1), axis=1).reshape(-1)
    bm_s = _moba_sample(q_s, k_s, v_s, cache_k, cache_v, phys, n_db, t_new)

    cnt0 = jnp.zeros((1, N_EXPERTS), F32)
    x1_p, h2_p, ri_p, rw_p, cnt_p = _merge(xp, bh_p, bm_p, gates_p, wlh, wlm, wout, g_ffn[li],
                                           w_router[li], b_router[li], cnt0)
    x1_s, h2_s, ri_s, rw_s, cnt_all = _merge(xs_, bh_s, bm_s, gates_s, wlh, wlm, wout, g_ffn[li],
                                             w_router[li], b_router[li], cnt_p)

    counts = cnt_all[0].astype(jnp.int32)
    padded = (counts + MOE_TILE - 1) // MOE_TILE * MOE_TILE
    pad_end = jnp.cumsum(padded)
    pad_start = pad_end - padded
    n_assign = (n_b * seq + n_db * t_new) * TOP_K
    n_tiles = -(-(n_assign + N_EXPERTS * (MOE_TILE - 1)) // MOE_TILE)
    tile_e = jnp.minimum(jnp.searchsorted(pad_end, jnp.arange(n_tiles, dtype=jnp.int32) * MOE_TILE,
                                          side='right'), N_EXPERTS - 1).astype(jnp.int32)
    n_used = (pad_end[-1:] // MOE_TILE).astype(jnp.int32)

    def dest_of(ri):
        return (pad_start[ri[:, :TOP_K]] + ri[:, TOP_K:2 * TOP_K]).reshape(-1).astype(jnp.int32)

    dest_p, dest_s = dest_of(ri_p), dest_of(ri_s)
    xs = jnp.zeros((n_tiles * MOE_TILE * rows_per_tok, LANES), F32)
    xs = _dispatch(dest_p, h2_p, xs, rows_per_tok)
    xs = _dispatch(dest_s, h2_s, xs, rows_per_tok)
    ys = _moe(tile_e, n_used, xs, wgu_bf, b_gate_up[li], wdn_bf, b_down[li], rows_per_tok)

    y_p = _combine(dest_p, ys, x1_p, rw_p, p_prompt[li].reshape(n_b * seq, -1), g_ple[li],
                   wple_bf, wgate_bf, rows_per_tok)
    y_s = _combine(dest_s, ys, x1_s, rw_s, p_sample[li].reshape(n_db * t_new, -1), g_ple[li],
                   wple_bf, wgate_bf, rows_per_tok)

    return (y_p.reshape(n_b, seq, d), y_s.reshape(n_db, t_new, d),
            k_p.reshape(1, n_b, seq // ps, N_HEADS, ps, HEAD_DIM),
            v_p.reshape(1, n_b, seq // ps, N_HEADS, ps, HEAD_DIM),
            k_s[None], v_s[None], st_p[None], st_s[None])
```

```python
import functools

import jax
import jax.numpy as jnp
from jax import lax
from jax.experimental import pallas as pl
from jax.experimental.pallas import tpu as pltpu

F32 = jnp.float32
BF16 = jnp.bfloat16
HIGHEST = lax.Precision.HIGHEST

EPS = 1e-6
N_HEADS = 4
HEAD_DIM = 128
HG_CHUNK = 64
MOBA_BLOCK = 256
MOBA_TOPK = 3
MOBA_QGROUP = 16
N_EXPERTS = 32
TOP_K = 4
SWIGLU_LIMIT = 7.0
SWIGLU_ALPHA = 1.702
NEG = -1e30

LANES = 128
SUBLANES = 8
TOKEN_TILE = 256
MOE_TILE = 512
VMEM_LIMIT = 56 * 1024 * 1024


def _params(*sem):
    return pltpu.CompilerParams(dimension_semantics=sem, vmem_limit_bytes=VMEM_LIMIT)


def _sigmoid(x):
    return 1.0 / (1.0 + jnp.exp(-x))


def _silu(x):
    return x * _sigmoid(x)


def _rms(x, g):
    return x * lax.rsqrt(jnp.mean(x * x, axis=-1, keepdims=True) + EPS) * g


def _dot(a, b):
    return jnp.dot(a.astype(BF16), b.astype(BF16), preferred_element_type=F32)


def _dot_nt(a, b):
    return lax.dot_general(a.astype(BF16), b.astype(BF16), (((1,), (1,)), ((), ())),
                           preferred_element_type=F32)


def _dot_tn(a, b):
    return lax.dot_general(a.astype(BF16), b.astype(BF16), (((0,), (0,)), ((), ())),
                           preferred_element_type=F32)


def _token_tile(n):
    return TOKEN_TILE if n % TOKEN_TILE == 0 else n


def _full(shape):
    return pl.BlockSpec(shape, lambda *_: (0,) * len(shape))


def _inproj_kernel(x_ref, gmix_ref, w_ref, lb_ref, gq_ref, gk_ref,
                   zh_ref, q_ref, k_ref, v_ref, gate_ref):
    x = x_ref[...]
    h = _rms(x, gmix_ref[...]).astype(BF16)
    hw = N_HEADS * HEAD_DIM
    d = x.shape[-1]
    tm = x.shape[0]

    def proj(c0, width):
        return jnp.dot(h, w_ref[:, c0:c0 + width], preferred_element_type=F32)

    lb = lb_ref[...]
    zh_ref[:, 0:hw] = _silu(proj(0, hw))
    fa = proj(hw, hw)
    zh_ref[:, hw:2 * hw] = jnp.log(lb + (1.0 - lb) * _sigmoid(fa))
    zh_ref[:, 2 * hw:3 * hw] = (1.0 - lb) * _sigmoid(-fa)
    zh_ref[:, 3 * hw:4 * hw] = proj(2 * hw, hw)
    zh_ref[:, 4 * hw:5 * hw] = _silu(proj(3 * hw, hw))

    qb = proj(4 * hw, hw)
    kb = proj(5 * hw, hw)
    vb = proj(6 * hw, hw)
    rows = k_ref.shape[2]
    for hd in range(N_HEADS):
        sl = slice(hd * HEAD_DIM, (hd + 1) * HEAD_DIM)
        q_ref[:, sl] = _rms(qb[:, sl], gq_ref[...])
        k_ref[:, hd, :, :] = _rms(kb[:, sl], gk_ref[...]).reshape(tm // rows, rows, HEAD_DIM)
        v_ref[:, hd, :, :] = vb[:, sl].reshape(tm // rows, rows, HEAD_DIM)

    gate_ref[:, 0:d] = _sigmoid(proj(7 * hw, d))
    gate_ref[:, d:2 * d] = _sigmoid(proj(7 * hw + d, d))


def _inproj(x2d, g_mix, w_in_bf, lb, g_q, g_k, rows_per_page):
    n, d = x2d.shape
    hw = N_HEADS * HEAD_DIM
    tm = _token_tile(n)
    r = rows_per_page
    n_in = w_in_bf.shape[1]
    out_shape = (
        jax.ShapeDtypeStruct((n, 5 * hw), F32),
        jax.ShapeDtypeStruct((n, hw), F32),
        jax.ShapeDtypeStruct((n // r, N_HEADS, r, HEAD_DIM), F32),
        jax.ShapeDtypeStruct((n // r, N_HEADS, r, HEAD_DIM), F32),
        jax.ShapeDtypeStruct((n, 2 * d), F32),
    )
    row = lambda w: pl.BlockSpec((tm, w), lambda i: (i, 0))
    page = pl.BlockSpec((tm // r, N_HEADS, r, HEAD_DIM), lambda i: (i, 0, 0, 0))
    return pl.pallas_call(
        _inproj_kernel,
        grid=(n // tm,),
        in_specs=[row(d), _full((1, d)),
                  pl.BlockSpec((d, n_in), lambda i: (0, 0), pipeline_mode=pl.Buffered(1)),
                  _full((1, hw)), _full((1, HEAD_DIM)), _full((1, HEAD_DIM))],
        out_specs=(row(5 * hw), row(hw), page, page, row(2 * d)),
        out_shape=out_shape,
        compiler_params=_params("parallel"),
        name="inproj",
    )(x2d, g_mix.reshape(1, d), w_in_bf, lb.reshape(1, hw), g_q.reshape(1, HEAD_DIM),
      g_k.reshape(1, HEAD_DIM))


def _hgrn_kernel(*refs, chunk, has_s0, exact_mxu):
    if has_s0:
        zh_ref, s0_ref, g_ref, o_ref, sout_ref, st_ref = refs
    else:
        zh_ref, g_ref, o_ref, sout_ref, st_ref = refs
        s0_ref = None
    t = pl.program_id(1)
    hw = N_HEADS * HEAD_DIM
    tb = zh_ref.shape[0]

    @pl.when(t == 0)
    def _():
        for hd in range(N_HEADS):
            if has_s0:
                st_ref[hd] = s0_ref[0, hd].T
            else:
                st_ref[hd] = jnp.zeros((HEAD_DIM, HEAD_DIM), F32)

    if exact_mxu:
        mm_nt = lambda a, b: lax.dot_general(a, b, (((1,), (1,)), ((), ())), precision=HIGHEST,
                                             preferred_element_type=F32)
        mm = lambda a, b: jnp.dot(a, b, precision=HIGHEST, preferred_element_type=F32)
        mm_tn = lambda a, b: lax.dot_general(a, b, (((0,), (0,)), ((), ())), precision=HIGHEST,
                                             preferred_element_type=F32)
    else:
        mm_nt, mm, mm_tn = _dot_nt, _dot, _dot_tn

    ri = lax.broadcasted_iota(jnp.int32, (chunk, chunk), 0)
    ci = lax.broadcasted_iota(jnp.int32, (chunk, chunk), 1)
    causal = ci <= ri
    tril = jnp.where(causal, 1.0, 0.0).astype(F32)

    sts = [st_ref[hd] for hd in range(N_HEADS)]
    for c in range(tb // chunk):
        rows = slice(c * chunk, (c + 1) * chunk)
        k_all = zh_ref[rows, 2 * hw:3 * hw]
        b_all = jnp.dot(tril, zh_ref[rows, hw:2 * hw], precision=HIGHEST,
                        preferred_element_type=F32)
        b_end = b_all[chunk - 1:chunk, :]
        qd_all = zh_ref[rows, 0:hw] * jnp.exp(b_all)
        kd_all = k_all * jnp.exp(-b_all)
        kk_all = k_all * jnp.exp(b_end - b_all)
        decay = jnp.exp(b_end)
        for hd in range(N_HEADS):
            sl = slice(hd * HEAD_DIM, (hd + 1) * HEAD_DIM)
            v = zh_ref[rows, 3 * hw + hd * HEAD_DIM:3 * hw + (hd + 1) * HEAD_DIM]
            og = zh_ref[rows, 4 * hw + hd * HEAD_DIM:4 * hw + (hd + 1) * HEAD_DIM]
            qd = qd_all[:, sl]
            att = jnp.where(causal, mm_nt(qd, kd_all[:, sl]), 0.0)
            o = mm(att, v) + mm_nt(qd, sts[hd])
            sts[hd] = decay[:, sl] * sts[hd] + mm_tn(v, kk_all[:, sl])
            o_ref[rows, sl] = _rms(o, g_ref[...]) * og
    for hd in range(N_HEADS):
        st_ref[hd] = sts[hd]

    @pl.when(t == pl.num_programs(1) - 1)
    def _():
        for hd in range(N_HEADS):
            sout_ref[0, hd] = st_ref[hd].T


def _hgrn(zh, s0, g_hg_out, n_seq, seq_len):
    hw = N_HEADS * HEAD_DIM
    chunk = min(HG_CHUNK, seq_len)
    assert seq_len % chunk == 0
    tb = min(TOKEN_TILE, seq_len)
    assert seq_len % tb == 0 and tb % chunk == 0
    n_t = seq_len // tb
    has_s0 = s0 is not None
    state_spec = pl.BlockSpec((1, N_HEADS, HEAD_DIM, HEAD_DIM), lambda b, t: (b, 0, 0, 0))
    in_specs = [pl.BlockSpec((tb, 5 * hw), lambda b, t: (b * n_t + t, 0))]
    args = [zh]
    if has_s0:
        in_specs.append(state_spec)
        args.append(s0)
    in_specs.append(_full((1, HEAD_DIM)))
    args.append(g_hg_out.reshape(1, HEAD_DIM))
    return pl.pallas_call(
        functools.partial(_hgrn_kernel, chunk=chunk, has_s0=has_s0, exact_mxu=chunk < 16),
        grid=(n_seq, n_t),
        in_specs=in_specs,
        out_specs=(pl.BlockSpec((tb, hw), lambda b, t: (b * n_t + t, 0)), state_spec),
        out_shape=(jax.ShapeDtypeStruct((n_seq * seq_len, hw), F32),
                   jax.ShapeDtypeStruct((n_seq, N_HEADS, HEAD_DIM, HEAD_DIM), F32)),
        scratch_shapes=[pltpu.VMEM((N_HEADS, HEAD_DIM, HEAD_DIM), F32)],
        compiler_params=_params("parallel", "arbitrary"),
        name="hgrn_prompt" if not has_s0 else "hgrn_sample",
    )(*args)


def _topk_rank(gsc, n_valid):
    rows, n_col = gsc.shape
    blk = lax.broadcasted_iota(jnp.int32, (rows, n_col), 1)
    rank = jnp.zeros((rows, n_col), jnp.int32)
    for m in range(n_col):
        sm = gsc[:, m:m + 1]
        beats = jnp.where(sm > gsc, 1, jnp.where(sm == gsc, jnp.where(blk > m, 1, 0), 0))
        rank = rank + jnp.where(m < n_valid, beats, 0)
    return blk, rank


def _moba_prompt_kernel(q_ref, k_ref, v_ref, o_ref, kmean_ref, osc_ref, perm_ref):
    n = pl.program_id(1)
    n_b = kmean_ref.shape[1]
    ppb = MOBA_BLOCK // k_ref.shape[2]
    scale = HEAD_DIM ** -0.5
    heads = range(N_HEADS)

    @pl.when(n == 0)
    def _():
        for hd in heads:
            for m in range(n_b):
                kb = k_ref[m * ppb:(m + 1) * ppb, hd].reshape(MOBA_BLOCK, HEAD_DIM)
                kmean_ref[hd, m:m + 1, :] = jnp.mean(kb, axis=0, keepdims=True)

    def block(m, hd):
        kb = k_ref[pl.ds(m * ppb, ppb), hd].reshape(MOBA_BLOCK, HEAD_DIM)
        vb = v_ref[pl.ds(m * ppb, ppb), hd].reshape(MOBA_BLOCK, HEAD_DIM)
        return kb, vb

    ri = lax.broadcasted_iota(jnp.int32, (MOBA_BLOCK, MOBA_BLOCK), 0)
    ci = lax.broadcasted_iota(jnp.int32, (MOBA_BLOCK, MOBA_BLOCK), 1)
    qbs, sels, init = [], [], []
    for hd in heads:
        q = q_ref[:, hd * HEAD_DIM:(hd + 1) * HEAD_DIM]
        gsc = lax.dot_general(q, kmean_ref[hd], (((1,), (1,)), ((), ())), precision=HIGHEST,
                              preferred_element_type=F32)
        blk, rank = _topk_rank(gsc, n)
        sels.append(jnp.where((blk < n) & (rank < MOBA_TOPK), 1.0, 0.0))
        qb = (q * scale).astype(BF16)
        qbs.append(qb)
        k_own, v_own = block(n, hd)
        s = jnp.where(ci <= ri, _dot_nt(qb, k_own), NEG)
        m0 = jnp.max(s, axis=-1, keepdims=True)
        p = jnp.exp(s - m0)
        init.append((m0, jnp.sum(p, axis=-1, keepdims=True), _dot(p, v_own)))
    blk = lax.broadcasted_iota(jnp.int32, (MOBA_BLOCK, n_b), 1)

    def body(m, carry):
        out = []
        for hd in heads:
            m_i, l_i, acc = carry[hd]
            kb, vb = block(m, hd)
            picked = jnp.sum(jnp.where(blk == m, sels[hd], 0.0), axis=-1, keepdims=True)
            s = jnp.where(picked > 0.0, _dot_nt(qbs[hd], kb), NEG)
            m_new = jnp.maximum(m_i, jnp.max(s, axis=-1, keepdims=True))
            alpha = jnp.exp(m_i - m_new)
            p = jnp.exp(s - m_new)
            out.append((m_new, alpha * l_i + jnp.sum(p, axis=-1, keepdims=True),
                        alpha * acc + _dot(p, vb)))
        return tuple(out)

    final = lax.fori_loop(0, n, body, tuple(init))
    n_grp = MOBA_BLOCK // MOBA_QGROUP
    for hd in heads:
        _, l_f, acc_f = final[hd]
        osc_ref[hd] = acc_f / l_f
        for qg in range(MOBA_QGROUP):
            perm_ref[qg, pl.ds(hd, n_grp, stride=N_HEADS), :] = (
                osc_ref[hd, pl.ds(qg, n_grp, stride=MOBA_QGROUP), :])
    for qg in range(MOBA_QGROUP):
        qh, ql = divmod(qg, N_HEADS)
        o_ref[ql, :, qh * HEAD_DIM:(qh + 1) * HEAD_DIM] = perm_ref[qg]


def _moba_prompt(q, k_pages, v_pages, n_seq, seq_len):
    assert seq_len % MOBA_BLOCK == 0 and MOBA_QGROUP == N_HEADS * N_HEADS
    n_b = seq_len // MOBA_BLOCK
    ps = k_pages.shape[2]
    assert MOBA_BLOCK % ps == 0
    pps = seq_len // ps
    hw = N_HEADS * HEAD_DIM
    n_grp = MOBA_BLOCK // MOBA_QGROUP
    qspec = pl.BlockSpec((MOBA_BLOCK, hw), lambda b, n: (b * n_b + n, 0))
    kvspec = pl.BlockSpec((pps, N_HEADS, ps, HEAD_DIM), lambda b, n: (b, 0, 0, 0))
    out = pl.pallas_call(
        _moba_prompt_kernel,
        grid=(n_seq, n_b),
        in_specs=[qspec, kvspec, kvspec],
        out_specs=pl.BlockSpec((None, N_HEADS, n_grp * N_HEADS, hw), lambda b, n: (b, 0, n, 0)),
        out_shape=jax.ShapeDtypeStruct((n_seq, N_HEADS, seq_len // N_HEADS, hw), F32),
        scratch_shapes=[pltpu.VMEM((N_HEADS, n_b, HEAD_DIM), F32),
                        pltpu.VMEM((N_HEADS, MOBA_BLOCK, HEAD_DIM), F32),
                        pltpu.VMEM((MOBA_QGROUP, n_grp * N_HEADS, HEAD_DIM), F32)],
        compiler_params=_params("parallel", "arbitrary"),
        name="moba_prompt",
    )(q, k_pages, v_pages)
    return out.reshape(n_seq * seq_len, hw)


PAGES_PER_STEP = 16


def _moba_select_kernel(pt_ref, q_ref, *refs):
    page_refs = refs[:PAGES_PER_STEP]
    sel_ref = refs[PAGES_PER_STEP]
    kmean_ref = refs[PAGES_PER_STEP + 1]
    j = pl.program_id(1)
    ps = page_refs[0].shape[1]
    ppb = MOBA_BLOCK // ps
    bps = PAGES_PER_STEP // ppb
    n_blk = kmean_ref.shape[1]
    t_new = q_ref.shape[0]

    for i in range(bps):
        for hd in range(N_HEADS):
            tot = jnp.zeros((1, HEAD_DIM), F32)
            for pp in range(ppb):
                tot = tot + jnp.sum(page_refs[i * ppb + pp][hd], axis=0, keepdims=True)
            kmean_ref[hd, pl.ds(j * bps + i, 1), :] = tot / float(MOBA_BLOCK)

    @pl.when(j == pl.num_programs(1) - 1)
    def _():
        for hd in range(N_HEADS):
            q = q_ref[:, hd * HEAD_DIM:(hd + 1) * HEAD_DIM]
            gsc = lax.dot_general(q, kmean_ref[hd], (((1,), (1,)), ((), ())), precision=HIGHEST,
                                  preferred_element_type=F32)
            blk, rank = _topk_rank(gsc, n_blk)
            lane = lax.broadcasted_iota(jnp.int32, (t_new, LANES), 1)
            out = jnp.zeros((t_new, LANES), jnp.int32)
            for r in range(MOBA_TOPK):
                idx = jnp.sum(jnp.where(rank == r, blk.astype(F32), 0.0), axis=-1, keepdims=True)
                out = jnp.where(lane == r, idx.astype(jnp.int32), out)
            sel_ref[0, hd] = out


def _moba_select(q_s, cache_k, page_table, t_new):
    n_seq, n_pg = page_table.shape
    _, _, _, ps, _ = cache_k.shape
    assert (n_pg * ps) % MOBA_BLOCK == 0, "cached rows must end on a MoBA block boundary"
    assert n_pg % PAGES_PER_STEP == 0 and MOBA_BLOCK % ps == 0
    n_blk = n_pg * ps // MOBA_BLOCK
    assert n_blk >= MOBA_TOPK
    hw = N_HEADS * HEAD_DIM

    def page_spec(i):
        return pl.BlockSpec((None, None, N_HEADS, ps, HEAD_DIM),
                            lambda b, j, pt: (0, pt[b, j * PAGES_PER_STEP + i], 0, 0, 0))

    grid_spec = pltpu.PrefetchScalarGridSpec(
        num_scalar_prefetch=1,
        grid=(n_seq, n_pg // PAGES_PER_STEP),
        in_specs=[pl.BlockSpec((t_new, hw), lambda b, j, pt: (b, 0))]
        + [page_spec(i) for i in range(PAGES_PER_STEP)],
        out_specs=pl.BlockSpec((1, N_HEADS, t_new, LANES), lambda b, j, pt: (b, 0, 0, 0)),
        scratch_shapes=[pltpu.VMEM((N_HEADS, n_blk, HEAD_DIM), F32)],
    )
    return pl.pallas_call(
        _moba_select_kernel,
        grid_spec=grid_spec,
        out_shape=jax.ShapeDtypeStruct((n_seq, N_HEADS, t_new, LANES), jnp.int32),
        compiler_params=_params("parallel", "arbitrary"),
        name="moba_select",
    )(page_table, q_s, *([cache_k] * PAGES_PER_STEP))


def _moba_sample_kernel(pages_ref, q_ref, kn_ref, vn_ref, ck_hbm, cv_hbm, o_ref,
                        kbuf, vbuf, sem, *, n_fetch):
    g = pl.program_id(0)
    n_g = pl.num_programs(0)
    ps = kbuf.shape[2]
    t_new = q_ref.shape[0]
    per_q = n_fetch // t_new * ps
    scale = HEAD_DIM ** -0.5

    def copies(step, slot):
        hd = step % N_HEADS
        out = []
        for i in range(n_fetch):
            pg = pages_ref[step * n_fetch + i]
            out.append(pltpu.make_async_copy(ck_hbm.at[0, pg, hd], kbuf.at[slot, i], sem.at[0, slot]))
            out.append(pltpu.make_async_copy(cv_hbm.at[0, pg, hd], vbuf.at[slot, i], sem.at[1, slot]))
        return out

    @pl.when(g == 0)
    def _():
        for c in copies(0, 0):
            c.start()

    @pl.when(g + 1 < n_g)
    def _():
        for c in copies(g + 1, (g + 1) % 2):
            c.start()

    slot = g % 2
    for c in copies(g, slot):
        c.wait()

    q = q_ref[...]
    pad = jnp.zeros((2 * SUBLANES - t_new, HEAD_DIM), F32)
    qp = jnp.concatenate([q, pad], axis=0).astype(BF16)
    kc = kbuf[slot].reshape(n_fetch * ps, HEAD_DIM)
    vc = vbuf[slot].reshape(n_fetch * ps, HEAD_DIM)
    s = _dot_nt(qp, kc) * scale
    ri = lax.broadcasted_iota(jnp.int32, s.shape, 0)
    ci = lax.broadcasted_iota(jnp.int32, s.shape, 1)
    lo = ri * per_q
    s = jnp.where((ci >= lo) & (ci < lo + per_q), s, NEG)
    kn = jnp.concatenate([kn_ref[...], pad], axis=0)
    vn = jnp.concatenate([vn_ref[...], pad], axis=0)
    s_new = _dot_nt(qp, kn) * scale
    rn = lax.broadcasted_iota(jnp.int32, s_new.shape, 0)
    cn = lax.broadcasted_iota(jnp.int32, s_new.shape, 1)
    s_new = jnp.where((cn <= rn) & (cn < t_new), s_new, NEG)
    m = jnp.maximum(jnp.max(s, axis=-1, keepdims=True), jnp.max(s_new, axis=-1, keepdims=True))
    p = jnp.exp(s - m)
    p_new = jnp.exp(s_new - m)
    den = jnp.sum(p, axis=-1, keepdims=True) + jnp.sum(p_new, axis=-1, keepdims=True)
    o = (_dot(p, vc) + _dot(p_new, vn)) / den
    o_ref[...] = o[:t_new]


def _moba_sample(q_s, k_new, v_new, cache_k, cache_v, phys_pages, n_seq, t_new):
    ps = cache_k.shape[3]
    n_fetch = phys_pages.shape[0] // (n_seq * N_HEADS)
    grid_spec = pltpu.PrefetchScalarGridSpec(
        num_scalar_prefetch=1,
        grid=(n_seq * N_HEADS,),
        in_specs=[
            pl.BlockSpec((t_new, HEAD_DIM), lambda g, pg: (g // N_HEADS, g % N_HEADS)),
            pl.BlockSpec((None, None, t_new, HEAD_DIM), lambda g, pg: (g // N_HEADS, g % N_HEADS, 0, 0)),
            pl.BlockSpec((None, None, t_new, HEAD_DIM), lambda g, pg: (g // N_HEADS, g % N_HEADS, 0, 0)),
            pl.BlockSpec(memory_space=pl.ANY),
            pl.BlockSpec(memory_space=pl.ANY),
        ],
        out_specs=pl.BlockSpec((t_new, HEAD_DIM), lambda g, pg: (g // N_HEADS, g % N_HEADS)),
        scratch_shapes=[
            pltpu.VMEM((2, n_fetch, ps, HEAD_DIM), F32),
            pltpu.VMEM((2, n_fetch, ps, HEAD_DIM), F32),
            pltpu.SemaphoreType.DMA((2, 2)),
        ],
    )
    return pl.pallas_call(
        functools.partial(_moba_sample_kernel, n_fetch=n_fetch),
        grid_spec=grid_spec,
        out_shape=jax.ShapeDtypeStruct(q_s.shape, F32),
        compiler_params=_params("arbitrary"),
        name="moba_sample",
    )(phys_pages, q_s, k_new, v_new, cache_k, cache_v)


def _merge_kernel(x_ref, bh_ref, bm_ref, gate_ref, wlh_ref, wlm_ref, wout_ref, gffn_ref,
                  wr_ref, br_ref, cnt0_ref, x1_ref, h2_ref, ri_ref, rw_ref, cnt_ref, run_ref):
    i = pl.program_id(0)
    tm, d = x_ref.shape

    @pl.when(i == 0)
    def _():
        run_ref[...] = cnt0_ref[...]

    mixed = (gate_ref[:, 0:d] * _dot(bh_ref[...], wlh_ref[...])
             + gate_ref[:, d:2 * d] * _dot(bm_ref[...], wlm_ref[...]))
    x1 = x_ref[...] + _dot(mixed, wout_ref[...])
    x1_ref[...] = x1
    h2 = _rms(x1, gffn_ref[...])
    n_chunk = d // LANES
    for j in range(n_chunk):
        h2_ref[pl.ds(j, tm, stride=n_chunk), :] = h2[:, j * LANES:(j + 1) * LANES]

    logits = lax.dot_general(wr_ref[...], h2, (((1,), (1,)), ((), ())), precision=HIGHEST,
                             preferred_element_type=F32) + br_ref[...]
    sub_e = lax.broadcasted_iota(jnp.int32, (N_EXPERTS, tm), 0).astype(F32)
    lg = logits
    vals, ids, hots = [], [], []
    for _ in range(TOP_K):
        mx = jnp.max(lg, axis=0, keepdims=True)
        idx = jnp.min(jnp.where(lg == mx, sub_e, float(N_EXPERTS)), axis=0, keepdims=True)
        hot = sub_e == idx
        vals.append(mx)
        ids.append(idx)
        hots.append(hot)
        lg = jnp.where(hot, -jnp.inf, lg)
    exps = [jnp.exp(v - vals[0]) for v in vals]
    den = exps[0] + exps[1] + exps[2] + exps[3]

    chosen = jnp.zeros((N_EXPERTS, tm), F32)
    for hot in hots:
        chosen = chosen + jnp.where(hot, 1.0, 0.0)
    r_i = lax.broadcasted_iota(jnp.int32, (tm, tm), 0)
    c_i = lax.broadcasted_iota(jnp.int32, (tm, tm), 1)
    earlier = jnp.where(r_i < c_i, 1.0, 0.0)
    base = _dot(chosen, earlier) + run_ref[:, 0:1]
    run_ref[...] = run_ref[...] + jnp.sum(chosen, axis=1, keepdims=True)
    cnt_ref[...] = run_ref[...]

    sub_i = lax.broadcasted_iota(jnp.int32, (2 * TOP_K, tm), 0)
    sub_w = lax.broadcasted_iota(jnp.int32, (LANES, tm), 0)
    out_i = jnp.zeros((2 * TOP_K, tm), jnp.int32)
    out_w = jnp.zeros((LANES, tm), F32)
    for k in range(TOP_K):
        r_k = jnp.sum(jnp.where(hots[k], base, 0.0), axis=0, keepdims=True)
        out_i = jnp.where(sub_i == k, ids[k].astype(jnp.int32), out_i)
        out_i = jnp.where(sub_i == TOP_K + k, r_k.astype(jnp.int32), out_i)
        out_w = jnp.where(sub_w == k, exps[k] / den, out_w)
    ri_ref[...] = out_i
    rw_ref[...] = out_w.T


def _merge(x2d, bh, bm, gates, wlh, wlm, wout, g_ffn, w_router, b_router, cnt0):
    n, d = x2d.shape
    hw = N_HEADS * HEAD_DIM
    tm = _token_tile(n)
    n_chunk = d // LANES
    row = lambda w: pl.BlockSpec((tm, w), lambda i: (i, 0))
    return pl.pallas_call(
        _merge_kernel,
        grid=(n // tm,),
        in_specs=[row(d), row(hw), row(hw), row(2 * d), _full((hw, d)), _full((hw, d)),
                  _full((d, d)), _full((1, d)), _full((N_EXPERTS, d)), _full((N_EXPERTS, 1)),
                  _full((N_EXPERTS, LANES))],
        out_specs=(row(d), pl.BlockSpec((tm * n_chunk, LANES), lambda i: (i, 0)),
                   pl.BlockSpec((2 * TOP_K, tm), lambda i: (0, i)), row(LANES),
                   _full((N_EXPERTS, LANES))),
        out_shape=(jax.ShapeDtypeStruct((n, d), F32),
                   jax.ShapeDtypeStruct((n * n_chunk, LANES), F32),
                   jax.ShapeDtypeStruct((2 * TOP_K, n), jnp.int32),
                   jax.ShapeDtypeStruct((n, LANES), F32),
                   jax.ShapeDtypeStruct((N_EXPERTS, LANES), F32)),
        scratch_shapes=[pltpu.VMEM((N_EXPERTS, LANES), F32)],
        compiler_params=_params("arbitrary"),
        name="merge_router",
    )(x2d, bh, bm, gates, wlh, wlm, wout, g_ffn.reshape(1, d), w_router.T,
      b_router.reshape(N_EXPERTS, 1), cnt0)


ZERO_ROWS = 64


def _dispatch_kernel(zlo_ref, zhi_ref, dest_p_ref, dest_s_ref, h2p_ref, h2s_ref, xs_hbm,
                     zbuf, sem, *, n_p, rows_per_tok):
    i = pl.program_id(0)
    r = rows_per_tok
    zrows = ZERO_ROWS * r

    def zero_copy(s):
        return pltpu.make_async_copy(zbuf, xs_hbm.at[pl.ds(pl.multiple_of(s * zrows, zrows), zrows)],
                                     sem.at[0])

    @pl.when(i == 0)
    def _():
        zbuf[...] = jnp.zeros(zbuf.shape, F32)

        def each_range(fn):
            def outer(e, carry):
                def inner(s, c):
                    fn(s)
                    return c
                return lax.fori_loop(zlo_ref[e], zhi_ref[e], inner, carry)
            lax.fori_loop(0, zlo_ref.shape[0], outer, 0)

        each_range(lambda s: zero_copy(s).start())
        each_range(lambda s: zero_copy(s).wait())

    def scatter(h2_ref, dest_ref):
        tm = h2_ref.shape[0] // r

        def body(t, carry):
            src = h2_ref.at[pl.ds(pl.multiple_of(t * r, r), r)]
            for k in range(TOP_K):
                dst = xs_hbm.at[pl.ds(pl.multiple_of(dest_ref[k * tm + t] * r, r), r)]
                pltpu.make_async_copy(src, dst, sem.at[1]).start(priority=k % 2)
            return carry

        lax.fori_loop(0, tm, body, 0)
        n_rows = tm * TOP_K * r
        pltpu.make_async_copy(xs_hbm.at[pl.ds(0, n_rows)], xs_hbm.at[pl.ds(0, n_rows)],
                              sem.at[1]).wait()

    @pl.when(i < n_p)
    def _():
        scatter(h2p_ref, dest_p_ref)

    @pl.when(i >= n_p)
    def _():
        scatter(h2s_ref, dest_s_ref)


def _dispatch(zlo, zhi, dest_p, dest_s, h2_p, h2_s, n_slots, rows_per_tok):
    r = rows_per_tok
    tm_p = _token_tile(h2_p.shape[0] // r)
    tm_s = _token_tile(h2_s.shape[0] // r)
    n_p = h2_p.shape[0] // (tm_p * r)
    n_s = h2_s.shape[0] // (tm_s * r)
    first = lambda i, *_: (jnp.minimum(i, n_p - 1),)
    second = lambda i, *_: (jnp.maximum(i - n_p, 0),)
    grid_spec = pltpu.PrefetchScalarGridSpec(
        num_scalar_prefetch=2,
        grid=(n_p + n_s,),
        in_specs=[
            pl.BlockSpec((tm_p * TOP_K,), first, memory_space=pltpu.SMEM),
            pl.BlockSpec((tm_s * TOP_K,), second, memory_space=pltpu.SMEM),
            pl.BlockSpec((tm_p * r, LANES), lambda i, *_: first(i) + (0,)),
            pl.BlockSpec((tm_s * r, LANES), lambda i, *_: second(i) + (0,)),
        ],
        out_specs=pl.BlockSpec(memory_space=pl.ANY),
        scratch_shapes=[pltpu.VMEM((ZERO_ROWS * r, LANES), F32), pltpu.SemaphoreType.DMA((2,))],
    )
    return pl.pallas_call(
        functools.partial(_dispatch_kernel, n_p=n_p, rows_per_tok=r),
        grid_spec=grid_spec,
        out_shape=jax.ShapeDtypeStruct((n_slots * r, LANES), F32),
        compiler_params=_params("arbitrary"),
        name="moe_dispatch",
    )(zlo, zhi, dest_p, dest_s, h2_p, h2_s)


def _moe_kernel(te_ref, nu_ref, xs_ref, wgu_f32, bgu_ref, wdn_f32, bdn_ref, ys_ref,
                wgu_ref, wdn_ref):
    i = pl.program_id(0)
    n_chunk = xs_ref.shape[0] // MOE_TILE
    d_ff = wdn_ref.shape[0]
    used = i < nu_ref[0]
    new_expert = (i == 0) | (te_ref[i] != te_ref[jnp.maximum(i - 1, 0)])

    @pl.when(used & new_expert)
    def _():
        rows = 128
        for c in range(wgu_ref.shape[0] // rows):
            wgu_ref[c * rows:(c + 1) * rows, :] = wgu_f32[c * rows:(c + 1) * rows, :].astype(BF16)
        for c in range(wdn_ref.shape[0] // rows):
            wdn_ref[c * rows:(c + 1) * rows, :] = wdn_f32[c * rows:(c + 1) * rows, :].astype(BF16)

    @pl.when(used)
    def _():
        x = jnp.concatenate(
            [xs_ref[pl.ds(j, MOE_TILE, stride=n_chunk), :] for j in range(n_chunk)], axis=-1)
        gu = _dot(x, wgu_ref[...]) + bgu_ref[...]
        g_ = jnp.minimum(gu[:, :d_ff], SWIGLU_LIMIT)
        u_ = jnp.clip(gu[:, d_ff:], -SWIGLU_LIMIT, SWIGLU_LIMIT)
        act = (u_ + 1.0) * g_ * _sigmoid(SWIGLU_ALPHA * g_)
        y = _dot(act, wdn_ref[...]) + bdn_ref[...]
        for j in range(n_chunk):
            ys_ref[pl.ds(j, MOE_TILE, stride=n_chunk), :] = y[:, j * LANES:(j + 1) * LANES]

    @pl.when(i >= nu_ref[0])
    def _():
        ys_ref[...] = jnp.zeros(ys_ref.shape, F32)


def _moe(tile_e, n_used, xs, w_gu, b_gu, w_dn, b_dn, rows_per_tok):
    n_tiles = xs.shape[0] // (MOE_TILE * rows_per_tok)
    d, two_ff = w_gu.shape[1:]
    d_ff = w_dn.shape[1]
    last = lambda i, nu: jnp.minimum(i, nu[0] - 1)
    grid_spec = pltpu.PrefetchScalarGridSpec(
        num_scalar_prefetch=2,
        grid=(n_tiles,),
        in_specs=[
            pl.BlockSpec((MOE_TILE * rows_per_tok, LANES), lambda i, te, nu: (last(i, nu), 0)),
            pl.BlockSpec((None, d, two_ff), lambda i, te, nu: (te[last(i, nu)], 0, 0)),
            pl.BlockSpec((None, 1, two_ff), lambda i, te, nu: (te[last(i, nu)], 0, 0)),
            pl.BlockSpec((None, d_ff, d), lambda i, te, nu: (te[last(i, nu)], 0, 0)),
            pl.BlockSpec((None, 1, d), lambda i, te, nu: (te[last(i, nu)], 0, 0)),
        ],
        out_specs=pl.BlockSpec((MOE_TILE * rows_per_tok, LANES), lambda i, te, nu: (i, 0)),
        scratch_shapes=[pltpu.VMEM((d, two_ff), BF16), pltpu.VMEM((d_ff, d), BF16)],
    )
    return pl.pallas_call(
        _moe_kernel,
        grid_spec=grid_spec,
        out_shape=jax.ShapeDtypeStruct(xs.shape, F32),
        compiler_params=_params("arbitrary"),
        name="moe_ffn",
    )(tile_e, n_used, xs, w_gu, b_gu.reshape(N_EXPERTS, 1, two_ff), w_dn,
      b_dn.reshape(N_EXPERTS, 1, d))


def _combine_kernel(dest_ref, dest_next_ref, ys_hbm, x1_ref, rw_ref, p_ref, gple_ref, wple_ref,
                    wgate_ref, y_ref, buf, sem, *, rows_per_tok):
    i = pl.program_id(0)
    n_i = pl.num_programs(0)
    tm, d = x1_ref.shape
    r = rows_per_tok

    def gather(d_ref, slot):
        def body(t, carry):
            for k in range(TOP_K):
                src = ys_hbm.at[pl.ds(pl.multiple_of(d_ref[k * tm + t] * r, r), r)]
                dst = buf.at[slot, pl.ds(pl.multiple_of((k * tm + t) * r, r), r)]
                pltpu.make_async_copy(src, dst, sem.at[slot]).start(priority=k % 2)
            return carry
        lax.fori_loop(0, tm, body, 0)

    slot = i % 2

    @pl.when(i == 0)
    def _():
        gather(dest_ref, 0)

    @pl.when(i + 1 < n_i)
    def _():
        gather(dest_next_ref, 1 - slot)

    pe = _dot(p_ref[...], wple_ref[...])
    pltpu.make_async_copy(buf.at[slot], buf.at[slot], sem.at[slot]).wait()

    rw = rw_ref[...]
    cols = []
    for j in range(r):
        acc = jnp.zeros((tm, LANES), F32)
        for k in range(TOP_K):
            acc = acc + rw[:, k:k + 1] * buf[slot, pl.ds(k * tm * r + j, tm, stride=r), :]
        cols.append(acc)
    x2 = x1_ref[...] + jnp.concatenate(cols, axis=-1)
    gate = _sigmoid(_dot(_rms(x2, gple_ref[...]), wgate_ref[...]))
    y_ref[...] = x2 + pe * gate


def _combine(dest_flat, ys, x1, rw, p2d, g_ple, wple_bf, wgate_bf, rows_per_tok):
    n, d = x1.shape
    tm = _token_tile(n)
    n_i = n // tm
    ple = p2d.shape[1]
    row = lambda w: pl.BlockSpec((tm, w), lambda i: (i, 0))
    return pl.pallas_call(
        functools.partial(_combine_kernel, rows_per_tok=rows_per_tok),
        grid=(n_i,),
        in_specs=[pl.BlockSpec((tm * TOP_K,), lambda i: (i,), memory_space=pltpu.SMEM),
                  pl.BlockSpec((tm * TOP_K,), lambda i: (jnp.minimum(i + 1, n_i - 1),),
                               memory_space=pltpu.SMEM),
                  pl.BlockSpec(memory_space=pl.ANY), row(d), row(LANES), row(ple),
                  _full((1, d)), _full((ple, d)), _full((d, d))],
        out_specs=row(d),
        out_shape=jax.ShapeDtypeStruct((n, d), F32),
        scratch_shapes=[pltpu.VMEM((2, TOP_K * tm * rows_per_tok, LANES), F32),
                        pltpu.SemaphoreType.DMA((2,))],
        compiler_params=_params("arbitrary"),
        name="combine_ple",
    )(dest_flat, dest_flat, ys, x1, rw, p2d, g_ple.reshape(1, d), wple_bf, wgate_bf)


def kernel(x_prompt, x_sample, cache_k, cache_v, state_hgrn, page_table, p_prompt, p_sample,
           g_mix, w_in, lb_logits, g_hg_out, g_q, g_k, w_lift_hgrn, w_lift_moba, w_out, g_ffn,
           w_router, b_router, w_gate_up, b_gate_up, w_down, b_down, g_ple, w_ple, w_ple_gate):
    depth = g_mix.shape[0]
    assert depth == 1, "single-layer step"
    li = 0
    n_b, seq, d = x_prompt.shape
    n_db, t_new, _ = x_sample.shape
    ps = cache_k.shape[3]
    rows_per_tok = d // LANES
    assert rows_per_tok == SUBLANES and seq % ps == 0

    lb = jnp.cumsum(jax.nn.softmax(lb_logits.astype(F32), axis=0), axis=0)[li]
    w_in_bf = w_in[li].astype(BF16)
    wlh, wlm, wout = (w_lift_hgrn[li].astype(BF16), w_lift_moba[li].astype(BF16),
                      w_out[li].astype(BF16))
    wple_bf, wgate_bf = w_ple[li].astype(BF16), w_ple_gate[li].astype(BF16)

    xp = x_prompt.reshape(n_b * seq, d)
    xs_ = x_sample.reshape(n_db * t_new, d)

    zh_p, q_p, k_p, v_p, gates_p = _inproj(xp, g_mix[li], w_in_bf, lb, g_q[li], g_k[li], ps)
    bh_p, st_p = _hgrn(zh_p, None, g_hg_out[li], n_b, seq)
    bm_p = _moba_prompt(q_p, k_p, v_p, n_b, seq)

    zh_s, q_s, k_s, v_s, gates_s = _inproj(xs_, g_mix[li], w_in_bf, lb, g_q[li], g_k[li], t_new)
    bh_s, st_s = _hgrn(zh_s, state_hgrn[li], g_hg_out[li], n_db, t_new)
    sel = _moba_select(q_s, cache_k, page_table, t_new)[..., :MOBA_TOPK]
    ppb = MOBA_BLOCK // ps
    logical = (sel[..., None] * ppb + jnp.arange(ppb, dtype=jnp.int32)).reshape(n_db, -1)
    n_pg = page_table.shape[1]
    hit = logical[:, :, None] == jnp.arange(n_pg, dtype=jnp.int32)
    phys = jnp.sum(jnp.where(hit, page_table[:, None, :], 0), axis=-1).reshape(-1)
    bm_s = _moba_sample(q_s, k_s, v_s, cache_k, cache_v, phys, n_db, t_new)

    cnt0 = jnp.zeros((N_EXPERTS, LANES), F32)
    x1_p, h2_p, ri_p, rw_p, cnt_p = _merge(xp, bh_p, bm_p, gates_p, wlh, wlm, wout, g_ffn[li],
                                           w_router[li], b_router[li], cnt0)
    x1_s, h2_s, ri_s, rw_s, cnt_all = _merge(xs_, bh_s, bm_s, gates_s, wlh, wlm, wout, g_ffn[li],
                                             w_router[li], b_router[li], cnt_p)

    counts = cnt_all[:, 0].astype(jnp.int32)
    padded = (counts + MOE_TILE - 1) // MOE_TILE * MOE_TILE
    pad_end = jnp.cumsum(padded)
    pad_start = pad_end - padded
    n_assign = (n_b * seq + n_db * t_new) * TOP_K
    n_tiles = -(-(n_assign + N_EXPERTS * (MOE_TILE - 1)) // MOE_TILE)
    tile_lo = jnp.arange(n_tiles, dtype=jnp.int32) * MOE_TILE
    tile_e = jnp.minimum(jnp.sum(pad_end[None, :] <= tile_lo[:, None], axis=1),
                         N_EXPERTS - 1).astype(jnp.int32)
    n_used = (pad_end[-1:] // MOE_TILE).astype(jnp.int32)
    zlo = jnp.concatenate([(pad_start + counts) // ZERO_ROWS, pad_end[-1:] // ZERO_ROWS])
    zhi = jnp.concatenate([pad_end // ZERO_ROWS,
                           jnp.full((1,), n_tiles * MOE_TILE // ZERO_ROWS, jnp.int32)])

    def dest_of(ri):
        n = ri.shape[1]
        tm = _token_tile(n)
        e, rank = ri[:TOP_K], ri[TOP_K:]
        start = jnp.sum(jnp.where(e[:, :, None] == jnp.arange(N_EXPERTS, dtype=jnp.int32),
                                  pad_start, 0), axis=-1)
        return (start + rank).reshape(TOP_K, n // tm, tm).transpose(1, 0, 2).reshape(-1)

    dest_p, dest_s = dest_of(ri_p), dest_of(ri_s)
    xs = _dispatch(zlo.astype(jnp.int32), zhi.astype(jnp.int32), dest_p, dest_s, h2_p, h2_s,
                   n_tiles * MOE_TILE, rows_per_tok)
    ys = _moe(tile_e, n_used, xs, w_gate_up[li], b_gate_up[li], w_down[li], b_down[li],
              rows_per_tok)

    y_p = _combine(dest_p, ys, x1_p, rw_p, p_prompt[li].reshape(n_b * seq, -1), g_ple[li],
                   wple_bf, wgate_bf, rows_per_tok)
    y_s = _combine(dest_s, ys, x1_s, rw_s, p_sample[li].reshape(n_db * t_new, -1), g_ple[li],
                   wple_bf, wgate_bf, rows_per_tok)

    return (y_p.reshape(n_b, seq, d), y_s.reshape(n_db, t_new, d),
            k_p.reshape(1, n_b, seq // ps, N_HEADS, ps, HEAD_DIM),
            v_p.reshape(1, n_b, seq // ps, N_HEADS, ps, HEAD_DIM),
            k_s[None], v_s[None], st_p[None], st_s[None])
```

```python
import functools

import jax
import jax.numpy as jnp
from jax import lax
from jax.experimental import pallas as pl
from jax.experimental.pallas import tpu as pltpu

F32 = jnp.float32
BF16 = jnp.bfloat16
HIGHEST = lax.Precision.HIGHEST

EPS = 1e-6
N_HEADS = 4
HEAD_DIM = 128
HG_CHUNK = 64
MOBA_BLOCK = 256
MOBA_TOPK = 3
MOBA_QGROUP = 16
N_EXPERTS = 32
TOP_K = 4
SWIGLU_LIMIT = 7.0
SWIGLU_ALPHA = 1.702
NEG = -1e30

LANES = 128
SUBLANES = 8
TOKEN_TILE = 256
HGRN_TILE = 512
MERGE_TILE = 512
MOE_TILE = 512
VMEM_LIMIT = 56 * 1024 * 1024


def _params(*sem):
    return pltpu.CompilerParams(dimension_semantics=sem, vmem_limit_bytes=VMEM_LIMIT)


def _sigmoid(x):
    return 1.0 / (1.0 + jnp.exp(-x))


def _silu(x):
    return x * _sigmoid(x)


def _rms(x, g):
    return x * lax.rsqrt(jnp.mean(x * x, axis=-1, keepdims=True) + EPS) * g


def _dot(a, b):
    return jnp.dot(a.astype(BF16), b.astype(BF16), preferred_element_type=F32)


def _dot_nt(a, b):
    return lax.dot_general(a.astype(BF16), b.astype(BF16), (((1,), (1,)), ((), ())),
                           preferred_element_type=F32)


def _dot_tn(a, b):
    return lax.dot_general(a.astype(BF16), b.astype(BF16), (((0,), (0,)), ((), ())),
                           preferred_element_type=F32)


def _token_tile(n):
    return TOKEN_TILE if n % TOKEN_TILE == 0 else n


def _full(shape):
    return pl.BlockSpec(shape, lambda *_: (0,) * len(shape))


def _inproj_kernel(x_ref, gmix_ref, w_ref, lb_ref, gq_ref, gk_ref,
                   zh_ref, q_ref, k_ref, v_ref, gate_ref):
    x = x_ref[...]
    h = _rms(x, gmix_ref[...]).astype(BF16)
    hw = N_HEADS * HEAD_DIM
    d = x.shape[-1]
    tm = x.shape[0]

    def proj(c0, width):
        return jnp.dot(h, w_ref[:, c0:c0 + width], preferred_element_type=F32)

    lb = lb_ref[...]
    zh_ref[:, 0:hw] = _silu(proj(0, hw))
    fa = proj(hw, hw)
    zh_ref[:, hw:2 * hw] = jnp.log(lb + (1.0 - lb) * _sigmoid(fa))
    zh_ref[:, 2 * hw:3 * hw] = (1.0 - lb) * _sigmoid(-fa)
    zh_ref[:, 3 * hw:4 * hw] = proj(2 * hw, hw)
    zh_ref[:, 4 * hw:5 * hw] = _silu(proj(3 * hw, hw))

    qb = proj(4 * hw, hw)
    kb = proj(5 * hw, hw)
    vb = proj(6 * hw, hw)
    rows = k_ref.shape[2]
    for hd in range(N_HEADS):
        sl = slice(hd * HEAD_DIM, (hd + 1) * HEAD_DIM)
        q_ref[:, sl] = _rms(qb[:, sl], gq_ref[...])
        k_ref[:, hd, :, :] = _rms(kb[:, sl], gk_ref[...]).reshape(tm // rows, rows, HEAD_DIM)
        v_ref[:, hd, :, :] = vb[:, sl].reshape(tm // rows, rows, HEAD_DIM)

    gate_ref[:, 0:d] = _sigmoid(proj(7 * hw, d))
    gate_ref[:, d:2 * d] = _sigmoid(proj(7 * hw + d, d))


def _inproj(x2d, g_mix, w_in_bf, lb, g_q, g_k, rows_per_page):
    n, d = x2d.shape
    hw = N_HEADS * HEAD_DIM
    tm = _token_tile(n)
    r = rows_per_page
    n_in = w_in_bf.shape[1]
    out_shape = (
        jax.ShapeDtypeStruct((n, 5 * hw), F32),
        jax.ShapeDtypeStruct((n, hw), F32),
        jax.ShapeDtypeStruct((n // r, N_HEADS, r, HEAD_DIM), F32),
        jax.ShapeDtypeStruct((n // r, N_HEADS, r, HEAD_DIM), F32),
        jax.ShapeDtypeStruct((n, 2 * d), F32),
    )
    row = lambda w: pl.BlockSpec((tm, w), lambda i: (i, 0))
    page = pl.BlockSpec((tm // r, N_HEADS, r, HEAD_DIM), lambda i: (i, 0, 0, 0))
    return pl.pallas_call(
        _inproj_kernel,
        grid=(n // tm,),
        in_specs=[row(d), _full((1, d)),
                  pl.BlockSpec((d, n_in), lambda i: (0, 0), pipeline_mode=pl.Buffered(1)),
                  _full((1, hw)), _full((1, HEAD_DIM)), _full((1, HEAD_DIM))],
        out_specs=(row(5 * hw), row(hw), page, page, row(2 * d)),
        out_shape=out_shape,
        compiler_params=_params("parallel"),
        name="inproj",
    )(x2d, g_mix.reshape(1, d), w_in_bf, lb.reshape(1, hw), g_q.reshape(1, HEAD_DIM),
      g_k.reshape(1, HEAD_DIM))


def _hgrn_kernel(*refs, chunk, has_s0, exact_mxu):
    if has_s0:
        zh_ref, s0_ref, g_ref, o_ref, sout_ref, st_ref = refs
    else:
        zh_ref, g_ref, o_ref, sout_ref, st_ref = refs
        s0_ref = None
    t = pl.program_id(1)
    hw = N_HEADS * HEAD_DIM
    n_sub = sout_ref.shape[0]
    tb = zh_ref.shape[0] // n_sub

    @pl.when(t == 0)
    def _():
        for s in range(n_sub):
            for hd in range(N_HEADS):
                if has_s0:
                    st_ref[s * N_HEADS + hd] = s0_ref[s, hd].T
                else:
                    st_ref[s * N_HEADS + hd] = jnp.zeros((HEAD_DIM, HEAD_DIM), F32)

    if exact_mxu:
        mm_nt = lambda a, b: lax.dot_general(a, b, (((1,), (1,)), ((), ())), precision=HIGHEST,
                                             preferred_element_type=F32)
        mm = lambda a, b: jnp.dot(a, b, precision=HIGHEST, preferred_element_type=F32)
        mm_tn = lambda a, b: lax.dot_general(a, b, (((0,), (0,)), ((), ())), precision=HIGHEST,
                                             preferred_element_type=F32)
    else:
        mm_nt, mm, mm_tn = _dot_nt, _dot, _dot_tn

    ri = lax.broadcasted_iota(jnp.int32, (chunk, chunk), 0)
    ci = lax.broadcasted_iota(jnp.int32, (chunk, chunk), 1)
    causal = ci <= ri
    tril = jnp.where(causal, 1.0, 0.0).astype(F32)

    for s in range(n_sub):
        sts = [st_ref[s * N_HEADS + hd] for hd in range(N_HEADS)]
        for c in range(tb // chunk):
            rows = slice(s * tb + c * chunk, s * tb + (c + 1) * chunk)
            k_all = zh_ref[rows, 2 * hw:3 * hw]
            b_all = jnp.dot(tril, zh_ref[rows, hw:2 * hw], precision=HIGHEST,
                            preferred_element_type=F32)
            b_end = b_all[chunk - 1:chunk, :]
            qd_all = zh_ref[rows, 0:hw] * jnp.exp(b_all)
            kd_all = k_all * jnp.exp(-b_all)
            kk_all = k_all * jnp.exp(b_end - b_all)
            decay = jnp.exp(b_end)
            for hd in range(N_HEADS):
                sl = slice(hd * HEAD_DIM, (hd + 1) * HEAD_DIM)
                v = zh_ref[rows, 3 * hw + hd * HEAD_DIM:3 * hw + (hd + 1) * HEAD_DIM]
                og = zh_ref[rows, 4 * hw + hd * HEAD_DIM:4 * hw + (hd + 1) * HEAD_DIM]
                qd = qd_all[:, sl]
                att = jnp.where(causal, mm_nt(qd, kd_all[:, sl]), 0.0)
                o = mm(att, v) + mm_nt(qd, sts[hd])
                sts[hd] = decay[:, sl] * sts[hd] + mm_tn(v, kk_all[:, sl])
                o_ref[rows, sl] = _rms(o, g_ref[...]) * og
        for hd in range(N_HEADS):
            st_ref[s * N_HEADS + hd] = sts[hd]

    @pl.when(t == pl.num_programs(1) - 1)
    def _():
        for s in range(n_sub):
            for hd in range(N_HEADS):
                sout_ref[s, hd] = st_ref[s * N_HEADS + hd].T


def _hgrn(zh, s0, g_hg_out, n_seq, seq_len):
    hw = N_HEADS * HEAD_DIM
    chunk = min(HG_CHUNK, seq_len)
    assert seq_len % chunk == 0
    tb = min(HGRN_TILE, seq_len)
    assert seq_len % tb == 0 and tb % chunk == 0
    n_t = seq_len // tb
    n_sub = 1
    if n_t == 1:
        n_sub = max(d for d in range(1, HGRN_TILE // (8 * tb) + 1) if n_seq % d == 0)
    n_steps, tb = n_seq // n_sub, tb * n_sub
    has_s0 = s0 is not None
    state_spec = pl.BlockSpec((n_sub, N_HEADS, HEAD_DIM, HEAD_DIM), lambda b, t: (b, 0, 0, 0))
    in_specs = [pl.BlockSpec((tb, 5 * hw), lambda b, t: (b * n_t + t, 0))]
    args = [zh]
    if has_s0:
        in_specs.append(state_spec)
        args.append(s0)
    in_specs.append(_full((1, HEAD_DIM)))
    args.append(g_hg_out.reshape(1, HEAD_DIM))
    return pl.pallas_call(
        functools.partial(_hgrn_kernel, chunk=chunk, has_s0=has_s0, exact_mxu=chunk < 16),
        grid=(n_steps, n_t),
        in_specs=in_specs,
        out_specs=(pl.BlockSpec((tb, hw), lambda b, t: (b * n_t + t, 0)), state_spec),
        out_shape=(jax.ShapeDtypeStruct((n_seq * seq_len, hw), F32),
                   jax.ShapeDtypeStruct((n_seq, N_HEADS, HEAD_DIM, HEAD_DIM), F32)),
        scratch_shapes=[pltpu.VMEM((n_sub * N_HEADS, HEAD_DIM, HEAD_DIM), F32)],
        compiler_params=_params("parallel", "arbitrary"),
        name="hgrn_prompt" if not has_s0 else "hgrn_sample",
    )(*args)


def _topk_rank(gsc, n_valid):
    rows, n_col = gsc.shape
    blk = lax.broadcasted_iota(jnp.int32, (rows, n_col), 1)
    rank = jnp.zeros((rows, n_col), jnp.int32)
    for m in range(n_col):
        sm = gsc[:, m:m + 1]
        beats = jnp.where(sm > gsc, 1, jnp.where(sm == gsc, jnp.where(blk > m, 1, 0), 0))
        rank = rank + jnp.where(m < n_valid, beats, 0)
    return blk, rank


def _moba_prompt_kernel(q_ref, k_ref, v_ref, o_ref, kmean_ref, osc_ref, perm_ref):
    n = pl.program_id(1)
    n_b = kmean_ref.shape[1]
    ppb = MOBA_BLOCK // k_ref.shape[2]
    scale = HEAD_DIM ** -0.5
    heads = range(N_HEADS)

    @pl.when(n == 0)
    def _():
        for hd in heads:
            for m in range(n_b):
                kb = k_ref[m * ppb:(m + 1) * ppb, hd].reshape(MOBA_BLOCK, HEAD_DIM)
                kmean_ref[hd, m:m + 1, :] = jnp.mean(kb, axis=0, keepdims=True)

    def block(m, hd):
        kb = k_ref[pl.ds(m * ppb, ppb), hd].reshape(MOBA_BLOCK, HEAD_DIM)
        vb = v_ref[pl.ds(m * ppb, ppb), hd].reshape(MOBA_BLOCK, HEAD_DIM)
        return kb, vb

    ri = lax.broadcasted_iota(jnp.int32, (MOBA_BLOCK, MOBA_BLOCK), 0)
    ci = lax.broadcasted_iota(jnp.int32, (MOBA_BLOCK, MOBA_BLOCK), 1)
    qbs, sels, init = [], [], []
    for hd in heads:
        q = q_ref[:, hd * HEAD_DIM:(hd + 1) * HEAD_DIM]
        gsc = lax.dot_general(q, kmean_ref[hd], (((1,), (1,)), ((), ())), precision=HIGHEST,
                              preferred_element_type=F32)
        blk, rank = _topk_rank(gsc, n)
        sels.append(jnp.where((blk < n) & (rank < MOBA_TOPK), 1.0, 0.0))
        qb = (q * scale).astype(BF16)
        qbs.append(qb)
        k_own, v_own = block(n, hd)
        s = jnp.where(ci <= ri, _dot_nt(qb, k_own), NEG)
        m0 = jnp.max(s, axis=-1, keepdims=True)
        p = jnp.exp(s - m0)
        init.append((m0, jnp.sum(p, axis=-1, keepdims=True), _dot(p, v_own)))
    blk = lax.broadcasted_iota(jnp.int32, (MOBA_BLOCK, n_b), 1)

    def body(m, carry):
        out = []
        for hd in heads:
            m_i, l_i, acc = carry[hd]
            kb, vb = block(m, hd)
            picked = jnp.sum(jnp.where(blk == m, sels[hd], 0.0), axis=-1, keepdims=True)
            s = jnp.where(picked > 0.0, _dot_nt(qbs[hd], kb), NEG)
            m_new = jnp.maximum(m_i, jnp.max(s, axis=-1, keepdims=True))
            alpha = jnp.exp(m_i - m_new)
            p = jnp.exp(s - m_new)
            out.append((m_new, alpha * l_i + jnp.sum(p, axis=-1, keepdims=True),
                        alpha * acc + _dot(p, vb)))
        return tuple(out)

    final = lax.fori_loop(0, n, body, tuple(init))
    n_grp = MOBA_BLOCK // MOBA_QGROUP
    for hd in heads:
        _, l_f, acc_f = final[hd]
        osc_ref[hd] = acc_f / l_f
        for qg in range(MOBA_QGROUP):
            perm_ref[qg, pl.ds(hd, n_grp, stride=N_HEADS), :] = (
                osc_ref[hd, pl.ds(qg, n_grp, stride=MOBA_QGROUP), :])
    for qg in range(MOBA_QGROUP):
        qh, ql = divmod(qg, N_HEADS)
        o_ref[ql, :, qh * HEAD_DIM:(qh + 1) * HEAD_DIM] = perm_ref[qg]


def _moba_prompt(q, k_pages, v_pages, n_seq, seq_len):
    assert seq_len % MOBA_BLOCK == 0 and MOBA_QGROUP == N_HEADS * N_HEADS
    n_b = seq_len // MOBA_BLOCK
    ps = k_pages.shape[2]
    assert MOBA_BLOCK % ps == 0
    pps = seq_len // ps
    hw = N_HEADS * HEAD_DIM
    n_grp = MOBA_BLOCK // MOBA_QGROUP
    qspec = pl.BlockSpec((MOBA_BLOCK, hw), lambda b, n: (b * n_b + n, 0))
    kvspec = pl.BlockSpec((pps, N_HEADS, ps, HEAD_DIM), lambda b, n: (b, 0, 0, 0))
    out = pl.pallas_call(
        _moba_prompt_kernel,
        grid=(n_seq, n_b),
        in_specs=[qspec, kvspec, kvspec],
        out_specs=pl.BlockSpec((None, N_HEADS, n_grp * N_HEADS, hw), lambda b, n: (b, 0, n, 0)),
        out_shape=jax.ShapeDtypeStruct((n_seq, N_HEADS, seq_len // N_HEADS, hw), F32),
        scratch_shapes=[pltpu.VMEM((N_HEADS, n_b, HEAD_DIM), F32),
                        pltpu.VMEM((N_HEADS, MOBA_BLOCK, HEAD_DIM), F32),
                        pltpu.VMEM((MOBA_QGROUP, n_grp * N_HEADS, HEAD_DIM), F32)],
        compiler_params=_params("parallel", "arbitrary"),
        name="moba_prompt",
    )(q, k_pages, v_pages)
    return out.reshape(n_seq * seq_len, hw)


PAGES_PER_STEP = 16


def _moba_select_kernel(pt_ref, q_ref, *refs):
    page_refs = refs[:PAGES_PER_STEP]
    sel_ref, sc_ref = refs[PAGES_PER_STEP:PAGES_PER_STEP + 2]
    kmean_ref, qbd_ref = refs[PAGES_PER_STEP + 2:]
    j = pl.program_id(1)
    ps = page_refs[0].shape[1]
    ppb = MOBA_BLOCK // ps
    bps = PAGES_PER_STEP // ppb
    n_blk = kmean_ref.shape[1]
    t_new = q_ref.shape[0]
    scale = HEAD_DIM ** -0.5

    @pl.when(j == 0)
    def _():
        zero = jnp.zeros((t_new, HEAD_DIM), F32)
        for p in range(N_HEADS // 2):
            q0 = q_ref[:, (2 * p) * HEAD_DIM:(2 * p + 1) * HEAD_DIM] * scale
            q1 = q_ref[:, (2 * p + 1) * HEAD_DIM:(2 * p + 2) * HEAD_DIM] * scale
            qbd_ref[p] = jnp.concatenate([jnp.concatenate([q0, zero], axis=-1),
                                          jnp.concatenate([zero, q1], axis=-1)], axis=0).astype(BF16)

    for i in range(PAGES_PER_STEP):
        for p in range(N_HEADS // 2):
            kcat = jnp.concatenate([page_refs[i][2 * p], page_refs[i][2 * p + 1]], axis=-1)
            sc_ref[p, i // ppb, :, (i % ppb) * ps:(i % ppb + 1) * ps] = _dot_nt(qbd_ref[p], kcat)

    for i in range(bps):
        for hd in range(N_HEADS):
            tot = jnp.zeros((1, HEAD_DIM), F32)
            for pp in range(ppb):
                tot = tot + jnp.sum(page_refs[i * ppb + pp][hd], axis=0, keepdims=True)
            kmean_ref[hd, pl.ds(j * bps + i, 1), :] = tot / float(MOBA_BLOCK)

    @pl.when(j == pl.num_programs(1) - 1)
    def _():
        for hd in range(N_HEADS):
            q = q_ref[:, hd * HEAD_DIM:(hd + 1) * HEAD_DIM]
            gsc = lax.dot_general(q, kmean_ref[hd], (((1,), (1,)), ((), ())), precision=HIGHEST,
                                  preferred_element_type=F32)
            blk, rank = _topk_rank(gsc, n_blk)
            lane = lax.broadcasted_iota(jnp.int32, (t_new, LANES), 1)
            out = jnp.zeros((t_new, LANES), jnp.int32)
            for r in range(MOBA_TOPK):
                idx = jnp.sum(jnp.where(rank == r, blk.astype(F32), 0.0), axis=-1, keepdims=True)
                out = jnp.where(lane == r, idx.astype(jnp.int32), out)
            sel_ref[0, hd] = out


def _moba_select(q_s, cache_k, page_table, t_new):
    n_seq, n_pg = page_table.shape
    _, _, _, ps, _ = cache_k.shape
    assert (n_pg * ps) % MOBA_BLOCK == 0, "cached rows must end on a MoBA block boundary"
    assert n_pg % PAGES_PER_STEP == 0 and MOBA_BLOCK % ps == 0
    n_blk = n_pg * ps // MOBA_BLOCK
    assert n_blk >= MOBA_TOPK
    hw = N_HEADS * HEAD_DIM
    bps = PAGES_PER_STEP * ps // MOBA_BLOCK

    def page_spec(i):
        return pl.BlockSpec((None, None, N_HEADS, ps, HEAD_DIM),
                            lambda b, j, pt: (0, pt[b, j * PAGES_PER_STEP + i], 0, 0, 0))

    grid_spec = pltpu.PrefetchScalarGridSpec(
        num_scalar_prefetch=1,
        grid=(n_seq, n_pg // PAGES_PER_STEP),
        in_specs=[pl.BlockSpec((t_new, hw), lambda b, j, pt: (b, 0))]
        + [page_spec(i) for i in range(PAGES_PER_STEP)],
        out_specs=(
            pl.BlockSpec((1, N_HEADS, t_new, LANES), lambda b, j, pt: (b, 0, 0, 0)),
            pl.BlockSpec((None, N_HEADS // 2, bps, 2 * t_new, MOBA_BLOCK),
                         lambda b, j, pt: (b, 0, j, 0, 0))),
        scratch_shapes=[pltpu.VMEM((N_HEADS, n_blk, HEAD_DIM), F32),
                        pltpu.VMEM((N_HEADS // 2, 2 * t_new, 2 * HEAD_DIM), BF16)],
    )
    return pl.pallas_call(
        _moba_select_kernel,
        grid_spec=grid_spec,
        out_shape=(jax.ShapeDtypeStruct((n_seq, N_HEADS, t_new, LANES), jnp.int32),
                   jax.ShapeDtypeStruct((n_seq, N_HEADS // 2, n_blk, 2 * t_new, MOBA_BLOCK), F32)),
        compiler_params=_params("parallel", "arbitrary"),
        name="moba_select",
    )(page_table, q_s, *([cache_k] * PAGES_PER_STEP))


def _moba_sample_kernel(pages_ref, blk_ref, q_ref, kn_ref, vn_ref, sc_ref, cv_hbm, o_ref,
                        vbuf, ssel_ref, sem, *, n_fetch):
    g = pl.program_id(0)
    n_g = pl.num_programs(0)
    ps = vbuf.shape[2]
    t_new = q_ref.shape[0]
    per_q = n_fetch // t_new * ps
    scale = HEAD_DIM ** -0.5

    def copies(step, slot):
        hd = step % N_HEADS
        return [pltpu.make_async_copy(cv_hbm.at[0, pages_ref[step * n_fetch + i], hd],
                                      vbuf.at[slot, i], sem.at[slot]) for i in range(n_fetch)]

    @pl.when(g == 0)
    def _():
        for c in copies(0, 0):
            c.start()

    @pl.when(g + 1 < n_g)
    def _():
        for c in copies(g + 1, (g + 1) % 2):
            c.start()

    for t in range(t_new):
        for r in range(MOBA_TOPK):
            blk = blk_ref[(g * t_new + t) * MOBA_TOPK + r]
            ssel_ref[t:t + 1, r * MOBA_BLOCK:(r + 1) * MOBA_BLOCK] = sc_ref[blk, t:t + 1, :]
    s = ssel_ref[...]

    pad = jnp.zeros((2 * SUBLANES - t_new, HEAD_DIM), F32)
    qp = jnp.concatenate([q_ref[...] * scale, pad], axis=0)
    kn = jnp.concatenate([kn_ref[...], pad], axis=0)
    vn = jnp.concatenate([vn_ref[...], pad], axis=0)
    s_new = _dot_nt(qp, kn)[:t_new]
    rn = lax.broadcasted_iota(jnp.int32, s_new.shape, 0)
    cn = lax.broadcasted_iota(jnp.int32, s_new.shape, 1)
    s_new = jnp.where((cn <= rn) & (cn < t_new), s_new, NEG)
    m = jnp.maximum(jnp.max(s, axis=-1, keepdims=True), jnp.max(s_new, axis=-1, keepdims=True))
    p = jnp.exp(s - m)
    p_new = jnp.exp(s_new - m)
    den = jnp.sum(p, axis=-1, keepdims=True) + jnp.sum(p_new, axis=-1, keepdims=True)

    row = lax.broadcasted_iota(jnp.int32, p.shape, 0)
    zero_rows = jnp.zeros((2 * SUBLANES - t_new, t_new * per_q), F32)
    p_wide = jnp.concatenate([jnp.where(row == t, p, 0.0) for t in range(t_new)], axis=-1)
    p_wide = jnp.concatenate([p_wide, zero_rows], axis=0)
    p_new = jnp.concatenate([p_new, jnp.zeros((2 * SUBLANES - t_new, p_new.shape[1]), F32)], axis=0)

    slot = g % 2
    for c in copies(g, slot):
        c.wait()
    vc = vbuf[slot].reshape(n_fetch * ps, HEAD_DIM)
    o = _dot(p_wide, vc) + _dot(p_new, vn)
    o_ref[...] = o[:t_new] / den


def _moba_sample(q_s, k_new, v_new, scores, cache_v, phys_pages, sel_blocks, n_seq, t_new):
    ps = cache_v.shape[3]
    n_fetch = phys_pages.shape[0] // (n_seq * N_HEADS)
    n_blk = scores.shape[2]
    head = lambda g, *_: (g // N_HEADS, g % N_HEADS)
    grid_spec = pltpu.PrefetchScalarGridSpec(
        num_scalar_prefetch=2,
        grid=(n_seq * N_HEADS,),
        in_specs=[
            pl.BlockSpec((t_new, HEAD_DIM), head),
            pl.BlockSpec((None, None, t_new, HEAD_DIM), lambda g, *_: head(g) + (0, 0)),
            pl.BlockSpec((None, None, t_new, HEAD_DIM), lambda g, *_: head(g) + (0, 0)),
            pl.BlockSpec((None, None, n_blk, t_new, MOBA_BLOCK),
                         lambda g, *_: (g // N_HEADS, (g % N_HEADS) // 2, 0, g % 2, 0)),
            pl.BlockSpec(memory_space=pl.ANY),
        ],
        out_specs=pl.BlockSpec((t_new, HEAD_DIM), head),
        scratch_shapes=[
            pltpu.VMEM((2, n_fetch, ps, HEAD_DIM), F32),
            pltpu.VMEM((t_new, MOBA_TOPK * MOBA_BLOCK), F32),
            pltpu.SemaphoreType.DMA((2,)),
        ],
    )
    return pl.pallas_call(
        functools.partial(_moba_sample_kernel, n_fetch=n_fetch),
        grid_spec=grid_spec,
        out_shape=jax.ShapeDtypeStruct(q_s.shape, F32),
        compiler_params=_params("arbitrary"),
        name="moba_sample",
    )(phys_pages, sel_blocks, q_s, k_new, v_new, scores, cache_v)


def _merge_kernel(x_ref, bh_ref, bm_ref, gate_ref, wlh_ref, wlm_ref, wout_ref, gffn_ref,
                  wr_ref, br_ref, cnt0_ref, x1_ref, h2_ref, ri_ref, rw_ref, cnt_ref, run_ref):
    i = pl.program_id(0)
    tm, d = x_ref.shape

    @pl.when(i == 0)
    def _():
        run_ref[...] = cnt0_ref[...]

    mixed = (gate_ref[:, 0:d] * _dot(bh_ref[...], wlh_ref[...])
             + gate_ref[:, d:2 * d] * _dot(bm_ref[...], wlm_ref[...]))
    x1 = x_ref[...] + _dot(mixed, wout_ref[...])
    x1_ref[...] = x1
    h2 = _rms(x1, gffn_ref[...])
    n_chunk = d // LANES
    for j in range(n_chunk):
        h2_ref[pl.ds(j, tm, stride=n_chunk), :] = h2[:, j * LANES:(j + 1) * LANES]

    logits = lax.dot_general(wr_ref[...], h2, (((1,), (1,)), ((), ())), precision=HIGHEST,
                             preferred_element_type=F32) + br_ref[...]
    sub_e = lax.broadcasted_iota(jnp.int32, (N_EXPERTS, tm), 0).astype(F32)
    lg = logits
    vals, ids, hots = [], [], []
    for _ in range(TOP_K):
        mx = jnp.max(lg, axis=0, keepdims=True)
        idx = jnp.min(jnp.where(lg == mx, sub_e, float(N_EXPERTS)), axis=0, keepdims=True)
        hot = sub_e == idx
        vals.append(mx)
        ids.append(idx)
        hots.append(hot)
        lg = jnp.where(hot, -jnp.inf, lg)
    exps = [jnp.exp(v - vals[0]) for v in vals]
    den = exps[0] + exps[1] + exps[2] + exps[3]

    chosen = jnp.zeros((N_EXPERTS, tm), F32)
    for hot in hots:
        chosen = chosen + jnp.where(hot, 1.0, 0.0)
    r_i = lax.broadcasted_iota(jnp.int32, (tm, tm), 0)
    c_i = lax.broadcasted_iota(jnp.int32, (tm, tm), 1)
    earlier = jnp.where(r_i < c_i, 1.0, 0.0)
    base = _dot(chosen, earlier) + run_ref[:, 0:1]
    run_ref[...] = run_ref[...] + jnp.sum(chosen, axis=1, keepdims=True)
    cnt_ref[...] = run_ref[...]

    sub_i = lax.broadcasted_iota(jnp.int32, (2 * TOP_K, tm), 0)
    sub_w = lax.broadcasted_iota(jnp.int32, (LANES, tm), 0)
    out_i = jnp.zeros((2 * TOP_K, tm), jnp.int32)
    out_w = jnp.zeros((LANES, tm), F32)
    for k in range(TOP_K):
        r_k = jnp.sum(jnp.where(hots[k], base, 0.0), axis=0, keepdims=True)
        out_i = jnp.where(sub_i == k, ids[k].astype(jnp.int32), out_i)
        out_i = jnp.where(sub_i == TOP_K + k, r_k.astype(jnp.int32), out_i)
        out_w = jnp.where(sub_w == k, exps[k] / den, out_w)
    ri_ref[...] = out_i
    rw_ref[...] = out_w.T


def _merge(x2d, bh, bm, gates, wlh, wlm, wout, g_ffn, w_router, b_router, cnt0):
    n, d = x2d.shape
    hw = N_HEADS * HEAD_DIM
    tm = MERGE_TILE if n % MERGE_TILE == 0 else n
    n_chunk = d // LANES
    row = lambda w: pl.BlockSpec((tm, w), lambda i: (i, 0))
    return pl.pallas_call(
        _merge_kernel,
        grid=(n // tm,),
        in_specs=[row(d), row(hw), row(hw), row(2 * d), _full((hw, d)), _full((hw, d)),
                  _full((d, d)), _full((1, d)), _full((N_EXPERTS, d)), _full((N_EXPERTS, 1)),
                  _full((N_EXPERTS, LANES))],
        out_specs=(row(d), pl.BlockSpec((tm * n_chunk, LANES), lambda i: (i, 0)),
                   pl.BlockSpec((2 * TOP_K, tm), lambda i: (0, i)), row(LANES),
                   _full((N_EXPERTS, LANES))),
        out_shape=(jax.ShapeDtypeStruct((n, d), F32),
                   jax.ShapeDtypeStruct((n * n_chunk, LANES), F32),
                   jax.ShapeDtypeStruct((2 * TOP_K, n), jnp.int32),
                   jax.ShapeDtypeStruct((n, LANES), F32),
                   jax.ShapeDtypeStruct((N_EXPERTS, LANES), F32)),
        scratch_shapes=[pltpu.VMEM((N_EXPERTS, LANES), F32)],
        compiler_params=_params("arbitrary"),
        name="merge_router",
    )(x2d, bh, bm, gates, wlh, wlm, wout, g_ffn.reshape(1, d), w_router.T,
      b_router.reshape(N_EXPERTS, 1), cnt0)


ZERO_ROWS = 64


def _dispatch_kernel(zlo_ref, zhi_ref, dest_p_ref, dest_s_ref, h2p_ref, h2s_ref, xs_hbm,
                     zbuf, sem, *, n_p, rows_per_tok):
    i = pl.program_id(0)
    r = rows_per_tok
    zrows = ZERO_ROWS * r

    def zero_copy(s):
        return pltpu.make_async_copy(zbuf, xs_hbm.at[pl.ds(pl.multiple_of(s * zrows, zrows), zrows)],
                                     sem.at[0])

    @pl.when(i == 0)
    def _():
        zbuf[...] = jnp.zeros(zbuf.shape, F32)

        def each_range(fn):
            def outer(e, carry):
                def inner(s, c):
                    fn(s)
                    return c
                return lax.fori_loop(zlo_ref[e], zhi_ref[e], inner, carry)
            lax.fori_loop(0, zlo_ref.shape[0], outer, 0)

        each_range(lambda s: zero_copy(s).start())
        each_range(lambda s: zero_copy(s).wait())

    def scatter(h2_ref, dest_ref):
        tm = h2_ref.shape[0] // r

        unroll = 8 if tm % 8 == 0 else 1

        def body(tu, carry):
            for u in range(unroll):
                t = tu * unroll + u
                src = h2_ref.at[pl.ds(pl.multiple_of(t * r, r), r)]
                for k in range(TOP_K):
                    dst = xs_hbm.at[pl.ds(pl.multiple_of(dest_ref[k * tm + t] * r, r), r)]
                    pltpu.make_async_copy(src, dst, sem.at[1]).start(priority=k % 2)
            return carry

        lax.fori_loop(0, tm // unroll, body, 0)
        n_rows = tm * TOP_K * r
        pltpu.make_async_copy(xs_hbm.at[pl.ds(0, n_rows)], xs_hbm.at[pl.ds(0, n_rows)],
                              sem.at[1]).wait()

    @pl.when(i < n_p)
    def _():
        scatter(h2p_ref, dest_p_ref)

    @pl.when(i >= n_p)
    def _():
        scatter(h2s_ref, dest_s_ref)


def _dispatch(zlo, zhi, dest_p, dest_s, h2_p, h2_s, n_slots, rows_per_tok):
    r = rows_per_tok
    tm_p = _token_tile(h2_p.shape[0] // r)
    tm_s = _token_tile(h2_s.shape[0] // r)
    n_p = h2_p.shape[0] // (tm_p * r)
    n_s = h2_s.shape[0] // (tm_s * r)
    first = lambda i, *_: (jnp.minimum(i, n_p - 1),)
    second = lambda i, *_: (jnp.maximum(i - n_p, 0),)
    grid_spec = pltpu.PrefetchScalarGridSpec(
        num_scalar_prefetch=2,
        grid=(n_p + n_s,),
        in_specs=[
            pl.BlockSpec((tm_p * TOP_K,), first, memory_space=pltpu.SMEM),
            pl.BlockSpec((tm_s * TOP_K,), second, memory_space=pltpu.SMEM),
            pl.BlockSpec((tm_p * r, LANES), lambda i, *_: first(i) + (0,)),
            pl.BlockSpec((tm_s * r, LANES), lambda i, *_: second(i) + (0,)),
        ],
        out_specs=pl.BlockSpec(memory_space=pl.ANY),
        scratch_shapes=[pltpu.VMEM((ZERO_ROWS * r, LANES), F32), pltpu.SemaphoreType.DMA((2,))],
    )
    return pl.pallas_call(
        functools.partial(_dispatch_kernel, n_p=n_p, rows_per_tok=r),
        grid_spec=grid_spec,
        out_shape=jax.ShapeDtypeStruct((n_slots * r, LANES), F32),
        compiler_params=_params("arbitrary"),
        name="moe_dispatch",
    )(zlo, zhi, dest_p, dest_s, h2_p, h2_s)


def _moe_kernel(te_ref, nu_ref, xs_ref, wgu_f32, bgu_ref, wdn_f32, bdn_ref, ys_ref,
                wgu_ref, wdn_ref):
    i = pl.program_id(0)
    n_chunk = xs_ref.shape[0] // MOE_TILE
    d_ff = wdn_ref.shape[0]
    used = i < nu_ref[0]
    new_expert = (i == 0) | (te_ref[i] != te_ref[jnp.maximum(i - 1, 0)])

    @pl.when(used & new_expert)
    def _():
        rows = 128
        for c in range(wgu_ref.shape[0] // rows):
            wgu_ref[c * rows:(c + 1) * rows, :] = wgu_f32[c * rows:(c + 1) * rows, :].astype(BF16)
        for c in range(wdn_ref.shape[0] // rows):
            wdn_ref[c * rows:(c + 1) * rows, :] = wdn_f32[c * rows:(c + 1) * rows, :].astype(BF16)

    @pl.when(used)
    def _():
        x = jnp.concatenate(
            [xs_ref[pl.ds(j, MOE_TILE, stride=n_chunk), :] for j in range(n_chunk)], axis=-1)
        gu = _dot(x, wgu_ref[...]) + bgu_ref[...]
        g_ = jnp.minimum(gu[:, :d_ff], SWIGLU_LIMIT)
        u_ = jnp.clip(gu[:, d_ff:], -SWIGLU_LIMIT, SWIGLU_LIMIT)
        act = (u_ + 1.0) * g_ * _sigmoid(SWIGLU_ALPHA * g_)
        y = _dot(act, wdn_ref[...]) + bdn_ref[...]
        for j in range(n_chunk):
            ys_ref[pl.ds(j, MOE_TILE, stride=n_chunk), :] = y[:, j * LANES:(j + 1) * LANES]

    @pl.when(i >= nu_ref[0])
    def _():
        ys_ref[...] = jnp.zeros(ys_ref.shape, F32)


def _moe(tile_e, n_used, xs, w_gu, b_gu, w_dn, b_dn, rows_per_tok):
    n_tiles = xs.shape[0] // (MOE_TILE * rows_per_tok)
    d, two_ff = w_gu.shape[1:]
    d_ff = w_dn.shape[1]
    last = lambda i, nu: jnp.minimum(i, nu[0] - 1)
    grid_spec = pltpu.PrefetchScalarGridSpec(
        num_scalar_prefetch=2,
        grid=(n_tiles,),
        in_specs=[
            pl.BlockSpec((MOE_TILE * rows_per_tok, LANES), lambda i, te, nu: (last(i, nu), 0)),
            pl.BlockSpec((None, d, two_ff), lambda i, te, nu: (te[last(i, nu)], 0, 0)),
            pl.BlockSpec((None, 1, two_ff), lambda i, te, nu: (te[last(i, nu)], 0, 0)),
            pl.BlockSpec((None, d_ff, d), lambda i, te, nu: (te[last(i, nu)], 0, 0)),
            pl.BlockSpec((None, 1, d), lambda i, te, nu: (te[last(i, nu)], 0, 0)),
        ],
        out_specs=pl.BlockSpec((MOE_TILE * rows_per_tok, LANES), lambda i, te, nu: (i, 0)),
        scratch_shapes=[pltpu.VMEM((d, two_ff), BF16), pltpu.VMEM((d_ff, d), BF16)],
    )
    return pl.pallas_call(
        _moe_kernel,
        grid_spec=grid_spec,
        out_shape=jax.ShapeDtypeStruct(xs.shape, F32),
        compiler_params=_params("arbitrary"),
        name="moe_ffn",
    )(tile_e, n_used, xs, w_gu, b_gu.reshape(N_EXPERTS, 1, two_ff), w_dn,
      b_dn.reshape(N_EXPERTS, 1, d))


def _combine_kernel(dest_ref, dest_next_ref, ys_hbm, x1_ref, rw_ref, p_ref, gple_ref, wple_ref,
                    wgate_ref, y_ref, buf, sem, *, rows_per_tok):
    i = pl.program_id(0)
    n_i = pl.num_programs(0)
    tm, d = x1_ref.shape
    r = rows_per_tok

    def row_copy(d_ref, slot, t, k):
        src = ys_hbm.at[pl.ds(pl.multiple_of(d_ref[k * tm + t] * r, r), r)]
        dst = buf.at[slot, pl.ds(pl.multiple_of((k * tm + t) * r, r), r)]
        return pltpu.make_async_copy(src, dst, sem.at[slot])

    def wait_slot(slot):
        pltpu.make_async_copy(buf.at[slot], buf.at[slot], sem.at[slot]).wait()

    slot = i % 2

    @pl.when(i == 0)
    def _():
        def body(t, carry):
            for k in range(TOP_K):
                row_copy(dest_ref, 0, t, k).start(priority=k % 2)
            return carry
        lax.fori_loop(0, tm, body, 0)

    for t in range(tm):
        for k in range(TOP_K):
            row_copy(dest_next_ref, 1 - slot, t, k).start(priority=k % 2)

    pe = _dot(p_ref[...], wple_ref[...])
    wait_slot(slot)

    rw = rw_ref[...]
    cols = []
    for j in range(r):
        acc = jnp.zeros((tm, LANES), F32)
        for k in range(TOP_K):
            acc = acc + rw[:, k:k + 1] * buf[slot, pl.ds(k * tm * r + j, tm, stride=r), :]
        cols.append(acc)
    x2 = x1_ref[...] + jnp.concatenate(cols, axis=-1)
    gate = _sigmoid(_dot(_rms(x2, gple_ref[...]), wgate_ref[...]))
    y_ref[...] = x2 + pe * gate

    @pl.when(i == n_i - 1)
    def _():
        wait_slot(1 - slot)


def _combine(dest_flat, ys, x1, rw, p2d, g_ple, wple_bf, wgate_bf, rows_per_tok):
    n, d = x1.shape
    tm = _token_tile(n)
    n_i = n // tm
    ple = p2d.shape[1]
    row = lambda w: pl.BlockSpec((tm, w), lambda i: (i, 0))
    return pl.pallas_call(
        functools.partial(_combine_kernel, rows_per_tok=rows_per_tok),
        grid=(n_i,),
        in_specs=[pl.BlockSpec((tm * TOP_K,), lambda i: (i,), memory_space=pltpu.SMEM),
                  pl.BlockSpec((tm * TOP_K,), lambda i: (jnp.minimum(i + 1, n_i - 1),),
                               memory_space=pltpu.SMEM),
                  pl.BlockSpec(memory_space=pl.ANY), row(d), row(LANES), row(ple),
                  _full((1, d)), _full((ple, d)), _full((d, d))],
        out_specs=row(d),
        out_shape=jax.ShapeDtypeStruct((n, d), F32),
        scratch_shapes=[pltpu.VMEM((2, TOP_K * tm * rows_per_tok, LANES), F32),
                        pltpu.SemaphoreType.DMA((2,))],
        compiler_params=_params("arbitrary"),
        name="combine_ple",
    )(dest_flat, dest_flat, ys, x1, rw, p2d, g_ple.reshape(1, d), wple_bf, wgate_bf)


def kernel(x_prompt, x_sample, cache_k, cache_v, state_hgrn, page_table, p_prompt, p_sample,
           g_mix, w_in, lb_logits, g_hg_out, g_q, g_k, w_lift_hgrn, w_lift_moba, w_out, g_ffn,
           w_router, b_router, w_gate_up, b_gate_up, w_down, b_down, g_ple, w_ple, w_ple_gate):
    depth = g_mix.shape[0]
    assert depth == 1, "single-layer step"
    li = 0
    n_b, seq, d = x_prompt.shape
    n_db, t_new, _ = x_sample.shape
    ps = cache_k.shape[3]
    rows_per_tok = d // LANES
    assert rows_per_tok == SUBLANES and seq % ps == 0

    lb = jnp.cumsum(jax.nn.softmax(lb_logits.astype(F32), axis=0), axis=0)[li]
    w_in_bf = w_in[li].astype(BF16)
    wlh, wlm, wout = (w_lift_hgrn[li].astype(BF16), w_lift_moba[li].astype(BF16),
                      w_out[li].astype(BF16))
    wple_bf, wgate_bf = w_ple[li].astype(BF16), w_ple_gate[li].astype(BF16)

    xp = x_prompt.reshape(n_b * seq, d)
    xs_ = x_sample.reshape(n_db * t_new, d)

    zh_p, q_p, k_p, v_p, gates_p = _inproj(xp, g_mix[li], w_in_bf, lb, g_q[li], g_k[li], ps)
    bh_p, st_p = _hgrn(zh_p, None, g_hg_out[li], n_b, seq)
    bm_p = _moba_prompt(q_p, k_p, v_p, n_b, seq)

    zh_s, q_s, k_s, v_s, gates_s = _inproj(xs_, g_mix[li], w_in_bf, lb, g_q[li], g_k[li], t_new)
    bh_s, st_s = _hgrn(zh_s, state_hgrn[li], g_hg_out[li], n_db, t_new)
    sel, scores = _moba_select(q_s, cache_k, page_table, t_new)
    sel = sel[..., :MOBA_TOPK]
    ppb = MOBA_BLOCK // ps
    logical = (sel[..., None] * ppb + jnp.arange(ppb, dtype=jnp.int32)).reshape(n_db, -1)
    n_pg = page_table.shape[1]
    hit = logical[:, :, None] == jnp.arange(n_pg, dtype=jnp.int32)
    phys = jnp.sum(jnp.where(hit, page_table[:, None, :], 0), axis=-1).reshape(-1)
    bm_s = _moba_sample(q_s, k_s, v_s, scores, cache_v, phys, sel.reshape(-1), n_db, t_new)

    cnt0 = jnp.zeros((N_EXPERTS, LANES), F32)
    x1_p, h2_p, ri_p, rw_p, cnt_p = _merge(xp, bh_p, bm_p, gates_p, wlh, wlm, wout, g_ffn[li],
                                           w_router[li], b_router[li], cnt0)
    x1_s, h2_s, ri_s, rw_s, cnt_all = _merge(xs_, bh_s, bm_s, gates_s, wlh, wlm, wout, g_ffn[li],
                                             w_router[li], b_router[li], cnt_p)

    counts = cnt_all[:, 0].astype(jnp.int32)
    padded = (counts + MOE_TILE - 1) // MOE_TILE * MOE_TILE
    pad_end = jnp.cumsum(padded)
    pad_start = pad_end - padded
    n_assign = (n_b * seq + n_db * t_new) * TOP_K
    n_tiles = -(-(n_assign + N_EXPERTS * (MOE_TILE - 1)) // MOE_TILE)
    tile_lo = jnp.arange(n_tiles, dtype=jnp.int32) * MOE_TILE
    tile_e = jnp.minimum(jnp.sum(pad_end[None, :] <= tile_lo[:, None], axis=1),
                         N_EXPERTS - 1).astype(jnp.int32)
    n_used = (pad_end[-1:] // MOE_TILE).astype(jnp.int32)
    zlo = jnp.concatenate([(pad_start + counts) // ZERO_ROWS, pad_end[-1:] // ZERO_ROWS])
    zhi = jnp.concatenate([pad_end // ZERO_ROWS,
                           jnp.full((1,), n_tiles * MOE_TILE // ZERO_ROWS, jnp.int32)])

    def dest_of(ri):
        n = ri.shape[1]
        tm = _token_tile(n)
        e, rank = ri[:TOP_K], ri[TOP_K:]
        start = jnp.sum(jnp.where(e[:, :, None] == jnp.arange(N_EXPERTS, dtype=jnp.int32),
                                  pad_start, 0), axis=-1)
        return (start + rank).reshape(TOP_K, n // tm, tm).transpose(1, 0, 2).reshape(-1)

    dest_p, dest_s = dest_of(ri_p), dest_of(ri_s)
    xs = _dispatch(zlo.astype(jnp.int32), zhi.astype(jnp.int32), dest_p, dest_s, h2_p, h2_s,
                   n_tiles * MOE_TILE, rows_per_tok)
    ys = _moe(tile_e, n_used, xs, w_gate_up[li], b_gate_up[li], w_down[li], b_down[li],
              rows_per_tok)

    y_p = _combine(dest_p, ys, x1_p, rw_p, p_prompt[li].reshape(n_b * seq, -1), g_ple[li],
                   wple_bf, wgate_bf, rows_per_tok)
    y_s = _combine(dest_s, ys, x1_s, rw_s, p_sample[li].reshape(n_db * t_new, -1), g_ple[li],
                   wple_bf, wgate_bf, rows_per_tok)

    return (y_p.reshape(n_b, seq, d), y_s.reshape(n_db, t_new, d),
            k_p.reshape(1, n_b, seq // ps, N_HEADS, ps, HEAD_DIM),
            v_p.reshape(1, n_b, seq // ps, N_HEADS, ps, HEAD_DIM),
            k_s[None], v_s[None], st_p[None], st_s[None])
```

```python
import functools

import jax
import jax.numpy as jnp
from jax import lax
from jax.experimental import pallas as pl
from jax.experimental.pallas import tpu as pltpu

F32 = jnp.float32
BF16 = jnp.bfloat16
HIGHEST = lax.Precision.HIGHEST

EPS = 1e-6
N_HEADS = 4
HEAD_DIM = 128
HG_CHUNK = 64
MOBA_BLOCK = 256
MOBA_TOPK = 3
MOBA_QGROUP = 16
N_EXPERTS = 32
TOP_K = 4
SWIGLU_LIMIT = 7.0
SWIGLU_ALPHA = 1.702
NEG = -1e30

LANES = 128
SUBLANES = 8
TOKEN_TILE = 256
HGRN_TILE = 512
MERGE_TILE = 512
MOE_TILE = 512
VMEM_LIMIT = 56 * 1024 * 1024


def _params(*sem):
    return pltpu.CompilerParams(dimension_semantics=sem, vmem_limit_bytes=VMEM_LIMIT)


def _sigmoid(x):
    return 1.0 / (1.0 + jnp.exp(-x))


def _silu(x):
    return x * _sigmoid(x)


def _rms(x, g):
    return x * lax.rsqrt(jnp.mean(x * x, axis=-1, keepdims=True) + EPS) * g


def _dot(a, b):
    return jnp.dot(a.astype(BF16), b.astype(BF16), preferred_element_type=F32)


def _dot_nt(a, b):
    return lax.dot_general(a.astype(BF16), b.astype(BF16), (((1,), (1,)), ((), ())),
                           preferred_element_type=F32)


def _dot_tn(a, b):
    return lax.dot_general(a.astype(BF16), b.astype(BF16), (((0,), (0,)), ((), ())),
                           preferred_element_type=F32)


def _token_tile(n):
    return TOKEN_TILE if n % TOKEN_TILE == 0 else n


def _full(shape):
    return pl.BlockSpec(shape, lambda *_: (0,) * len(shape))


def _inproj_kernel(x_ref, gmix_ref, w_ref, lb_ref, gq_ref, gk_ref,
                   zh_ref, q_ref, k_ref, v_ref, gate_ref):
    x = x_ref[...]
    h = _rms(x, gmix_ref[...]).astype(BF16)
    hw = N_HEADS * HEAD_DIM
    d = x.shape[-1]
    tm = x.shape[0]

    def proj(c0, width):
        return jnp.dot(h, w_ref[:, c0:c0 + width], preferred_element_type=F32)

    lb = lb_ref[...]
    zh_ref[:, 0:hw] = _silu(proj(0, hw))
    fa = proj(hw, hw)
    zh_ref[:, hw:2 * hw] = jnp.log(lb + (1.0 - lb) * _sigmoid(fa))
    zh_ref[:, 2 * hw:3 * hw] = (1.0 - lb) * _sigmoid(-fa)
    zh_ref[:, 3 * hw:4 * hw] = proj(2 * hw, hw)
    zh_ref[:, 4 * hw:5 * hw] = _silu(proj(3 * hw, hw))

    qb = proj(4 * hw, hw)
    kb = proj(5 * hw, hw)
    vb = proj(6 * hw, hw)
    rows = k_ref.shape[2]
    for hd in range(N_HEADS):
        sl = slice(hd * HEAD_DIM, (hd + 1) * HEAD_DIM)
        q_ref[:, sl] = _rms(qb[:, sl], gq_ref[...])
        k_ref[:, hd, :, :] = _rms(kb[:, sl], gk_ref[...]).reshape(tm // rows, rows, HEAD_DIM)
        v_ref[:, hd, :, :] = vb[:, sl].reshape(tm // rows, rows, HEAD_DIM)

    gate_ref[:, 0:d] = _sigmoid(proj(7 * hw, d))
    gate_ref[:, d:2 * d] = _sigmoid(proj(7 * hw + d, d))


def _inproj(x2d, g_mix, w_in_bf, lb, g_q, g_k, rows_per_page):
    n, d = x2d.shape
    hw = N_HEADS * HEAD_DIM
    tm = _token_tile(n)
    r = rows_per_page
    n_in = w_in_bf.shape[1]
    out_shape = (
        jax.ShapeDtypeStruct((n, 5 * hw), F32),
        jax.ShapeDtypeStruct((n, hw), F32),
        jax.ShapeDtypeStruct((n // r, N_HEADS, r, HEAD_DIM), F32),
        jax.ShapeDtypeStruct((n // r, N_HEADS, r, HEAD_DIM), F32),
        jax.ShapeDtypeStruct((n, 2 * d), F32),
    )
    row = lambda w: pl.BlockSpec((tm, w), lambda i: (i, 0))
    page = pl.BlockSpec((tm // r, N_HEADS, r, HEAD_DIM), lambda i: (i, 0, 0, 0))
    return pl.pallas_call(
        _inproj_kernel,
        grid=(n // tm,),
        in_specs=[row(d), _full((1, d)),
                  pl.BlockSpec((d, n_in), lambda i: (0, 0), pipeline_mode=pl.Buffered(1)),
                  _full((1, hw)), _full((1, HEAD_DIM)), _full((1, HEAD_DIM))],
        out_specs=(row(5 * hw), row(hw), page, page, row(2 * d)),
        out_shape=out_shape,
        compiler_params=_params("parallel"),
        name="inproj",
    )(x2d, g_mix.reshape(1, d), w_in_bf, lb.reshape(1, hw), g_q.reshape(1, HEAD_DIM),
      g_k.reshape(1, HEAD_DIM))


def _hgrn_kernel(*refs, chunk, has_s0, exact_mxu):
    if has_s0:
        zh_ref, s0_ref, g_ref, o_ref, sout_ref, st_ref = refs
    else:
        zh_ref, g_ref, o_ref, sout_ref, st_ref = refs
        s0_ref = None
    t = pl.program_id(1)
    hw = N_HEADS * HEAD_DIM
    n_sub = sout_ref.shape[0]
    tb = zh_ref.shape[0] // n_sub

    @pl.when(t == 0)
    def _():
        for s in range(n_sub):
            for hd in range(N_HEADS):
                if has_s0:
                    st_ref[s * N_HEADS + hd] = s0_ref[s, hd].T
                else:
                    st_ref[s * N_HEADS + hd] = jnp.zeros((HEAD_DIM, HEAD_DIM), F32)

    if exact_mxu:
        mm_nt = lambda a, b: lax.dot_general(a, b, (((1,), (1,)), ((), ())), precision=HIGHEST,
                                             preferred_element_type=F32)
        mm = lambda a, b: jnp.dot(a, b, precision=HIGHEST, preferred_element_type=F32)
        mm_tn = lambda a, b: lax.dot_general(a, b, (((0,), (0,)), ((), ())), precision=HIGHEST,
                                             preferred_element_type=F32)
    else:
        mm_nt, mm, mm_tn = _dot_nt, _dot, _dot_tn

    ri = lax.broadcasted_iota(jnp.int32, (chunk, chunk), 0)
    ci = lax.broadcasted_iota(jnp.int32, (chunk, chunk), 1)
    causal = ci <= ri
    tril = jnp.where(causal, 1.0, 0.0).astype(F32)

    for s in range(n_sub):
        sts = [st_ref[s * N_HEADS + hd] for hd in range(N_HEADS)]
        for c in range(tb // chunk):
            rows = slice(s * tb + c * chunk, s * tb + (c + 1) * chunk)
            k_all = zh_ref[rows, 2 * hw:3 * hw]
            b_all = jnp.dot(tril, zh_ref[rows, hw:2 * hw], precision=HIGHEST,
                            preferred_element_type=F32)
            b_end = b_all[chunk - 1:chunk, :]
            qd_all = zh_ref[rows, 0:hw] * jnp.exp(b_all)
            kd_all = k_all * jnp.exp(-b_all)
            kk_all = k_all * jnp.exp(b_end - b_all)
            decay = jnp.exp(b_end)
            for hd in range(N_HEADS):
                sl = slice(hd * HEAD_DIM, (hd + 1) * HEAD_DIM)
                v = zh_ref[rows, 3 * hw + hd * HEAD_DIM:3 * hw + (hd + 1) * HEAD_DIM]
                og = zh_ref[rows, 4 * hw + hd * HEAD_DIM:4 * hw + (hd + 1) * HEAD_DIM]
                qd = qd_all[:, sl]
                att = jnp.where(causal, mm_nt(qd, kd_all[:, sl]), 0.0)
                o = mm(att, v) + mm_nt(qd, sts[hd])
                sts[hd] = decay[:, sl] * sts[hd] + mm_tn(v, kk_all[:, sl])
                o_ref[rows, sl] = _rms(o, g_ref[...]) * og
        for hd in range(N_HEADS):
            st_ref[s * N_HEADS + hd] = sts[hd]

    @pl.when(t == pl.num_programs(1) - 1)
    def _():
        for s in range(n_sub):
            for hd in range(N_HEADS):
                sout_ref[s, hd] = st_ref[s * N_HEADS + hd].T


def _hgrn(zh, s0, g_hg_out, n_seq, seq_len):
    hw = N_HEADS * HEAD_DIM
    chunk = min(HG_CHUNK, seq_len)
    assert seq_len % chunk == 0
    tb = min(HGRN_TILE, seq_len)
    assert seq_len % tb == 0 and tb % chunk == 0
    n_t = seq_len // tb
    n_sub = 1
    if n_t == 1:
        n_sub = max(d for d in range(1, HGRN_TILE // (8 * tb) + 1) if n_seq % d == 0)
    n_steps, tb = n_seq // n_sub, tb * n_sub
    has_s0 = s0 is not None
    state_spec = pl.BlockSpec((n_sub, N_HEADS, HEAD_DIM, HEAD_DIM), lambda b, t: (b, 0, 0, 0))
    in_specs = [pl.BlockSpec((tb, 5 * hw), lambda b, t: (b * n_t + t, 0))]
    args = [zh]
    if has_s0:
        in_specs.append(state_spec)
        args.append(s0)
    in_specs.append(_full((1, HEAD_DIM)))
    args.append(g_hg_out.reshape(1, HEAD_DIM))
    return pl.pallas_call(
        functools.partial(_hgrn_kernel, chunk=chunk, has_s0=has_s0, exact_mxu=chunk < 16),
        grid=(n_steps, n_t),
        in_specs=in_specs,
        out_specs=(pl.BlockSpec((tb, hw), lambda b, t: (b * n_t + t, 0)), state_spec),
        out_shape=(jax.ShapeDtypeStruct((n_seq * seq_len, hw), F32),
                   jax.ShapeDtypeStruct((n_seq, N_HEADS, HEAD_DIM, HEAD_DIM), F32)),
        scratch_shapes=[pltpu.VMEM((n_sub * N_HEADS, HEAD_DIM, HEAD_DIM), F32)],
        compiler_params=_params("parallel", "arbitrary"),
        name="hgrn_prompt" if not has_s0 else "hgrn_sample",
    )(*args)


def _topk_rank(gsc, n_valid):
    rows, n_col = gsc.shape
    blk = lax.broadcasted_iota(jnp.int32, (rows, n_col), 1)
    rank = jnp.zeros((rows, n_col), jnp.int32)
    for m in range(n_col):
        sm = gsc[:, m:m + 1]
        beats = jnp.where(sm > gsc, 1, jnp.where(sm == gsc, jnp.where(blk > m, 1, 0), 0))
        rank = rank + jnp.where(m < n_valid, beats, 0)
    return blk, rank


def _moba_prompt_kernel(q_ref, k_ref, v_ref, o_ref, kmean_ref, kbf_ref, vt_ref, osc_ref, perm_ref):
    n = pl.program_id(1)
    n_b = kmean_ref.shape[1]
    ps = k_ref.shape[2]
    ppb = MOBA_BLOCK // ps
    scale = HEAD_DIM ** -0.5
    heads = range(N_HEADS)

    @pl.when(n == 0)
    def _():
        for hd in heads:
            for m in range(n_b):
                kb = k_ref[m * ppb:(m + 1) * ppb, hd].reshape(MOBA_BLOCK, HEAD_DIM)
                kmean_ref[hd, m:m + 1, :] = jnp.mean(kb, axis=0, keepdims=True)
                kbf_ref[hd, m] = kb.astype(BF16)
                for pp in range(ppb):
                    vt_ref[hd, m, :, pp * ps:(pp + 1) * ps] = v_ref[m * ppb + pp, hd].T.astype(BF16)

    def block(m, hd):
        return kbf_ref[hd, m], vt_ref[hd, m]

    key_i = lax.broadcasted_iota(jnp.int32, (MOBA_BLOCK, MOBA_BLOCK), 0)
    qry_i = lax.broadcasted_iota(jnp.int32, (MOBA_BLOCK, MOBA_BLOCK), 1)
    blk = lax.broadcasted_iota(jnp.int32, (n_b, MOBA_BLOCK), 0)
    qbs, sels, init = [], [], []
    for hd in heads:
        q = q_ref[:, hd * HEAD_DIM:(hd + 1) * HEAD_DIM]
        gsc = lax.dot_general(kmean_ref[hd], q, (((1,), (1,)), ((), ())), precision=HIGHEST,
                              preferred_element_type=F32)
        rank = jnp.zeros((n_b, MOBA_BLOCK), jnp.int32)
        for m in range(n_b):
            sm = gsc[m:m + 1, :]
            beats = jnp.where(sm > gsc, 1, jnp.where(sm == gsc, jnp.where(blk > m, 1, 0), 0))
            rank = rank + jnp.where(m < n, beats, 0)
        sels.append(jnp.where((blk < n) & (rank < MOBA_TOPK), 1.0, 0.0))
        qbs.append((q * scale).astype(BF16))
    raw = [_dot_nt(kbf_ref[hd, n], qbs[hd]) for hd in heads]
    mid = []
    for hd in heads:
        s = jnp.where(key_i <= qry_i, raw[hd], NEG)
        m0 = jnp.max(s, axis=0, keepdims=True)
        p = jnp.exp(s - m0)
        mid.append((m0, jnp.sum(p, axis=0, keepdims=True), p.astype(BF16)))
    for hd in heads:
        m0, l0, p = mid[hd]
        init.append((m0, l0, _dot(vt_ref[hd, n], p)))

    def body(m, carry):
        raw = [_dot_nt(kbf_ref[hd, m], qbs[hd]) for hd in heads]
        mid = []
        for hd in heads:
            m_i, l_i, _ = carry[hd]
            picked = jnp.sum(jnp.where(blk == m, sels[hd], 0.0), axis=0, keepdims=True)
            s = jnp.where(picked > 0.0, raw[hd], NEG)
            m_new = jnp.maximum(m_i, jnp.max(s, axis=0, keepdims=True))
            alpha = jnp.exp(m_i - m_new)
            p = jnp.exp(s - m_new)
            mid.append((m_new, alpha, alpha * l_i + jnp.sum(p, axis=0, keepdims=True),
                        p.astype(BF16)))
        out = []
        for hd in heads:
            m_new, alpha, l_new, p = mid[hd]
            out.append((m_new, l_new, alpha * carry[hd][2] + _dot(vt_ref[hd, m], p)))
        return tuple(out)

    final = lax.fori_loop(0, n, body, tuple(init))
    n_grp = MOBA_BLOCK // MOBA_QGROUP
    for hd in heads:
        _, l_f, acc_f = final[hd]
        osc_ref[hd] = (acc_f / l_f).T
        for qg in range(MOBA_QGROUP):
            perm_ref[qg, pl.ds(hd, n_grp, stride=N_HEADS), :] = (
                osc_ref[hd, pl.ds(qg, n_grp, stride=MOBA_QGROUP), :])
    for qg in range(MOBA_QGROUP):
        qh, ql = divmod(qg, N_HEADS)
        o_ref[ql, :, qh * HEAD_DIM:(qh + 1) * HEAD_DIM] = perm_ref[qg]


def _moba_prompt(q, k_pages, v_pages, n_seq, seq_len):
    assert seq_len % MOBA_BLOCK == 0 and MOBA_QGROUP == N_HEADS * N_HEADS
    n_b = seq_len // MOBA_BLOCK
    ps = k_pages.shape[2]
    assert MOBA_BLOCK % ps == 0
    pps = seq_len // ps
    hw = N_HEADS * HEAD_DIM
    n_grp = MOBA_BLOCK // MOBA_QGROUP
    qspec = pl.BlockSpec((MOBA_BLOCK, hw), lambda b, n: (b * n_b + n, 0))
    kvspec = pl.BlockSpec((pps, N_HEADS, ps, HEAD_DIM), lambda b, n: (b, 0, 0, 0))
    out = pl.pallas_call(
        _moba_prompt_kernel,
        grid=(n_seq, n_b),
        in_specs=[qspec, kvspec, kvspec],
        out_specs=pl.BlockSpec((None, N_HEADS, n_grp * N_HEADS, hw), lambda b, n: (b, 0, n, 0)),
        out_shape=jax.ShapeDtypeStruct((n_seq, N_HEADS, seq_len // N_HEADS, hw), F32),
        scratch_shapes=[pltpu.VMEM((N_HEADS, n_b, HEAD_DIM), F32),
                        pltpu.VMEM((N_HEADS, n_b, MOBA_BLOCK, HEAD_DIM), BF16),
                        pltpu.VMEM((N_HEADS, n_b, HEAD_DIM, MOBA_BLOCK), BF16),
                        pltpu.VMEM((N_HEADS, MOBA_BLOCK, HEAD_DIM), F32),
                        pltpu.VMEM((MOBA_QGROUP, n_grp * N_HEADS, HEAD_DIM), F32)],
        compiler_params=_params("parallel", "arbitrary"),
        name="moba_prompt",
    )(q, k_pages, v_pages)
    return out.reshape(n_seq * seq_len, hw)


PAGES_PER_STEP = 16


def _moba_select_kernel(pt_ref, q_ref, ck_hbm, sel_ref, sc_ref, kbuf, kmean_ref, qbd_ref, sem,
                        *, n_j):
    g = pl.program_id(0)
    n_g = pl.num_programs(0)
    j = g % n_j
    ps = kbuf.shape[3]
    ppb = MOBA_BLOCK // ps
    bps = PAGES_PER_STEP // ppb
    n_blk = kmean_ref.shape[1]
    t_new = q_ref.shape[0]
    scale = HEAD_DIM ** -0.5

    def copies(step, slot):
        return [pltpu.make_async_copy(ck_hbm.at[0, pt_ref[step * PAGES_PER_STEP + i]],
                                      kbuf.at[slot, i], sem.at[slot])
                for i in range(PAGES_PER_STEP)]

    @pl.when(g == 0)
    def _():
        for i, c in enumerate(copies(0, 0)):
            c.start(priority=i % 2)

    @pl.when(g + 1 < n_g)
    def _():
        for i, c in enumerate(copies(g + 1, (g + 1) % 2)):
            c.start(priority=i % 2)

    slot = g % 2
    for c in copies(g, slot):
        c.wait()
    page_refs = [kbuf.at[slot, i] for i in range(PAGES_PER_STEP)]

    @pl.when(j == 0)
    def _():
        zero = jnp.zeros((t_new, HEAD_DIM), F32)
        for p in range(N_HEADS // 2):
            q0 = q_ref[:, (2 * p) * HEAD_DIM:(2 * p + 1) * HEAD_DIM] * scale
            q1 = q_ref[:, (2 * p + 1) * HEAD_DIM:(2 * p + 2) * HEAD_DIM] * scale
            qbd_ref[p] = jnp.concatenate([jnp.concatenate([q0, zero], axis=-1),
                                          jnp.concatenate([zero, q1], axis=-1)], axis=0).astype(BF16)

    for i in range(PAGES_PER_STEP):
        for p in range(N_HEADS // 2):
            kcat = jnp.concatenate([page_refs[i][2 * p], page_refs[i][2 * p + 1]], axis=-1)
            sc_ref[p, i // ppb, :, (i % ppb) * ps:(i % ppb + 1) * ps] = _dot_nt(qbd_ref[p], kcat)

    for i in range(bps):
        for hd in range(N_HEADS):
            tot = jnp.zeros((1, HEAD_DIM), F32)
            for pp in range(ppb):
                tot = tot + jnp.sum(page_refs[i * ppb + pp][hd], axis=0, keepdims=True)
            kmean_ref[hd, pl.ds(j * bps + i, 1), :] = tot / float(MOBA_BLOCK)

    @pl.when(j == n_j - 1)
    def _():
        for hd in range(N_HEADS):
            q = q_ref[:, hd * HEAD_DIM:(hd + 1) * HEAD_DIM]
            gsc = lax.dot_general(q, kmean_ref[hd], (((1,), (1,)), ((), ())), precision=HIGHEST,
                                  preferred_element_type=F32)
            blk, rank = _topk_rank(gsc, n_blk)
            lane = lax.broadcasted_iota(jnp.int32, (t_new, LANES), 1)
            out = jnp.zeros((t_new, LANES), jnp.int32)
            for r in range(MOBA_TOPK):
                idx = jnp.sum(jnp.where(rank == r, blk.astype(F32), 0.0), axis=-1, keepdims=True)
                out = jnp.where(lane == r, idx.astype(jnp.int32), out)
            sel_ref[0, hd] = out


def _moba_select(q_s, cache_k, page_table, t_new):
    n_seq, n_pg = page_table.shape
    _, _, _, ps, _ = cache_k.shape
    assert (n_pg * ps) % MOBA_BLOCK == 0, "cached rows must end on a MoBA block boundary"
    assert n_pg % PAGES_PER_STEP == 0 and MOBA_BLOCK % ps == 0
    n_blk = n_pg * ps // MOBA_BLOCK
    assert n_blk >= MOBA_TOPK
    hw = N_HEADS * HEAD_DIM
    bps = PAGES_PER_STEP * ps // MOBA_BLOCK

    n_j = n_pg // PAGES_PER_STEP
    grid_spec = pltpu.PrefetchScalarGridSpec(
        num_scalar_prefetch=1,
        grid=(n_seq * n_j,),
        in_specs=[pl.BlockSpec((t_new, hw), lambda g, pt: (g // n_j, 0)),
                  pl.BlockSpec(memory_space=pl.ANY)],
        out_specs=(
            pl.BlockSpec((1, N_HEADS, t_new, LANES), lambda g, pt: (g // n_j, 0, 0, 0)),
            pl.BlockSpec((None, N_HEADS // 2, bps, 2 * t_new, MOBA_BLOCK),
                         lambda g, pt: (g // n_j, 0, g % n_j, 0, 0))),
        scratch_shapes=[pltpu.VMEM((2, PAGES_PER_STEP, N_HEADS, ps, HEAD_DIM), F32),
                        pltpu.VMEM((N_HEADS, n_blk, HEAD_DIM), F32),
                        pltpu.VMEM((N_HEADS // 2, 2 * t_new, 2 * HEAD_DIM), BF16),
                        pltpu.SemaphoreType.DMA((2,))],
    )
    return pl.pallas_call(
        functools.partial(_moba_select_kernel, n_j=n_j),
        grid_spec=grid_spec,
        out_shape=(jax.ShapeDtypeStruct((n_seq, N_HEADS, t_new, LANES), jnp.int32),
                   jax.ShapeDtypeStruct((n_seq, N_HEADS // 2, n_blk, 2 * t_new, MOBA_BLOCK), F32)),
        compiler_params=_params("arbitrary"),
        name="moba_select",
    )(page_table.reshape(-1), q_s, cache_k)


def _moba_sample_kernel(pages_ref, blk_ref, q_ref, kn_ref, vn_ref, sc_ref, cv_hbm, o_ref,
                        vbuf, ssel_ref, sem, *, n_fetch):
    g = pl.program_id(0)
    n_g = pl.num_programs(0)
    ps = vbuf.shape[2]
    t_new = q_ref.shape[0]
    per_q = n_fetch // t_new * ps
    scale = HEAD_DIM ** -0.5

    def copies(step, slot):
        hd = step % N_HEADS
        return [pltpu.make_async_copy(cv_hbm.at[0, pages_ref[step * n_fetch + i], hd],
                                      vbuf.at[slot, i], sem.at[slot]) for i in range(n_fetch)]

    @pl.when(g == 0)
    def _():
        for i, c in enumerate(copies(0, 0)):
            c.start(priority=i % 2)

    @pl.when(g + 1 < n_g)
    def _():
        for i, c in enumerate(copies(g + 1, (g + 1) % 2)):
            c.start(priority=i % 2)

    for t in range(t_new):
        for r in range(MOBA_TOPK):
            blk = blk_ref[(g * t_new + t) * MOBA_TOPK + r]
            ssel_ref[t:t + 1, r * MOBA_BLOCK:(r + 1) * MOBA_BLOCK] = sc_ref[blk, t:t + 1, :]
    s = ssel_ref[...]

    pad = jnp.zeros((2 * SUBLANES - t_new, HEAD_DIM), F32)
    qp = jnp.concatenate([q_ref[...] * scale, pad], axis=0)
    kn = jnp.concatenate([kn_ref[...], pad], axis=0)
    vn = jnp.concatenate([vn_ref[...], pad], axis=0)
    s_new = _dot_nt(qp, kn)[:t_new]
    rn = lax.broadcasted_iota(jnp.int32, s_new.shape, 0)
    cn = lax.broadcasted_iota(jnp.int32, s_new.shape, 1)
    s_new = jnp.where((cn <= rn) & (cn < t_new), s_new, NEG)
    m = jnp.maximum(jnp.max(s, axis=-1, keepdims=True), jnp.max(s_new, axis=-1, keepdims=True))
    p = jnp.exp(s - m)
    p_new = jnp.exp(s_new - m)
    den = jnp.sum(p, axis=-1, keepdims=True) + jnp.sum(p_new, axis=-1, keepdims=True)

    row = lax.broadcasted_iota(jnp.int32, p.shape, 0)
    zero_rows = jnp.zeros((2 * SUBLANES - t_new, t_new * per_q), F32)
    p_wide = jnp.concatenate([jnp.where(row == t, p, 0.0) for t in range(t_new)], axis=-1)
    p_wide = jnp.concatenate([p_wide, zero_rows], axis=0)
    p_new = jnp.concatenate([p_new, jnp.zeros((2 * SUBLANES - t_new, p_new.shape[1]), F32)], axis=0)

    slot = g % 2
    for c in copies(g, slot):
        c.wait()
    vc = vbuf[slot].reshape(n_fetch * ps, HEAD_DIM)
    o = _dot(p_wide, vc) + _dot(p_new, vn)
    o_ref[...] = o[:t_new] / den


def _moba_sample(q_s, k_new, v_new, scores, cache_v, phys_pages, sel_blocks, n_seq, t_new):
    ps = cache_v.shape[3]
    n_fetch = phys_pages.shape[0] // (n_seq * N_HEADS)
    n_blk = scores.shape[2]
    head = lambda g, *_: (g // N_HEADS, g % N_HEADS)
    grid_spec = pltpu.PrefetchScalarGridSpec(
        num_scalar_prefetch=2,
        grid=(n_seq * N_HEADS,),
        in_specs=[
            pl.BlockSpec((t_new, HEAD_DIM), head),
            pl.BlockSpec((None, None, t_new, HEAD_DIM), lambda g, *_: head(g) + (0, 0)),
            pl.BlockSpec((None, None, t_new, HEAD_DIM), lambda g, *_: head(g) + (0, 0)),
            pl.BlockSpec((None, None, n_blk, t_new, MOBA_BLOCK),
                         lambda g, *_: (g // N_HEADS, (g % N_HEADS) // 2, 0, g % 2, 0)),
            pl.BlockSpec(memory_space=pl.ANY),
        ],
        out_specs=pl.BlockSpec((t_new, HEAD_DIM), head),
        scratch_shapes=[
            pltpu.VMEM((2, n_fetch, ps, HEAD_DIM), F32),
            pltpu.VMEM((t_new, MOBA_TOPK * MOBA_BLOCK), F32),
            pltpu.SemaphoreType.DMA((2,)),
        ],
    )
    return pl.pallas_call(
        functools.partial(_moba_sample_kernel, n_fetch=n_fetch),
        grid_spec=grid_spec,
        out_shape=jax.ShapeDtypeStruct(q_s.shape, F32),
        compiler_params=_params("arbitrary"),
        name="moba_sample",
    )(phys_pages, sel_blocks, q_s, k_new, v_new, scores, cache_v)


def _merge_kernel(x_ref, bh_ref, bm_ref, gate_ref, wlh_ref, wlm_ref, wout_ref, gffn_ref,
                  wr_ref, br_ref, cnt0_ref, x1_ref, h2_ref, ri_ref, rw_ref, cnt_ref, run_ref):
    i = pl.program_id(0)
    tm, d = x_ref.shape

    @pl.when(i == 0)
    def _():
        run_ref[...] = cnt0_ref[...]

    mixed = (gate_ref[:, 0:d] * _dot(bh_ref[...], wlh_ref[...])
             + gate_ref[:, d:2 * d] * _dot(bm_ref[...], wlm_ref[...]))
    x1 = x_ref[...] + _dot(mixed, wout_ref[...])
    x1_ref[...] = x1
    h2 = _rms(x1, gffn_ref[...])
    n_chunk = d // LANES
    for j in range(n_chunk):
        h2_ref[pl.ds(j, tm, stride=n_chunk), :] = h2[:, j * LANES:(j + 1) * LANES]

    logits = lax.dot_general(wr_ref[...], h2, (((1,), (1,)), ((), ())), precision=HIGHEST,
                             preferred_element_type=F32) + br_ref[...]
    sub_e = lax.broadcasted_iota(jnp.int32, (N_EXPERTS, tm), 0).astype(F32)
    lg = logits
    vals, ids, hots = [], [], []
    for _ in range(TOP_K):
        mx = jnp.max(lg, axis=0, keepdims=True)
        idx = jnp.min(jnp.where(lg == mx, sub_e, float(N_EXPERTS)), axis=0, keepdims=True)
        hot = sub_e == idx
        vals.append(mx)
        ids.append(idx)
        hots.append(hot)
        lg = jnp.where(hot, -jnp.inf, lg)
    exps = [jnp.exp(v - vals[0]) for v in vals]
    den = exps[0] + exps[1] + exps[2] + exps[3]

    chosen = jnp.zeros((N_EXPERTS, tm), F32)
    for hot in hots:
        chosen = chosen + jnp.where(hot, 1.0, 0.0)
    r_i = lax.broadcasted_iota(jnp.int32, (tm, tm), 0)
    c_i = lax.broadcasted_iota(jnp.int32, (tm, tm), 1)
    earlier = jnp.where(r_i < c_i, 1.0, 0.0)
    base = _dot(chosen, earlier) + run_ref[:, 0:1]
    run_ref[...] = run_ref[...] + jnp.sum(chosen, axis=1, keepdims=True)
    cnt_ref[...] = run_ref[...]

    sub_i = lax.broadcasted_iota(jnp.int32, (2 * TOP_K, tm), 0)
    sub_w = lax.broadcasted_iota(jnp.int32, (LANES, tm), 0)
    out_i = jnp.zeros((2 * TOP_K, tm), jnp.int32)
    out_w = jnp.zeros((LANES, tm), F32)
    for k in range(TOP_K):
        r_k = jnp.sum(jnp.where(hots[k], base, 0.0), axis=0, keepdims=True)
        out_i = jnp.where(sub_i == k, ids[k].astype(jnp.int32), out_i)
        out_i = jnp.where(sub_i == TOP_K + k, r_k.astype(jnp.int32), out_i)
        out_w = jnp.where(sub_w == k, exps[k] / den, out_w)
    ri_ref[...] = out_i
    rw_ref[...] = out_w.T


def _merge(x2d, bh, bm, gates, wlh, wlm, wout, g_ffn, w_router, b_router, cnt0):
    n, d = x2d.shape
    hw = N_HEADS * HEAD_DIM
    tm = MERGE_TILE if n % MERGE_TILE == 0 else n
    n_chunk = d // LANES
    row = lambda w: pl.BlockSpec((tm, w), lambda i: (i, 0))
    return pl.pallas_call(
        _merge_kernel,
        grid=(n // tm,),
        in_specs=[row(d), row(hw), row(hw), row(2 * d), _full((hw, d)), _full((hw, d)),
                  _full((d, d)), _full((1, d)), _full((N_EXPERTS, d)), _full((N_EXPERTS, 1)),
                  _full((N_EXPERTS, LANES))],
        out_specs=(row(d), pl.BlockSpec((tm * n_chunk, LANES), lambda i: (i, 0)),
                   pl.BlockSpec((2 * TOP_K, tm), lambda i: (0, i)), row(LANES),
                   _full((N_EXPERTS, LANES))),
        out_shape=(jax.ShapeDtypeStruct((n, d), F32),
                   jax.ShapeDtypeStruct((n * n_chunk, LANES), F32),
                   jax.ShapeDtypeStruct((2 * TOP_K, n), jnp.int32),
                   jax.ShapeDtypeStruct((n, LANES), F32),
                   jax.ShapeDtypeStruct((N_EXPERTS, LANES), F32)),
        scratch_shapes=[pltpu.VMEM((N_EXPERTS, LANES), F32)],
        compiler_params=_params("arbitrary"),
        name="merge_router",
    )(x2d, bh, bm, gates, wlh, wlm, wout, g_ffn.reshape(1, d), w_router.T,
      b_router.reshape(N_EXPERTS, 1), cnt0)


ZERO_ROWS = 64


def _dispatch_kernel(zlo_ref, zhi_ref, dest_p_ref, dest_s_ref, h2p_ref, h2s_ref, xs_hbm,
                     zbuf, sem, *, n_p, rows_per_tok):
    i = pl.program_id(0)
    r = rows_per_tok
    zrows = ZERO_ROWS * r

    def zero_copy(s):
        return pltpu.make_async_copy(zbuf, xs_hbm.at[pl.ds(pl.multiple_of(s * zrows, zrows), zrows)],
                                     sem.at[0])

    @pl.when(i == 0)
    def _():
        zbuf[...] = jnp.zeros(zbuf.shape, F32)

        def each_range(fn):
            def outer(e, carry):
                def inner(s, c):
                    fn(s)
                    return c
                return lax.fori_loop(zlo_ref[e], zhi_ref[e], inner, carry)
            lax.fori_loop(0, zlo_ref.shape[0], outer, 0)

        each_range(lambda s: zero_copy(s).start())
        each_range(lambda s: zero_copy(s).wait())

    def scatter(h2_ref, dest_ref):
        tm = h2_ref.shape[0] // r

        unroll = 8 if tm % 8 == 0 else 1

        def body(tu, carry):
            for u in range(unroll):
                t = tu * unroll + u
                src = h2_ref.at[pl.ds(pl.multiple_of(t * r, r), r)]
                for k in range(TOP_K):
                    dst = xs_hbm.at[pl.ds(pl.multiple_of(dest_ref[k * tm + t] * r, r), r)]
                    pltpu.make_async_copy(src, dst, sem.at[1]).start(priority=k % 2)
            return carry

        lax.fori_loop(0, tm // unroll, body, 0)
        n_rows = tm * TOP_K * r
        pltpu.make_async_copy(xs_hbm.at[pl.ds(0, n_rows)], xs_hbm.at[pl.ds(0, n_rows)],
                              sem.at[1]).wait()

    @pl.when(i < n_p)
    def _():
        scatter(h2p_ref, dest_p_ref)

    @pl.when(i >= n_p)
    def _():
        scatter(h2s_ref, dest_s_ref)


def _dispatch(zlo, zhi, dest_p, dest_s, h2_p, h2_s, n_slots, rows_per_tok):
    r = rows_per_tok
    tm_p = _token_tile(h2_p.shape[0] // r)
    tm_s = _token_tile(h2_s.shape[0] // r)
    n_p = h2_p.shape[0] // (tm_p * r)
    n_s = h2_s.shape[0] // (tm_s * r)
    first = lambda i, *_: (jnp.minimum(i, n_p - 1),)
    second = lambda i, *_: (jnp.maximum(i - n_p, 0),)
    grid_spec = pltpu.PrefetchScalarGridSpec(
        num_scalar_prefetch=2,
        grid=(n_p + n_s,),
        in_specs=[
            pl.BlockSpec((tm_p * TOP_K,), first, memory_space=pltpu.SMEM),
            pl.BlockSpec((tm_s * TOP_K,), second, memory_space=pltpu.SMEM),
            pl.BlockSpec((tm_p * r, LANES), lambda i, *_: first(i) + (0,)),
            pl.BlockSpec((tm_s * r, LANES), lambda i, *_: second(i) + (0,)),
        ],
        out_specs=pl.BlockSpec(memory_space=pl.ANY),
        scratch_shapes=[pltpu.VMEM((ZERO_ROWS * r, LANES), F32), pltpu.SemaphoreType.DMA((2,))],
    )
    return pl.pallas_call(
        functools.partial(_dispatch_kernel, n_p=n_p, rows_per_tok=r),
        grid_spec=grid_spec,
        out_shape=jax.ShapeDtypeStruct((n_slots * r, LANES), F32),
        compiler_params=_params("arbitrary"),
        name="moe_dispatch",
    )(zlo, zhi, dest_p, dest_s, h2_p, h2_s)


def _moe_kernel(te_ref, nu_ref, xs_ref, wgu_f32, bgu_ref, wdn_f32, bdn_ref, ys_ref,
                wgu_ref, wdn_ref):
    i = pl.program_id(0)
    n_chunk = xs_ref.shape[0] // MOE_TILE
    d_ff = wdn_ref.shape[0]
    used = i < nu_ref[0]
    new_expert = (i == 0) | (te_ref[i] != te_ref[jnp.maximum(i - 1, 0)])

    @pl.when(used & new_expert)
    def _():
        rows = 128
        for c in range(wgu_ref.shape[0] // rows):
            wgu_ref[c * rows:(c + 1) * rows, :] = wgu_f32[c * rows:(c + 1) * rows, :].astype(BF16)
        for c in range(wdn_ref.shape[0] // rows):
            wdn_ref[c * rows:(c + 1) * rows, :] = wdn_f32[c * rows:(c + 1) * rows, :].astype(BF16)

    @pl.when(used)
    def _():
        x = jnp.concatenate(
            [xs_ref[pl.ds(j, MOE_TILE, stride=n_chunk), :] for j in range(n_chunk)], axis=-1)
        gu = _dot(x, wgu_ref[...]) + bgu_ref[...]
        g_ = jnp.minimum(gu[:, :d_ff], SWIGLU_LIMIT)
        u_ = jnp.clip(gu[:, d_ff:], -SWIGLU_LIMIT, SWIGLU_LIMIT)
        act = (u_ + 1.0) * g_ * _sigmoid(SWIGLU_ALPHA * g_)
        y = _dot(act, wdn_ref[...]) + bdn_ref[...]
        for j in range(n_chunk):
            ys_ref[pl.ds(j, MOE_TILE, stride=n_chunk), :] = y[:, j * LANES:(j + 1) * LANES]

    @pl.when(i >= nu_ref[0])
    def _():
        ys_ref[...] = jnp.zeros(ys_ref.shape, F32)


def _moe(tile_e, n_used, xs, w_gu, b_gu, w_dn, b_dn, rows_per_tok):
    n_tiles = xs.shape[0] // (MOE_TILE * rows_per_tok)
    d, two_ff = w_gu.shape[1:]
    d_ff = w_dn.shape[1]
    last = lambda i, nu: jnp.minimum(i, nu[0] - 1)
    grid_spec = pltpu.PrefetchScalarGridSpec(
        num_scalar_prefetch=2,
        grid=(n_tiles,),
        in_specs=[
            pl.BlockSpec((MOE_TILE * rows_per_tok, LANES), lambda i, te, nu: (last(i, nu), 0)),
            pl.BlockSpec((None, d, two_ff), lambda i, te, nu: (te[last(i, nu)], 0, 0)),
            pl.BlockSpec((None, 1, two_ff), lambda i, te, nu: (te[last(i, nu)], 0, 0)),
            pl.BlockSpec((None, d_ff, d), lambda i, te, nu: (te[last(i, nu)], 0, 0)),
            pl.BlockSpec((None, 1, d), lambda i, te, nu: (te[last(i, nu)], 0, 0)),
        ],
        out_specs=pl.BlockSpec((MOE_TILE * rows_per_tok, LANES), lambda i, te, nu: (i, 0)),
        scratch_shapes=[pltpu.VMEM((d, two_ff), BF16), pltpu.VMEM((d_ff, d), BF16)],
    )
    return pl.pallas_call(
        _moe_kernel,
        grid_spec=grid_spec,
        out_shape=jax.ShapeDtypeStruct(xs.shape, F32),
        compiler_params=_params("arbitrary"),
        name="moe_ffn",
    )(tile_e, n_used, xs, w_gu, b_gu.reshape(N_EXPERTS, 1, two_ff), w_dn,
      b_dn.reshape(N_EXPERTS, 1, d))


def _combine_kernel(dest_ref, dest_next_ref, ys_hbm, x1_ref, rw_ref, p_ref, gple_ref, wple_ref,
                    wgate_ref, y_ref, buf, sem, *, rows_per_tok):
    i = pl.program_id(0)
    n_i = pl.num_programs(0)
    tm, d = x1_ref.shape
    r = rows_per_tok

    def row_copy(d_ref, slot, t, k):
        src = ys_hbm.at[pl.ds(pl.multiple_of(d_ref[k * tm + t] * r, r), r)]
        dst = buf.at[slot, pl.ds(pl.multiple_of((k * tm + t) * r, r), r)]
        return pltpu.make_async_copy(src, dst, sem.at[slot])

    def wait_slot(slot):
        pltpu.make_async_copy(buf.at[slot], buf.at[slot], sem.at[slot]).wait()

    slot = i % 2

    @pl.when(i == 0)
    def _():
        def body(t, carry):
            for k in range(TOP_K):
                row_copy(dest_ref, 0, t, k).start(priority=k % 2)
            return carry
        lax.fori_loop(0, tm, body, 0)

    for t in range(tm):
        for k in range(TOP_K):
            row_copy(dest_next_ref, 1 - slot, t, k).start(priority=k % 2)

    pe = _dot(p_ref[...], wple_ref[...])
    wait_slot(slot)

    rw = rw_ref[...]
    cols = []
    for j in range(r):
        acc = jnp.zeros((tm, LANES), F32)
        for k in range(TOP_K):
            acc = acc + rw[:, k:k + 1] * buf[slot, pl.ds(k * tm * r + j, tm, stride=r), :]
        cols.append(acc)
    x2 = x1_ref[...] + jnp.concatenate(cols, axis=-1)
    gate = _sigmoid(_dot(_rms(x2, gple_ref[...]), wgate_ref[...]))
    y_ref[...] = x2 + pe * gate

    @pl.when(i == n_i - 1)
    def _():
        wait_slot(1 - slot)


def _combine(dest_flat, ys, x1, rw, p2d, g_ple, wple_bf, wgate_bf, rows_per_tok):
    n, d = x1.shape
    tm = _token_tile(n)
    n_i = n // tm
    ple = p2d.shape[1]
    row = lambda w: pl.BlockSpec((tm, w), lambda i: (i, 0))
    return pl.pallas_call(
        functools.partial(_combine_kernel, rows_per_tok=rows_per_tok),
        grid=(n_i,),
        in_specs=[pl.BlockSpec((tm * TOP_K,), lambda i: (i,), memory_space=pltpu.SMEM),
                  pl.BlockSpec((tm * TOP_K,), lambda i: (jnp.minimum(i + 1, n_i - 1),),
                               memory_space=pltpu.SMEM),
                  pl.BlockSpec(memory_space=pl.ANY), row(d), row(LANES), row(ple),
                  _full((1, d)), _full((ple, d)), _full((d, d))],
        out_specs=row(d),
        out_shape=jax.ShapeDtypeStruct((n, d), F32),
        scratch_shapes=[pltpu.VMEM((2, TOP_K * tm * rows_per_tok, LANES), F32),
                        pltpu.SemaphoreType.DMA((2,))],
        compiler_params=_params("arbitrary"),
        name="combine_ple",
    )(dest_flat, dest_flat, ys, x1, rw, p2d, g_ple.reshape(1, d), wple_bf, wgate_bf)


def kernel(x_prompt, x_sample, cache_k, cache_v, state_hgrn, page_table, p_prompt, p_sample,
           g_mix, w_in, lb_logits, g_hg_out, g_q, g_k, w_lift_hgrn, w_lift_moba, w_out, g_ffn,
           w_router, b_router, w_gate_up, b_gate_up, w_down, b_down, g_ple, w_ple, w_ple_gate):
    depth = g_mix.shape[0]
    assert depth == 1, "single-layer step"
    li = 0
    n_b, seq, d = x_prompt.shape
    n_db, t_new, _ = x_sample.shape
    ps = cache_k.shape[3]
    rows_per_tok = d // LANES
    assert rows_per_tok == SUBLANES and seq % ps == 0

    lb = jnp.cumsum(jax.nn.softmax(lb_logits.astype(F32), axis=0), axis=0)[li]
    w_in_bf = w_in[li].astype(BF16)
    wlh, wlm, wout = (w_lift_hgrn[li].astype(BF16), w_lift_moba[li].astype(BF16),
                      w_out[li].astype(BF16))
    wple_bf, wgate_bf = w_ple[li].astype(BF16), w_ple_gate[li].astype(BF16)

    xp = x_prompt.reshape(n_b * seq, d)
    xs_ = x_sample.reshape(n_db * t_new, d)

    zh_p, q_p, k_p, v_p, gates_p = _inproj(xp, g_mix[li], w_in_bf, lb, g_q[li], g_k[li], ps)
    bh_p, st_p = _hgrn(zh_p, None, g_hg_out[li], n_b, seq)
    bm_p = _moba_prompt(q_p, k_p, v_p, n_b, seq)

    zh_s, q_s, k_s, v_s, gates_s = _inproj(xs_, g_mix[li], w_in_bf, lb, g_q[li], g_k[li], t_new)
    bh_s, st_s = _hgrn(zh_s, state_hgrn[li], g_hg_out[li], n_db, t_new)
    sel, scores = _moba_select(q_s, cache_k, page_table, t_new)
    sel = sel[..., :MOBA_TOPK]
    ppb = MOBA_BLOCK // ps
    logical = (sel[..., None] * ppb + jnp.arange(ppb, dtype=jnp.int32)).reshape(n_db, -1)
    n_pg = page_table.shape[1]
    hit = logical[:, :, None] == jnp.arange(n_pg, dtype=jnp.int32)
    phys = jnp.sum(jnp.where(hit, page_table[:, None, :], 0), axis=-1).reshape(-1)
    bm_s = _moba_sample(q_s, k_s, v_s, scores, cache_v, phys, sel.reshape(-1), n_db, t_new)

    cnt0 = jnp.zeros((N_EXPERTS, LANES), F32)
    x1_p, h2_p, ri_p, rw_p, cnt_p = _merge(xp, bh_p, bm_p, gates_p, wlh, wlm, wout, g_ffn[li],
                                           w_router[li], b_router[li], cnt0)
    x1_s, h2_s, ri_s, rw_s, cnt_all = _merge(xs_, bh_s, bm_s, gates_s, wlh, wlm, wout, g_ffn[li],
                                             w_router[li], b_router[li], cnt_p)

    counts = cnt_all[:, 0].astype(jnp.int32)
    padded = (counts + MOE_TILE - 1) // MOE_TILE * MOE_TILE
    pad_end = jnp.cumsum(padded)
    pad_start = pad_end - padded
    n_assign = (n_b * seq + n_db * t_new) * TOP_K
    n_tiles = -(-(n_assign + N_EXPERTS * (MOE_TILE - 1)) // MOE_TILE)
    tile_lo = jnp.arange(n_tiles, dtype=jnp.int32) * MOE_TILE
    tile_e = jnp.minimum(jnp.sum(pad_end[None, :] <= tile_lo[:, None], axis=1),
                         N_EXPERTS - 1).astype(jnp.int32)
    n_used = (pad_end[-1:] // MOE_TILE).astype(jnp.int32)
    zlo = jnp.concatenate([(pad_start + counts) // ZERO_ROWS, pad_end[-1:] // ZERO_ROWS])
    zhi = jnp.concatenate([pad_end // ZERO_ROWS,
                           jnp.full((1,), n_tiles * MOE_TILE // ZERO_ROWS, jnp.int32)])

    def dest_of(ri):
        n = ri.shape[1]
        tm = _token_tile(n)
        e, rank = ri[:TOP_K], ri[TOP_K:]
        start = jnp.sum(jnp.where(e[:, :, None] == jnp.arange(N_EXPERTS, dtype=jnp.int32),
                                  pad_start, 0), axis=-1)
        return (start + rank).reshape(TOP_K, n // tm, tm).transpose(1, 0, 2).reshape(-1)

    dest_p, dest_s = dest_of(ri_p), dest_of(ri_s)
    xs = _dispatch(zlo.astype(jnp.int32), zhi.astype(jnp.int32), dest_p, dest_s, h2_p, h2_s,
                   n_tiles * MOE_TILE, rows_per_tok)
    ys = _moe(tile_e, n_used, xs, w_gate_up[li], b_gate_up[li], w_down[li], b_down[li],
              rows_per_tok)

    y_p = _combine(dest_p, ys, x1_p, rw_p, p_prompt[li].reshape(n_b * seq, -1), g_ple[li],
                   wple_bf, wgate_bf, rows_per_tok)
    y_s = _combine(dest_s, ys, x1_s, rw_s, p_sample[li].reshape(n_db * t_new, -1), g_ple[li],
                   wple_bf, wgate_bf, rows_per_tok)

    return (y_p.reshape(n_b, seq, d), y_s.reshape(n_db, t_new, d),
            k_p.reshape(1, n_b, seq // ps, N_HEADS, ps, HEAD_DIM),
            v_p.reshape(1, n_b, seq // ps, N_HEADS, ps, HEAD_DIM),
            k_s[None], v_s[None], st_p[None], st_s[None])
```

```python
import functools

import jax
import jax.numpy as jnp
from jax import lax
from jax.experimental import pallas as pl
from jax.experimental.pallas import tpu as pltpu

F32 = jnp.float32
BF16 = jnp.bfloat16
HIGHEST = lax.Precision.HIGHEST

EPS = 1e-6
N_HEADS = 4
HEAD_DIM = 128
HG_CHUNK = 64
MOBA_BLOCK = 256
MOBA_TOPK = 3
MOBA_QGROUP = 16
N_EXPERTS = 32
TOP_K = 4
SWIGLU_LIMIT = 7.0
SWIGLU_ALPHA = 1.702
NEG = -1e30

LANES = 128
SUBLANES = 8
TOKEN_TILE = 256
HGRN_TILE = 512
MERGE_TILE = 512
MOE_TILE = 512
VMEM_LIMIT = 56 * 1024 * 1024


def _params(*sem):
    return pltpu.CompilerParams(dimension_semantics=sem, vmem_limit_bytes=VMEM_LIMIT)


def _sigmoid(x):
    return 1.0 / (1.0 + jnp.exp(-x))


def _silu(x):
    return x * _sigmoid(x)


def _rms(x, g):
    return x * lax.rsqrt(jnp.mean(x * x, axis=-1, keepdims=True) + EPS) * g


def _dot(a, b):
    return jnp.dot(a.astype(BF16), b.astype(BF16), preferred_element_type=F32)


def _dot_nt(a, b):
    return lax.dot_general(a.astype(BF16), b.astype(BF16), (((1,), (1,)), ((), ())),
                           preferred_element_type=F32)


def _dot_tn(a, b):
    return lax.dot_general(a.astype(BF16), b.astype(BF16), (((0,), (0,)), ((), ())),
                           preferred_element_type=F32)


def _token_tile(n):
    return TOKEN_TILE if n % TOKEN_TILE == 0 else n


def _full(shape):
    return pl.BlockSpec(shape, lambda *_: (0,) * len(shape))


def _inproj_kernel(x_ref, gmix_ref, w_ref, lb_ref, gq_ref, gk_ref,
                   zh_ref, q_ref, k_ref, v_ref, gate_ref):
    x = x_ref[...]
    h = _rms(x, gmix_ref[...]).astype(BF16)
    hw = N_HEADS * HEAD_DIM
    d = x.shape[-1]
    tm = x.shape[0]

    def proj(c0, width):
        return jnp.dot(h, w_ref[:, c0:c0 + width], preferred_element_type=F32)

    lb = lb_ref[...]
    zh_ref[:, 0:hw] = _silu(proj(0, hw))
    fa = proj(hw, hw)
    zh_ref[:, hw:2 * hw] = jnp.log(lb + (1.0 - lb) * _sigmoid(fa))
    zh_ref[:, 2 * hw:3 * hw] = (1.0 - lb) * _sigmoid(-fa)
    zh_ref[:, 3 * hw:4 * hw] = proj(2 * hw, hw)
    zh_ref[:, 4 * hw:5 * hw] = _silu(proj(3 * hw, hw))

    qb = proj(4 * hw, hw)
    kb = proj(5 * hw, hw)
    vb = proj(6 * hw, hw)
    rows = k_ref.shape[2]
    for hd in range(N_HEADS):
        sl = slice(hd * HEAD_DIM, (hd + 1) * HEAD_DIM)
        q_ref[:, sl] = _rms(qb[:, sl], gq_ref[...])
        k_ref[:, hd, :, :] = _rms(kb[:, sl], gk_ref[...]).reshape(tm // rows, rows, HEAD_DIM)
        v_ref[:, hd, :, :] = vb[:, sl].reshape(tm // rows, rows, HEAD_DIM)

    gate_ref[:, 0:d] = _sigmoid(proj(7 * hw, d))
    gate_ref[:, d:2 * d] = _sigmoid(proj(7 * hw + d, d))


def _inproj(x2d, g_mix, w_in_bf, lb, g_q, g_k, rows_per_page):
    n, d = x2d.shape
    hw = N_HEADS * HEAD_DIM
    tm = _token_tile(n)
    r = rows_per_page
    n_in = w_in_bf.shape[1]
    out_shape = (
        jax.ShapeDtypeStruct((n, 5 * hw), F32),
        jax.ShapeDtypeStruct((n, hw), F32),
        jax.ShapeDtypeStruct((n // r, N_HEADS, r, HEAD_DIM), F32),
        jax.ShapeDtypeStruct((n // r, N_HEADS, r, HEAD_DIM), F32),
        jax.ShapeDtypeStruct((n, 2 * d), F32),
    )
    row = lambda w: pl.BlockSpec((tm, w), lambda i: (i, 0))
    page = pl.BlockSpec((tm // r, N_HEADS, r, HEAD_DIM), lambda i: (i, 0, 0, 0))
    return pl.pallas_call(
        _inproj_kernel,
        grid=(n // tm,),
        in_specs=[row(d), _full((1, d)),
                  pl.BlockSpec((d, n_in), lambda i: (0, 0), pipeline_mode=pl.Buffered(1)),
                  _full((1, hw)), _full((1, HEAD_DIM)), _full((1, HEAD_DIM))],
        out_specs=(row(5 * hw), row(hw), page, page, row(2 * d)),
        out_shape=out_shape,
        compiler_params=_params("parallel"),
        name="inproj",
    )(x2d, g_mix.reshape(1, d), w_in_bf, lb.reshape(1, hw), g_q.reshape(1, HEAD_DIM),
      g_k.reshape(1, HEAD_DIM))


def _hgrn_kernel(*refs, chunk, has_s0, exact_mxu):
    if has_s0:
        zh_ref, s0_ref, g_ref, o_ref, sout_ref, st_ref = refs
    else:
        zh_ref, g_ref, o_ref, sout_ref, st_ref = refs
        s0_ref = None
    t = pl.program_id(1)
    hw = N_HEADS * HEAD_DIM
    n_sub = sout_ref.shape[0]
    tb = zh_ref.shape[0] // n_sub

    @pl.when(t == 0)
    def _():
        for s in range(n_sub):
            for hd in range(N_HEADS):
                if has_s0:
                    st_ref[s * N_HEADS + hd] = s0_ref[s, hd].T
                else:
                    st_ref[s * N_HEADS + hd] = jnp.zeros((HEAD_DIM, HEAD_DIM), F32)

    if exact_mxu:
        mm_nt = lambda a, b: lax.dot_general(a, b, (((1,), (1,)), ((), ())), precision=HIGHEST,
                                             preferred_element_type=F32)
        mm = lambda a, b: jnp.dot(a, b, precision=HIGHEST, preferred_element_type=F32)
        mm_tn = lambda a, b: lax.dot_general(a, b, (((0,), (0,)), ((), ())), precision=HIGHEST,
                                             preferred_element_type=F32)
    else:
        mm_nt, mm, mm_tn = _dot_nt, _dot, _dot_tn

    ri = lax.broadcasted_iota(jnp.int32, (chunk, chunk), 0)
    ci = lax.broadcasted_iota(jnp.int32, (chunk, chunk), 1)
    causal = ci <= ri
    tril = jnp.where(causal, 1.0, 0.0).astype(F32)

    for s in range(n_sub):
        sts = [st_ref[s * N_HEADS + hd] for hd in range(N_HEADS)]
        for c in range(tb // chunk):
            rows = slice(s * tb + c * chunk, s * tb + (c + 1) * chunk)
            k_all = zh_ref[rows, 2 * hw:3 * hw]
            b_all = jnp.dot(tril, zh_ref[rows, hw:2 * hw], precision=HIGHEST,
                            preferred_element_type=F32)
            b_end = b_all[chunk - 1:chunk, :]
            qd_all = zh_ref[rows, 0:hw] * jnp.exp(b_all)
            kd_all = k_all * jnp.exp(-b_all)
            kk_all = k_all * jnp.exp(b_end - b_all)
            decay = jnp.exp(b_end)
            for hd in range(N_HEADS):
                sl = slice(hd * HEAD_DIM, (hd + 1) * HEAD_DIM)
                v = zh_ref[rows, 3 * hw + hd * HEAD_DIM:3 * hw + (hd + 1) * HEAD_DIM]
                og = zh_ref[rows, 4 * hw + hd * HEAD_DIM:4 * hw + (hd + 1) * HEAD_DIM]
                qd = qd_all[:, sl]
                att = jnp.where(causal, mm_nt(qd, kd_all[:, sl]), 0.0)
                o = mm(att, v) + mm_nt(qd, sts[hd])
                sts[hd] = decay[:, sl] * sts[hd] + mm_tn(v, kk_all[:, sl])
                o_ref[rows, sl] = _rms(o, g_ref[...]) * og
        for hd in range(N_HEADS):
            st_ref[s * N_HEADS + hd] = sts[hd]

    @pl.when(t == pl.num_programs(1) - 1)
    def _():
        for s in range(n_sub):
            for hd in range(N_HEADS):
                sout_ref[s, hd] = st_ref[s * N_HEADS + hd].T


def _hgrn(zh, s0, g_hg_out, n_seq, seq_len):
    hw = N_HEADS * HEAD_DIM
    chunk = min(HG_CHUNK, seq_len)
    assert seq_len % chunk == 0
    tb = min(HGRN_TILE, seq_len)
    assert seq_len % tb == 0 and tb % chunk == 0
    n_t = seq_len // tb
    n_sub = 1
    if n_t == 1:
        n_sub = max(d for d in range(1, HGRN_TILE // (8 * tb) + 1) if n_seq % d == 0)
    n_steps, tb = n_seq // n_sub, tb * n_sub
    has_s0 = s0 is not None
    state_spec = pl.BlockSpec((n_sub, N_HEADS, HEAD_DIM, HEAD_DIM), lambda b, t: (b, 0, 0, 0))
    in_specs = [pl.BlockSpec((tb, 5 * hw), lambda b, t: (b * n_t + t, 0))]
    args = [zh]
    if has_s0:
        in_specs.append(state_spec)
        args.append(s0)
    in_specs.append(_full((1, HEAD_DIM)))
    args.append(g_hg_out.reshape(1, HEAD_DIM))
    return pl.pallas_call(
        functools.partial(_hgrn_kernel, chunk=chunk, has_s0=has_s0, exact_mxu=False),
        grid=(n_steps, n_t),
        in_specs=in_specs,
        out_specs=(pl.BlockSpec((tb, hw), lambda b, t: (b * n_t + t, 0)), state_spec),
        out_shape=(jax.ShapeDtypeStruct((n_seq * seq_len, hw), F32),
                   jax.ShapeDtypeStruct((n_seq, N_HEADS, HEAD_DIM, HEAD_DIM), F32)),
        scratch_shapes=[pltpu.VMEM((n_sub * N_HEADS, HEAD_DIM, HEAD_DIM), F32)],
        compiler_params=_params("parallel", "arbitrary"),
        name="hgrn_prompt" if not has_s0 else "hgrn_sample",
    )(*args)


def _topk_rank(gsc, n_valid):
    rows, n_col = gsc.shape
    blk = lax.broadcasted_iota(jnp.int32, (rows, n_col), 1)
    rank = jnp.zeros((rows, n_col), jnp.int32)
    for m in range(n_col):
        sm = gsc[:, m:m + 1]
        beats = jnp.where(sm > gsc, 1, jnp.where(sm == gsc, jnp.where(blk > m, 1, 0), 0))
        rank = rank + jnp.where(m < n_valid, beats, 0)
    return blk, rank


def _moba_prompt_kernel(q_ref, k_ref, v_ref, o_ref, kmean_ref, kbf_ref, vt_ref, osc_ref, perm_ref):
    n = pl.program_id(1)
    n_b = kmean_ref.shape[1]
    ps = k_ref.shape[2]
    ppb = MOBA_BLOCK // ps
    scale = HEAD_DIM ** -0.5
    heads = range(N_HEADS)

    @pl.when(n == 0)
    def _():
        for hd in heads:
            for m in range(n_b):
                kb = k_ref[m * ppb:(m + 1) * ppb, hd].reshape(MOBA_BLOCK, HEAD_DIM)
                kmean_ref[hd, m:m + 1, :] = jnp.mean(kb, axis=0, keepdims=True)
                kbf_ref[hd, m] = kb.astype(BF16)
                for pp in range(ppb):
                    vt_ref[hd, m, :, pp * ps:(pp + 1) * ps] = v_ref[m * ppb + pp, hd].T.astype(BF16)

    def block(m, hd):
        return kbf_ref[hd, m], vt_ref[hd, m]

    key_i = lax.broadcasted_iota(jnp.int32, (MOBA_BLOCK, MOBA_BLOCK), 0)
    qry_i = lax.broadcasted_iota(jnp.int32, (MOBA_BLOCK, MOBA_BLOCK), 1)
    blk = lax.broadcasted_iota(jnp.int32, (n_b, MOBA_BLOCK), 0)
    qbs, sels, init = [], [], []
    for hd in heads:
        q = q_ref[:, hd * HEAD_DIM:(hd + 1) * HEAD_DIM]
        gsc = lax.dot_general(kmean_ref[hd], q, (((1,), (1,)), ((), ())), precision=HIGHEST,
                              preferred_element_type=F32)
        rank = jnp.zeros((n_b, MOBA_BLOCK), jnp.int32)
        for m in range(n_b):
            sm = gsc[m:m + 1, :]
            beats = jnp.where(sm > gsc, 1, jnp.where(sm == gsc, jnp.where(blk > m, 1, 0), 0))
            rank = rank + jnp.where(m < n, beats, 0)
        sels.append(jnp.where((blk < n) & (rank < MOBA_TOPK), 1.0, 0.0))
        qbs.append((q * scale).astype(BF16))
    raw = [_dot_nt(kbf_ref[hd, n], qbs[hd]) for hd in heads]
    mid = []
    for hd in heads:
        s = jnp.where(key_i <= qry_i, raw[hd], NEG)
        m0 = jnp.max(s, axis=0, keepdims=True)
        p = jnp.exp(s - m0)
        mid.append((m0, jnp.sum(p, axis=0, keepdims=True), p.astype(BF16)))
    for hd in heads:
        m0, l0, p = mid[hd]
        init.append((m0, l0, _dot(vt_ref[hd, n], p)))

    def body(m, carry):
        raw = [_dot_nt(kbf_ref[hd, m], qbs[hd]) for hd in heads]
        mid = []
        for hd in heads:
            m_i, l_i, _ = carry[hd]
            picked = jnp.sum(jnp.where(blk == m, sels[hd], 0.0), axis=0, keepdims=True)
            s = jnp.where(picked > 0.0, raw[hd], NEG)
            m_new = jnp.maximum(m_i, jnp.max(s, axis=0, keepdims=True))
            alpha = jnp.exp(m_i - m_new)
            p = jnp.exp(s - m_new)
            mid.append((m_new, alpha, alpha * l_i + jnp.sum(p, axis=0, keepdims=True),
                        p.astype(BF16)))
        out = []
        for hd in heads:
            m_new, alpha, l_new, p = mid[hd]
            out.append((m_new, l_new, alpha * carry[hd][2] + _dot(vt_ref[hd, m], p)))
        return tuple(out)

    final = lax.fori_loop(0, n, body, tuple(init))
    n_grp = MOBA_BLOCK // MOBA_QGROUP
    for hd in heads:
        _, l_f, acc_f = final[hd]
        osc_ref[hd] = (acc_f / l_f).T
        for qg in range(MOBA_QGROUP):
            perm_ref[qg, pl.ds(hd, n_grp, stride=N_HEADS), :] = (
                osc_ref[hd, pl.ds(qg, n_grp, stride=MOBA_QGROUP), :])
    for qg in range(MOBA_QGROUP):
        qh, ql = divmod(qg, N_HEADS)
        o_ref[ql, :, qh * HEAD_DIM:(qh + 1) * HEAD_DIM] = perm_ref[qg]


def _moba_prompt(q, k_pages, v_pages, n_seq, seq_len):
    assert seq_len % MOBA_BLOCK == 0 and MOBA_QGROUP == N_HEADS * N_HEADS
    n_b = seq_len // MOBA_BLOCK
    ps = k_pages.shape[2]
    assert MOBA_BLOCK % ps == 0
    pps = seq_len // ps
    hw = N_HEADS * HEAD_DIM
    n_grp = MOBA_BLOCK // MOBA_QGROUP
    qspec = pl.BlockSpec((MOBA_BLOCK, hw), lambda b, n: (b * n_b + n, 0))
    kvspec = pl.BlockSpec((pps, N_HEADS, ps, HEAD_DIM), lambda b, n: (b, 0, 0, 0))
    out = pl.pallas_call(
        _moba_prompt_kernel,
        grid=(n_seq, n_b),
        in_specs=[qspec, kvspec, kvspec],
        out_specs=pl.BlockSpec((None, N_HEADS, n_grp * N_HEADS, hw), lambda b, n: (b, 0, n, 0)),
        out_shape=jax.ShapeDtypeStruct((n_seq, N_HEADS, seq_len // N_HEADS, hw), F32),
        scratch_shapes=[pltpu.VMEM((N_HEADS, n_b, HEAD_DIM), F32),
                        pltpu.VMEM((N_HEADS, n_b, MOBA_BLOCK, HEAD_DIM), BF16),
                        pltpu.VMEM((N_HEADS, n_b, HEAD_DIM, MOBA_BLOCK), BF16),
                        pltpu.VMEM((N_HEADS, MOBA_BLOCK, HEAD_DIM), F32),
                        pltpu.VMEM((MOBA_QGROUP, n_grp * N_HEADS, HEAD_DIM), F32)],
        compiler_params=_params("parallel", "arbitrary"),
        name="moba_prompt",
    )(q, k_pages, v_pages)
    return out.reshape(n_seq * seq_len, hw)


PAGES_PER_STEP = 16
PREFETCH_DEPTH = 3


def _moba_select_kernel(pt_ref, q_ref, ck_hbm, sel_ref, sc_ref, kbuf, kmean_ref, qbd_ref, sem,
                        *, n_j, n_g):
    g = pl.program_id(0)
    j = g % n_j
    ps = kbuf.shape[3]
    ppb = MOBA_BLOCK // ps
    bps = PAGES_PER_STEP // ppb
    n_blk = kmean_ref.shape[1]
    t_new = q_ref.shape[0]
    scale = HEAD_DIM ** -0.5

    def copies(step, slot):
        return [pltpu.make_async_copy(ck_hbm.at[0, pt_ref[step * PAGES_PER_STEP + i]],
                                      kbuf.at[slot, i], sem.at[slot])
                for i in range(PAGES_PER_STEP)]

    @pl.when(g == 0)
    def _():
        for ahead in range(min(PREFETCH_DEPTH - 1, n_g)):
            for i, c in enumerate(copies(ahead, ahead)):
                c.start(priority=i % 2)

    nxt = g + PREFETCH_DEPTH - 1

    @pl.when(nxt < n_g)
    def _():
        for i, c in enumerate(copies(nxt, nxt % PREFETCH_DEPTH)):
            c.start(priority=i % 2)

    slot = g % PREFETCH_DEPTH
    for c in copies(g, slot):
        c.wait()
    page_refs = [kbuf.at[slot, i] for i in range(PAGES_PER_STEP)]

    @pl.when(j == 0)
    def _():
        zero = jnp.zeros((t_new, HEAD_DIM), F32)
        for p in range(N_HEADS // 2):
            q0 = q_ref[:, (2 * p) * HEAD_DIM:(2 * p + 1) * HEAD_DIM] * scale
            q1 = q_ref[:, (2 * p + 1) * HEAD_DIM:(2 * p + 2) * HEAD_DIM] * scale
            qbd_ref[p] = jnp.concatenate([jnp.concatenate([q0, zero], axis=-1),
                                          jnp.concatenate([zero, q1], axis=-1)], axis=0).astype(BF16)

    for i in range(PAGES_PER_STEP):
        for p in range(N_HEADS // 2):
            kcat = jnp.concatenate([page_refs[i][2 * p], page_refs[i][2 * p + 1]], axis=-1)
            sc_ref[p, i // ppb, :, (i % ppb) * ps:(i % ppb + 1) * ps] = _dot_nt(qbd_ref[p], kcat)

    for i in range(bps):
        for hd in range(N_HEADS):
            tot = jnp.zeros((1, HEAD_DIM), F32)
            for pp in range(ppb):
                tot = tot + jnp.sum(page_refs[i * ppb + pp][hd], axis=0, keepdims=True)
            kmean_ref[hd, pl.ds(j * bps + i, 1), :] = tot / float(MOBA_BLOCK)

    @pl.when(j == n_j - 1)
    def _():
        for hd in range(N_HEADS):
            q = q_ref[:, hd * HEAD_DIM:(hd + 1) * HEAD_DIM]
            gsc = lax.dot_general(q, kmean_ref[hd], (((1,), (1,)), ((), ())), precision=HIGHEST,
                                  preferred_element_type=F32)
            blk, rank = _topk_rank(gsc, n_blk)
            lane = lax.broadcasted_iota(jnp.int32, (t_new, LANES), 1)
            out = jnp.zeros((t_new, LANES), jnp.int32)
            for r in range(MOBA_TOPK):
                idx = jnp.sum(jnp.where(rank == r, blk.astype(F32), 0.0), axis=-1, keepdims=True)
                out = jnp.where(lane == r, idx.astype(jnp.int32), out)
            sel_ref[0, hd] = out


def _moba_select(q_s, cache_k, page_table, t_new):
    n_seq, n_pg = page_table.shape
    _, _, _, ps, _ = cache_k.shape
    assert (n_pg * ps) % MOBA_BLOCK == 0, "cached rows must end on a MoBA block boundary"
    assert n_pg % PAGES_PER_STEP == 0 and MOBA_BLOCK % ps == 0
    n_blk = n_pg * ps // MOBA_BLOCK
    assert n_blk >= MOBA_TOPK
    hw = N_HEADS * HEAD_DIM
    bps = PAGES_PER_STEP * ps // MOBA_BLOCK

    n_j = n_pg // PAGES_PER_STEP
    grid_spec = pltpu.PrefetchScalarGridSpec(
        num_scalar_prefetch=1,
        grid=(n_seq * n_j,),
        in_specs=[pl.BlockSpec((t_new, hw), lambda g, pt: (g // n_j, 0)),
                  pl.BlockSpec(memory_space=pl.ANY)],
        out_specs=(
            pl.BlockSpec((1, N_HEADS, t_new, LANES), lambda g, pt: (g // n_j, 0, 0, 0)),
            pl.BlockSpec((None, N_HEADS // 2, bps, 2 * t_new, MOBA_BLOCK),
                         lambda g, pt: (g // n_j, 0, g % n_j, 0, 0))),
        scratch_shapes=[pltpu.VMEM((PREFETCH_DEPTH, PAGES_PER_STEP, N_HEADS, ps, HEAD_DIM), F32),
                        pltpu.VMEM((N_HEADS, n_blk, HEAD_DIM), F32),
                        pltpu.VMEM((N_HEADS // 2, 2 * t_new, 2 * HEAD_DIM), BF16),
                        pltpu.SemaphoreType.DMA((PREFETCH_DEPTH,))],
    )
    return pl.pallas_call(
        functools.partial(_moba_select_kernel, n_j=n_j, n_g=n_seq * n_j),
        grid_spec=grid_spec,
        out_shape=(jax.ShapeDtypeStruct((n_seq, N_HEADS, t_new, LANES), jnp.int32),
                   jax.ShapeDtypeStruct((n_seq, N_HEADS // 2, n_blk, 2 * t_new, MOBA_BLOCK), F32)),
        compiler_params=_params("arbitrary"),
        name="moba_select",
    )(page_table.reshape(-1), q_s, cache_k)


def _moba_sample_kernel(pages_ref, blk_ref, q_ref, kn_ref, vn_ref, sc_ref, cv_hbm, o_ref,
                        vbuf, ssel_ref, sem, *, n_fetch, n_g):
    g = pl.program_id(0)
    ps = vbuf.shape[2]
    t_new = q_ref.shape[0]
    per_q = n_fetch // t_new * ps
    scale = HEAD_DIM ** -0.5

    def copies(step, slot):
        hd = step % N_HEADS
        return [pltpu.make_async_copy(cv_hbm.at[0, pages_ref[step * n_fetch + i], hd],
                                      vbuf.at[slot, i], sem.at[slot]) for i in range(n_fetch)]

    @pl.when(g == 0)
    def _():
        for ahead in range(min(PREFETCH_DEPTH - 1, n_g)):
            for i, c in enumerate(copies(ahead, ahead)):
                c.start(priority=i % 2)

    nxt = g + PREFETCH_DEPTH - 1

    @pl.when(nxt < n_g)
    def _():
        for i, c in enumerate(copies(nxt, nxt % PREFETCH_DEPTH)):
            c.start(priority=i % 2)

    for t in range(t_new):
        for r in range(MOBA_TOPK):
            blk = blk_ref[(g * t_new + t) * MOBA_TOPK + r]
            ssel_ref[t:t + 1, r * MOBA_BLOCK:(r + 1) * MOBA_BLOCK] = sc_ref[blk, t:t + 1, :]
    s = ssel_ref[...]

    pad = jnp.zeros((2 * SUBLANES - t_new, HEAD_DIM), F32)
    qp = jnp.concatenate([q_ref[...] * scale, pad], axis=0)
    kn = jnp.concatenate([kn_ref[...], pad], axis=0)
    vn = jnp.concatenate([vn_ref[...], pad], axis=0)
    s_new = _dot_nt(qp, kn)[:t_new]
    rn = lax.broadcasted_iota(jnp.int32, s_new.shape, 0)
    cn = lax.broadcasted_iota(jnp.int32, s_new.shape, 1)
    s_new = jnp.where((cn <= rn) & (cn < t_new), s_new, NEG)
    m = jnp.maximum(jnp.max(s, axis=-1, keepdims=True), jnp.max(s_new, axis=-1, keepdims=True))
    p = jnp.exp(s - m)
    p_new = jnp.exp(s_new - m)
    den = jnp.sum(p, axis=-1, keepdims=True) + jnp.sum(p_new, axis=-1, keepdims=True)

    row = lax.broadcasted_iota(jnp.int32, p.shape, 0)
    zero_rows = jnp.zeros((2 * SUBLANES - t_new, t_new * per_q), F32)
    p_wide = jnp.concatenate([jnp.where(row == t, p, 0.0) for t in range(t_new)], axis=-1)
    p_wide = jnp.concatenate([p_wide, zero_rows], axis=0)
    p_new = jnp.concatenate([p_new, jnp.zeros((2 * SUBLANES - t_new, p_new.shape[1]), F32)], axis=0)

    slot = g % PREFETCH_DEPTH
    for c in copies(g, slot):
        c.wait()
    vc = vbuf[slot].reshape(n_fetch * ps, HEAD_DIM)
    o = _dot(p_wide, vc) + _dot(p_new, vn)
    o_ref[...] = o[:t_new] / den


def _moba_sample(q_s, k_new, v_new, scores, cache_v, phys_pages, sel_blocks, n_seq, t_new):
    ps = cache_v.shape[3]
    n_fetch = phys_pages.shape[0] // (n_seq * N_HEADS)
    n_blk = scores.shape[2]
    head = lambda g, *_: (g // N_HEADS, g % N_HEADS)
    grid_spec = pltpu.PrefetchScalarGridSpec(
        num_scalar_prefetch=2,
        grid=(n_seq * N_HEADS,),
        in_specs=[
            pl.BlockSpec((t_new, HEAD_DIM), head),
            pl.BlockSpec((None, None, t_new, HEAD_DIM), lambda g, *_: head(g) + (0, 0)),
            pl.BlockSpec((None, None, t_new, HEAD_DIM), lambda g, *_: head(g) + (0, 0)),
            pl.BlockSpec((None, None, n_blk, t_new, MOBA_BLOCK),
                         lambda g, *_: (g // N_HEADS, (g % N_HEADS) // 2, 0, g % 2, 0)),
            pl.BlockSpec(memory_space=pl.ANY),
        ],
        out_specs=pl.BlockSpec((t_new, HEAD_DIM), head),
        scratch_shapes=[
            pltpu.VMEM((PREFETCH_DEPTH, n_fetch, ps, HEAD_DIM), F32),
            pltpu.VMEM((t_new, MOBA_TOPK * MOBA_BLOCK), F32),
            pltpu.SemaphoreType.DMA((PREFETCH_DEPTH,)),
        ],
    )
    return pl.pallas_call(
        functools.partial(_moba_sample_kernel, n_fetch=n_fetch, n_g=n_seq * N_HEADS),
        grid_spec=grid_spec,
        out_shape=jax.ShapeDtypeStruct(q_s.shape, F32),
        compiler_params=_params("arbitrary"),
        name="moba_sample",
    )(phys_pages, sel_blocks, q_s, k_new, v_new, scores, cache_v)


def _merge_kernel(x_ref, bh_ref, bm_ref, gate_ref, wlh_ref, wlm_ref, wout_ref, gffn_ref,
                  wr_ref, br_ref, cnt0_ref, x1_ref, h2_ref, ri_ref, rw_ref, cnt_ref, run_ref):
    i = pl.program_id(0)
    tm, d = x_ref.shape

    @pl.when(i == 0)
    def _():
        run_ref[...] = cnt0_ref[...]

    mixed = (gate_ref[:, 0:d] * _dot(bh_ref[...], wlh_ref[...])
             + gate_ref[:, d:2 * d] * _dot(bm_ref[...], wlm_ref[...]))
    x1 = x_ref[...] + _dot(mixed, wout_ref[...])
    x1_ref[...] = x1
    h2 = _rms(x1, gffn_ref[...])
    n_chunk = d // LANES
    for j in range(n_chunk):
        h2_ref[pl.ds(j, tm, stride=n_chunk), :] = h2[:, j * LANES:(j + 1) * LANES]

    logits = lax.dot_general(wr_ref[...], h2, (((1,), (1,)), ((), ())), precision=HIGHEST,
                             preferred_element_type=F32) + br_ref[...]
    sub_e = lax.broadcasted_iota(jnp.int32, (N_EXPERTS, tm), 0).astype(F32)
    lg = logits
    vals, ids, hots = [], [], []
    for _ in range(TOP_K):
        mx = jnp.max(lg, axis=0, keepdims=True)
        idx = jnp.min(jnp.where(lg == mx, sub_e, float(N_EXPERTS)), axis=0, keepdims=True)
        hot = sub_e == idx
        vals.append(mx)
        ids.append(idx)
        hots.append(hot)
        lg = jnp.where(hot, -jnp.inf, lg)
    exps = [jnp.exp(v - vals[0]) for v in vals]
    den = exps[0] + exps[1] + exps[2] + exps[3]

    chosen = jnp.zeros((N_EXPERTS, tm), F32)
    for hot in hots:
        chosen = chosen + jnp.where(hot, 1.0, 0.0)
    r_i = lax.broadcasted_iota(jnp.int32, (tm, tm), 0)
    c_i = lax.broadcasted_iota(jnp.int32, (tm, tm), 1)
    earlier = jnp.where(r_i < c_i, 1.0, 0.0)
    base = _dot(chosen, earlier) + run_ref[:, 0:1]
    run_ref[...] = run_ref[...] + jnp.sum(chosen, axis=1, keepdims=True)
    cnt_ref[...] = run_ref[...]

    sub_i = lax.broadcasted_iota(jnp.int32, (2 * TOP_K, tm), 0)
    sub_w = lax.broadcasted_iota(jnp.int32, (LANES, tm), 0)
    out_i = jnp.zeros((2 * TOP_K, tm), jnp.int32)
    out_w = jnp.zeros((LANES, tm), F32)
    for k in range(TOP_K):
        r_k = jnp.sum(jnp.where(hots[k], base, 0.0), axis=0, keepdims=True)
        out_i = jnp.where(sub_i == k, ids[k].astype(jnp.int32), out_i)
        out_i = jnp.where(sub_i == TOP_K + k, r_k.astype(jnp.int32), out_i)
        out_w = jnp.where(sub_w == k, exps[k] / den, out_w)
    ri_ref[...] = out_i
    rw_ref[...] = out_w.T


def _merge(x2d, bh, bm, gates, wlh, wlm, wout, g_ffn, w_router, b_router, cnt0):
    n, d = x2d.shape
    hw = N_HEADS * HEAD_DIM
    tm = MERGE_TILE if n % MERGE_TILE == 0 else n
    n_chunk = d // LANES
    row = lambda w: pl.BlockSpec((tm, w), lambda i: (i, 0))
    return pl.pallas_call(
        _merge_kernel,
        grid=(n // tm,),
        in_specs=[row(d), row(hw), row(hw), row(2 * d), _full((hw, d)), _full((hw, d)),
                  _full((d, d)), _full((1, d)), _full((N_EXPERTS, d)), _full((N_EXPERTS, 1)),
                  _full((N_EXPERTS, LANES))],
        out_specs=(row(d), pl.BlockSpec((tm * n_chunk, LANES), lambda i: (i, 0)),
                   pl.BlockSpec((2 * TOP_K, tm), lambda i: (0, i)), row(LANES),
                   _full((N_EXPERTS, LANES))),
        out_shape=(jax.ShapeDtypeStruct((n, d), F32),
                   jax.ShapeDtypeStruct((n * n_chunk, LANES), F32),
                   jax.ShapeDtypeStruct((2 * TOP_K, n), jnp.int32),
                   jax.ShapeDtypeStruct((n, LANES), F32),
                   jax.ShapeDtypeStruct((N_EXPERTS, LANES), F32)),
        scratch_shapes=[pltpu.VMEM((N_EXPERTS, LANES), F32)],
        compiler_params=_params("arbitrary"),
        name="merge_router",
    )(x2d, bh, bm, gates, wlh, wlm, wout, g_ffn.reshape(1, d), w_router.T,
      b_router.reshape(N_EXPERTS, 1), cnt0)


ZERO_ROWS = 64


def _dispatch_kernel(zlo_ref, zhi_ref, dest_p_ref, dest_s_ref, h2p_ref, h2s_ref, xs_hbm,
                     zbuf, sem, *, n_p, rows_per_tok):
    i = pl.program_id(0)
    r = rows_per_tok
    zrows = ZERO_ROWS * r

    def zero_copy(s):
        return pltpu.make_async_copy(zbuf, xs_hbm.at[pl.ds(pl.multiple_of(s * zrows, zrows), zrows)],
                                     sem.at[0])

    @pl.when(i == 0)
    def _():
        zbuf[...] = jnp.zeros(zbuf.shape, F32)

        def each_range(fn):
            def outer(e, carry):
                def inner(s, c):
                    fn(s)
                    return c
                return lax.fori_loop(zlo_ref[e], zhi_ref[e], inner, carry)
            lax.fori_loop(0, zlo_ref.shape[0], outer, 0)

        each_range(lambda s: zero_copy(s).start())
        each_range(lambda s: zero_copy(s).wait())

    def scatter(h2_ref, dest_ref):
        tm = h2_ref.shape[0] // r

        unroll = 8 if tm % 8 == 0 else 1

        def body(tu, carry):
            for u in range(unroll):
                t = tu * unroll + u
                src = h2_ref.at[pl.ds(pl.multiple_of(t * r, r), r)]
                for k in range(TOP_K):
                    dst = xs_hbm.at[pl.ds(pl.multiple_of(dest_ref[k * tm + t] * r, r), r)]
                    pltpu.make_async_copy(src, dst, sem.at[1]).start(priority=k % 2)
            return carry

        lax.fori_loop(0, tm // unroll, body, 0)
        n_rows = tm * TOP_K * r
        pltpu.make_async_copy(xs_hbm.at[pl.ds(0, n_rows)], xs_hbm.at[pl.ds(0, n_rows)],
                              sem.at[1]).wait()

    @pl.when(i < n_p)
    def _():
        scatter(h2p_ref, dest_p_ref)

    @pl.when(i >= n_p)
    def _():
        scatter(h2s_ref, dest_s_ref)


def _dispatch(zlo, zhi, dest_p, dest_s, h2_p, h2_s, n_slots, rows_per_tok):
    r = rows_per_tok
    tm_p = _token_tile(h2_p.shape[0] // r)
    tm_s = _token_tile(h2_s.shape[0] // r)
    n_p = h2_p.shape[0] // (tm_p * r)
    n_s = h2_s.shape[0] // (tm_s * r)
    first = lambda i, *_: (jnp.minimum(i, n_p - 1),)
    second = lambda i, *_: (jnp.maximum(i - n_p, 0),)
    grid_spec = pltpu.PrefetchScalarGridSpec(
        num_scalar_prefetch=2,
        grid=(n_p + n_s,),
        in_specs=[
            pl.BlockSpec((tm_p * TOP_K,), first, memory_space=pltpu.SMEM),
            pl.BlockSpec((tm_s * TOP_K,), second, memory_space=pltpu.SMEM),
            pl.BlockSpec((tm_p * r, LANES), lambda i, *_: first(i) + (0,)),
            pl.BlockSpec((tm_s * r, LANES), lambda i, *_: second(i) + (0,)),
        ],
        out_specs=pl.BlockSpec(memory_space=pl.ANY),
        scratch_shapes=[pltpu.VMEM((ZERO_ROWS * r, LANES), F32), pltpu.SemaphoreType.DMA((2,))],
    )
    return pl.pallas_call(
        functools.partial(_dispatch_kernel, n_p=n_p, rows_per_tok=r),
        grid_spec=grid_spec,
        out_shape=jax.ShapeDtypeStruct((n_slots * r, LANES), F32),
        compiler_params=_params("arbitrary"),
        name="moe_dispatch",
    )(zlo, zhi, dest_p, dest_s, h2_p, h2_s)


def _moe_kernel(te_ref, nu_ref, xs_ref, wgu_f32, bgu_ref, wdn_f32, bdn_ref, ys_ref,
                wgu_ref, wdn_ref):
    i = pl.program_id(0)
    n_chunk = xs_ref.shape[0] // MOE_TILE
    d_ff = wdn_ref.shape[0]
    used = i < nu_ref[0]
    new_expert = (i == 0) | (te_ref[i] != te_ref[jnp.maximum(i - 1, 0)])

    @pl.when(used & new_expert)
    def _():
        rows = 128
        for c in range(wgu_ref.shape[0] // rows):
            wgu_ref[c * rows:(c + 1) * rows, :] = wgu_f32[c * rows:(c + 1) * rows, :].astype(BF16)
        for c in range(wdn_ref.shape[0] // rows):
            wdn_ref[c * rows:(c + 1) * rows, :] = wdn_f32[c * rows:(c + 1) * rows, :].astype(BF16)

    @pl.when(used)
    def _():
        x = jnp.concatenate(
            [xs_ref[pl.ds(j, MOE_TILE, stride=n_chunk), :] for j in range(n_chunk)], axis=-1)
        gu = _dot(x, wgu_ref[...]) + bgu_ref[...]
        g_ = jnp.minimum(gu[:, :d_ff], SWIGLU_LIMIT)
        u_ = jnp.clip(gu[:, d_ff:], -SWIGLU_LIMIT, SWIGLU_LIMIT)
        act = (u_ + 1.0) * g_ * _sigmoid(SWIGLU_ALPHA * g_)
        y = _dot(act, wdn_ref[...]) + bdn_ref[...]
        for j in range(n_chunk):
            ys_ref[pl.ds(j, MOE_TILE, stride=n_chunk), :] = y[:, j * LANES:(j + 1) * LANES]

    @pl.when(i >= nu_ref[0])
    def _():
        ys_ref[...] = jnp.zeros(ys_ref.shape, F32)


def _moe(tile_e, n_used, xs, w_gu, b_gu, w_dn, b_dn, rows_per_tok):
    n_tiles = xs.shape[0] // (MOE_TILE * rows_per_tok)
    d, two_ff = w_gu.shape[1:]
    d_ff = w_dn.shape[1]
    last = lambda i, nu: jnp.minimum(i, nu[0] - 1)
    grid_spec = pltpu.PrefetchScalarGridSpec(
        num_scalar_prefetch=2,
        grid=(n_tiles,),
        in_specs=[
            pl.BlockSpec((MOE_TILE * rows_per_tok, LANES), lambda i, te, nu: (last(i, nu), 0)),
            pl.BlockSpec((None, d, two_ff), lambda i, te, nu: (te[last(i, nu)], 0, 0)),
            pl.BlockSpec((None, 1, two_ff), lambda i, te, nu: (te[last(i, nu)], 0, 0)),
            pl.BlockSpec((None, d_ff, d), lambda i, te, nu: (te[last(i, nu)], 0, 0)),
            pl.BlockSpec((None, 1, d), lambda i, te, nu: (te[last(i, nu)], 0, 0)),
        ],
        out_specs=pl.BlockSpec((MOE_TILE * rows_per_tok, LANES), lambda i, te, nu: (i, 0)),
        scratch_shapes=[pltpu.VMEM((d, two_ff), BF16), pltpu.VMEM((d_ff, d), BF16)],
    )
    return pl.pallas_call(
        _moe_kernel,
        grid_spec=grid_spec,
        out_shape=jax.ShapeDtypeStruct(xs.shape, F32),
        compiler_params=_params("arbitrary"),
        name="moe_ffn",
    )(tile_e, n_used, xs, w_gu, b_gu.reshape(N_EXPERTS, 1, two_ff), w_dn,
      b_dn.reshape(N_EXPERTS, 1, d))


def _combine_kernel(dest_ref, dest_next_ref, ys_hbm, x1_ref, rw_ref, p_ref, gple_ref, wple_ref,
                    wgate_ref, y_ref, buf, sem, *, rows_per_tok):
    i = pl.program_id(0)
    n_i = pl.num_programs(0)
    tm, d = x1_ref.shape
    r = rows_per_tok

    def row_copy(d_ref, slot, t, k):
        src = ys_hbm.at[pl.ds(pl.multiple_of(d_ref[k * tm + t] * r, r), r)]
        dst = buf.at[slot, pl.ds(pl.multiple_of((k * tm + t) * r, r), r)]
        return pltpu.make_async_copy(src, dst, sem.at[slot])

    def wait_slot(slot):
        pltpu.make_async_copy(buf.at[slot], buf.at[slot], sem.at[slot]).wait()

    slot = i % 2

    @pl.when(i == 0)
    def _():
        def body(t, carry):
            for k in range(TOP_K):
                row_copy(dest_ref, 0, t, k).start(priority=k % 2)
            return carry
        lax.fori_loop(0, tm, body, 0)

    for t in range(tm):
        for k in range(TOP_K):
            row_copy(dest_next_ref, 1 - slot, t, k).start(priority=k % 2)

    pe = _dot(p_ref[...], wple_ref[...])
    wait_slot(slot)

    rw = rw_ref[...]
    cols = []
    for j in range(r):
        acc = jnp.zeros((tm, LANES), F32)
        for k in range(TOP_K):
            acc = acc + rw[:, k:k + 1] * buf[slot, pl.ds(k * tm * r + j, tm, stride=r), :]
        cols.append(acc)
    x2 = x1_ref[...] + jnp.concatenate(cols, axis=-1)
    gate = _sigmoid(_dot(_rms(x2, gple_ref[...]), wgate_ref[...]))
    y_ref[...] = x2 + pe * gate

    @pl.when(i == n_i - 1)
    def _():
        wait_slot(1 - slot)


def _combine(dest_flat, ys, x1, rw, p2d, g_ple, wple_bf, wgate_bf, rows_per_tok):
    n, d = x1.shape
    tm = _token_tile(n)
    n_i = n // tm
    ple = p2d.shape[1]
    row = lambda w: pl.BlockSpec((tm, w), lambda i: (i, 0))
    return pl.pallas_call(
        functools.partial(_combine_kernel, rows_per_tok=rows_per_tok),
        grid=(n_i,),
        in_specs=[pl.BlockSpec((tm * TOP_K,), lambda i: (i,), memory_space=pltpu.SMEM),
                  pl.BlockSpec((tm * TOP_K,), lambda i: (jnp.minimum(i + 1, n_i - 1),),
                               memory_space=pltpu.SMEM),
                  pl.BlockSpec(memory_space=pl.ANY), row(d), row(LANES), row(ple),
                  _full((1, d)), _full((ple, d)), _full((d, d))],
        out_specs=row(d),
        out_shape=jax.ShapeDtypeStruct((n, d), F32),
        scratch_shapes=[pltpu.VMEM((2, TOP_K * tm * rows_per_tok, LANES), F32),
                        pltpu.SemaphoreType.DMA((2,))],
        compiler_params=_params("arbitrary"),
        name="combine_ple",
    )(dest_flat, dest_flat, ys, x1, rw, p2d, g_ple.reshape(1, d), wple_bf, wgate_bf)


def kernel(x_prompt, x_sample, cache_k, cache_v, state_hgrn, page_table, p_prompt, p_sample,
           g_mix, w_in, lb_logits, g_hg_out, g_q, g_k, w_lift_hgrn, w_lift_moba, w_out, g_ffn,
           w_router, b_router, w_gate_up, b_gate_up, w_down, b_down, g_ple, w_ple, w_ple_gate):
    depth = g_mix.shape[0]
    assert depth == 1, "single-layer step"
    li = 0
    n_b, seq, d = x_prompt.shape
    n_db, t_new, _ = x_sample.shape
    ps = cache_k.shape[3]
    rows_per_tok = d // LANES
    assert rows_per_tok == SUBLANES and seq % ps == 0

    lb = jnp.cumsum(jax.nn.softmax(lb_logits.astype(F32), axis=0), axis=0)[li]
    w_in_bf = w_in[li].astype(BF16)
    wlh, wlm, wout = (w_lift_hgrn[li].astype(BF16), w_lift_moba[li].astype(BF16),
                      w_out[li].astype(BF16))
    wple_bf, wgate_bf = w_ple[li].astype(BF16), w_ple_gate[li].astype(BF16)

    xp = x_prompt.reshape(n_b * seq, d)
    xs_ = x_sample.reshape(n_db * t_new, d)

    zh_p, q_p, k_p, v_p, gates_p = _inproj(xp, g_mix[li], w_in_bf, lb, g_q[li], g_k[li], ps)
    bh_p, st_p = _hgrn(zh_p, None, g_hg_out[li], n_b, seq)
    bm_p = _moba_prompt(q_p, k_p, v_p, n_b, seq)

    zh_s, q_s, k_s, v_s, gates_s = _inproj(xs_, g_mix[li], w_in_bf, lb, g_q[li], g_k[li], t_new)
    bh_s, st_s = _hgrn(zh_s, state_hgrn[li], g_hg_out[li], n_db, t_new)
    sel, scores = _moba_select(q_s, cache_k, page_table, t_new)
    sel = sel[..., :MOBA_TOPK]
    ppb = MOBA_BLOCK // ps
    logical = (sel[..., None] * ppb + jnp.arange(ppb, dtype=jnp.int32)).reshape(n_db, -1)
    n_pg = page_table.shape[1]
    hit = logical[:, :, None] == jnp.arange(n_pg, dtype=jnp.int32)
    phys = jnp.sum(jnp.where(hit, page_table[:, None, :], 0), axis=-1).reshape(-1)
    bm_s = _moba_sample(q_s, k_s, v_s, scores, cache_v, phys, sel.reshape(-1), n_db, t_new)

    cnt0 = jnp.zeros((N_EXPERTS, LANES), F32)
    x1_p, h2_p, ri_p, rw_p, cnt_p = _merge(xp, bh_p, bm_p, gates_p, wlh, wlm, wout, g_ffn[li],
                                           w_router[li], b_router[li], cnt0)
    x1_s, h2_s, ri_s, rw_s, cnt_all = _merge(xs_, bh_s, bm_s, gates_s, wlh, wlm, wout, g_ffn[li],
                                             w_router[li], b_router[li], cnt_p)

    counts = cnt_all[:, 0].astype(jnp.int32)
    padded = (counts + MOE_TILE - 1) // MOE_TILE * MOE_TILE
    pad_end = jnp.cumsum(padded)
    pad_start = pad_end - padded
    n_assign = (n_b * seq + n_db * t_new) * TOP_K
    n_tiles = -(-(n_assign + N_EXPERTS * (MOE_TILE - 1)) // MOE_TILE)
    tile_lo = jnp.arange(n_tiles, dtype=jnp.int32) * MOE_TILE
    tile_e = jnp.minimum(jnp.sum(pad_end[None, :] <= tile_lo[:, None], axis=1),
                         N_EXPERTS - 1).astype(jnp.int32)
    n_used = (pad_end[-1:] // MOE_TILE).astype(jnp.int32)
    zlo = jnp.concatenate([(pad_start + counts) // ZERO_ROWS, pad_end[-1:] // ZERO_ROWS])
    zhi = jnp.concatenate([pad_end // ZERO_ROWS,
                           jnp.full((1,), n_tiles * MOE_TILE // ZERO_ROWS, jnp.int32)])

    def dest_of(ri):
        n = ri.shape[1]
        tm = _token_tile(n)
        e, rank = ri[:TOP_K], ri[TOP_K:]
        start = jnp.sum(jnp.where(e[:, :, None] == jnp.arange(N_EXPERTS, dtype=jnp.int32),
                                  pad_start, 0), axis=-1)
        return (start + rank).reshape(TOP_K, n // tm, tm).transpose(1, 0, 2).reshape(-1)

    dest_p, dest_s = dest_of(ri_p), dest_of(ri_s)
    xs = _dispatch(zlo.astype(jnp.int32), zhi.astype(jnp.int32), dest_p, dest_s, h2_p, h2_s,
                   n_tiles * MOE_TILE, rows_per_tok)
    ys = _moe(tile_e, n_used, xs, w_gate_up[li], b_gate_up[li], w_down[li], b_down[li],
              rows_per_tok)

    y_p = _combine(dest_p, ys, x1_p, rw_p, p_prompt[li].reshape(n_b * seq, -1), g_ple[li],
                   wple_bf, wgate_bf, rows_per_tok)
    y_s = _combine(dest_s, ys, x1_s, rw_s, p_sample[li].reshape(n_db * t_new, -1), g_ple[li],
                   wple_bf, wgate_bf, rows_per_tok)

    return (y_p.reshape(n_b, seq, d), y_s.reshape(n_db, t_new, d),
            k_p.reshape(1, n_b, seq // ps, N_HEADS, ps, HEAD_DIM),
            v_p.reshape(1, n_b, seq // ps, N_HEADS, ps, HEAD_DIM),
            k_s[None], v_s[None], st_p[None], st_s[None])
```

```python
import functools

import jax
import jax.numpy as jnp
from jax import lax
from jax.experimental import pallas as pl
from jax.experimental.pallas import tpu as pltpu

F32 = jnp.float32
BF16 = jnp.bfloat16
HIGHEST = lax.Precision.HIGHEST

EPS = 1e-6
N_HEADS = 4
HEAD_DIM = 128
HG_CHUNK = 64
MOBA_BLOCK = 256
MOBA_TOPK = 3
MOBA_QGROUP = 16
N_EXPERTS = 32
TOP_K = 4
SWIGLU_LIMIT = 7.0
SWIGLU_ALPHA = 1.702
NEG = -1e30

LANES = 128
SUBLANES = 8
TOKEN_TILE = 256
INPROJ_TILE = 256
HGRN_TILE = 512
MERGE_TILE = 512
MOE_TILE = 512
VMEM_LIMIT = 56 * 1024 * 1024


def _params(*sem):
    return pltpu.CompilerParams(dimension_semantics=sem, vmem_limit_bytes=VMEM_LIMIT)


def _sigmoid(x):
    return 1.0 / (1.0 + jnp.exp(-x))


def _silu(x):
    return x * _sigmoid(x)


def _rms(x, g):
    return x * lax.rsqrt(jnp.mean(x * x, axis=-1, keepdims=True) + EPS) * g


def _dot(a, b):
    return jnp.dot(a.astype(BF16), b.astype(BF16), preferred_element_type=F32)


def _dot_nt(a, b):
    return lax.dot_general(a.astype(BF16), b.astype(BF16), (((1,), (1,)), ((), ())),
                           preferred_element_type=F32)


def _dot_tn(a, b):
    return lax.dot_general(a.astype(BF16), b.astype(BF16), (((0,), (0,)), ((), ())),
                           preferred_element_type=F32)


def _token_tile(n):
    return TOKEN_TILE if n % TOKEN_TILE == 0 else n


def _full(shape):
    return pl.BlockSpec(shape, lambda *_: (0,) * len(shape))


def _inproj_kernel(x_ref, gmix_ref, w_ref, lb_ref, gq_ref, gk_ref,
                   zh_ref, q_ref, k_ref, v_ref, gate_ref):
    x = x_ref[...]
    h = _rms(x, gmix_ref[...]).astype(BF16)
    hw = N_HEADS * HEAD_DIM
    d = x.shape[-1]
    tm = x.shape[0]

    def proj(c0, width):
        return jnp.dot(h, w_ref[:, c0:c0 + width], preferred_element_type=F32)

    lb = lb_ref[...]
    zh_ref[:, 0:hw] = _silu(proj(0, hw))
    fa = proj(hw, hw)
    zh_ref[:, hw:2 * hw] = jnp.log(lb + (1.0 - lb) * _sigmoid(fa))
    zh_ref[:, 2 * hw:3 * hw] = (1.0 - lb) * _sigmoid(-fa)
    zh_ref[:, 3 * hw:4 * hw] = proj(2 * hw, hw)
    zh_ref[:, 4 * hw:5 * hw] = _silu(proj(3 * hw, hw))

    qb = proj(4 * hw, hw)
    kb = proj(5 * hw, hw)
    vb = proj(6 * hw, hw)
    rows = k_ref.shape[2]
    for hd in range(N_HEADS):
        sl = slice(hd * HEAD_DIM, (hd + 1) * HEAD_DIM)
        q_ref[:, sl] = _rms(qb[:, sl], gq_ref[...])
        k_ref[:, hd, :, :] = _rms(kb[:, sl], gk_ref[...]).reshape(tm // rows, rows, HEAD_DIM)
        v_ref[:, hd, :, :] = vb[:, sl].reshape(tm // rows, rows, HEAD_DIM)

    gate_ref[:, 0:d] = _sigmoid(proj(7 * hw, d))
    gate_ref[:, d:2 * d] = _sigmoid(proj(7 * hw + d, d))


def _inproj(x2d, g_mix, w_in_bf, lb, g_q, g_k, rows_per_page):
    n, d = x2d.shape
    hw = N_HEADS * HEAD_DIM
    tm = INPROJ_TILE if n % INPROJ_TILE == 0 else _token_tile(n)
    r = rows_per_page
    n_in = w_in_bf.shape[1]
    out_shape = (
        jax.ShapeDtypeStruct((n, 5 * hw), F32),
        jax.ShapeDtypeStruct((n, hw), F32),
        jax.ShapeDtypeStruct((n // r, N_HEADS, r, HEAD_DIM), F32),
        jax.ShapeDtypeStruct((n // r, N_HEADS, r, HEAD_DIM), F32),
        jax.ShapeDtypeStruct((n, 2 * d), F32),
    )
    row = lambda w: pl.BlockSpec((tm, w), lambda i: (i, 0))
    page = pl.BlockSpec((tm // r, N_HEADS, r, HEAD_DIM), lambda i: (i, 0, 0, 0))
    return pl.pallas_call(
        _inproj_kernel,
        grid=(n // tm,),
        in_specs=[row(d), _full((1, d)),
                  pl.BlockSpec((d, n_in), lambda i: (0, 0), pipeline_mode=pl.Buffered(1)),
                  _full((1, hw)), _full((1, HEAD_DIM)), _full((1, HEAD_DIM))],
        out_specs=(row(5 * hw), row(hw), page, page, row(2 * d)),
        out_shape=out_shape,
        compiler_params=_params("parallel"),
        name="inproj",
    )(x2d, g_mix.reshape(1, d), w_in_bf, lb.reshape(1, hw), g_q.reshape(1, HEAD_DIM),
      g_k.reshape(1, HEAD_DIM))


def _hgrn_kernel(*refs, chunk, has_s0, exact_mxu):
    if has_s0:
        zh_ref, s0_ref, g_ref, o_ref, sout_ref, st_ref = refs
    else:
        zh_ref, g_ref, o_ref, sout_ref, st_ref = refs
        s0_ref = None
    t = pl.program_id(1)
    hw = N_HEADS * HEAD_DIM
    n_sub = sout_ref.shape[0]
    tb = zh_ref.shape[0] // n_sub

    @pl.when(t == 0)
    def _():
        for s in range(n_sub):
            for hd in range(N_HEADS):
                if has_s0:
                    st_ref[s * N_HEADS + hd] = s0_ref[s, hd].T
                else:
                    st_ref[s * N_HEADS + hd] = jnp.zeros((HEAD_DIM, HEAD_DIM), F32)

    if exact_mxu:
        mm_nt = lambda a, b: lax.dot_general(a, b, (((1,), (1,)), ((), ())), precision=HIGHEST,
                                             preferred_element_type=F32)
        mm = lambda a, b: jnp.dot(a, b, precision=HIGHEST, preferred_element_type=F32)
        mm_tn = lambda a, b: lax.dot_general(a, b, (((0,), (0,)), ((), ())), precision=HIGHEST,
                                             preferred_element_type=F32)
    else:
        mm_nt, mm, mm_tn = _dot_nt, _dot, _dot_tn

    ri = lax.broadcasted_iota(jnp.int32, (chunk, chunk), 0)
    ci = lax.broadcasted_iota(jnp.int32, (chunk, chunk), 1)
    causal = ci <= ri
    tril = jnp.where(causal, 1.0, 0.0).astype(F32)

    sls = [slice(hd * HEAD_DIM, (hd + 1) * HEAD_DIM) for hd in range(N_HEADS)]
    n_c = tb // chunk
    pre = []
    for sc in range(n_sub * n_c):
        rows = slice(sc * chunk, (sc + 1) * chunk)
        k_all = zh_ref[rows, 2 * hw:3 * hw]
        b_all = jnp.dot(tril, zh_ref[rows, hw:2 * hw], precision=HIGHEST,
                        preferred_element_type=F32)
        b_end = b_all[chunk - 1:chunk, :]
        pre.append((rows, zh_ref[rows, 0:hw] * jnp.exp(b_all), k_all * jnp.exp(-b_all),
                    k_all * jnp.exp(b_end - b_all), jnp.exp(b_end)))
    intra, upd = [], []
    for rows, qd_all, kd_all, kk_all, _ in pre:
        vs = [zh_ref[rows, 3 * hw + hd * HEAD_DIM:3 * hw + (hd + 1) * HEAD_DIM]
              for hd in range(N_HEADS)]
        att = [mm_nt(qd_all[:, sl], kd_all[:, sl]) for sl in sls]
        upd.append([mm_tn(vs[hd], kk_all[:, sl]) for hd, sl in enumerate(sls)])
        intra.append([mm(jnp.where(causal, att[hd], 0.0), vs[hd]) for hd in range(N_HEADS)])
    for s in range(n_sub):
        sts = [st_ref[s * N_HEADS + hd] for hd in range(N_HEADS)]
        for c in range(s * n_c, (s + 1) * n_c):
            rows, qd_all, _, _, decay = pre[c]
            inter = [mm_nt(qd_all[:, sl], sts[hd]) for hd, sl in enumerate(sls)]
            for hd, sl in enumerate(sls):
                sts[hd] = decay[:, sl] * sts[hd] + upd[c][hd]
                og = zh_ref[rows, 4 * hw + hd * HEAD_DIM:4 * hw + (hd + 1) * HEAD_DIM]
                o_ref[rows, sl] = _rms(intra[c][hd] + inter[hd], g_ref[...]) * og
        for hd in range(N_HEADS):
            st_ref[s * N_HEADS + hd] = sts[hd]

    @pl.when(t == pl.num_programs(1) - 1)
    def _():
        for s in range(n_sub):
            for hd in range(N_HEADS):
                sout_ref[s, hd] = st_ref[s * N_HEADS + hd].T


def _hgrn(zh, s0, g_hg_out, n_seq, seq_len):
    hw = N_HEADS * HEAD_DIM
    chunk = min(HG_CHUNK, seq_len)
    assert seq_len % chunk == 0
    tb = min(HGRN_TILE, seq_len)
    assert seq_len % tb == 0 and tb % chunk == 0
    n_t = seq_len // tb
    n_sub = 1
    if n_t == 1:
        n_sub = max(d for d in range(1, HGRN_TILE // (8 * tb) + 1) if n_seq % d == 0)
    n_steps, tb = n_seq // n_sub, tb * n_sub
    has_s0 = s0 is not None
    state_spec = pl.BlockSpec((n_sub, N_HEADS, HEAD_DIM, HEAD_DIM), lambda b, t: (b, 0, 0, 0))
    in_specs = [pl.BlockSpec((tb, 5 * hw), lambda b, t: (b * n_t + t, 0))]
    args = [zh]
    if has_s0:
        in_specs.append(state_spec)
        args.append(s0)
    in_specs.append(_full((1, HEAD_DIM)))
    args.append(g_hg_out.reshape(1, HEAD_DIM))
    return pl.pallas_call(
        functools.partial(_hgrn_kernel, chunk=chunk, has_s0=has_s0, exact_mxu=False),
        grid=(n_steps, n_t),
        in_specs=in_specs,
        out_specs=(pl.BlockSpec((tb, hw), lambda b, t: (b * n_t + t, 0)), state_spec),
        out_shape=(jax.ShapeDtypeStruct((n_seq * seq_len, hw), F32),
                   jax.ShapeDtypeStruct((n_seq, N_HEADS, HEAD_DIM, HEAD_DIM), F32)),
        scratch_shapes=[pltpu.VMEM((n_sub * N_HEADS, HEAD_DIM, HEAD_DIM), F32)],
        compiler_params=_params("parallel", "arbitrary"),
        name="hgrn_prompt" if not has_s0 else "hgrn_sample",
    )(*args)


def _topk_rank(gsc, n_valid):
    rows, n_col = gsc.shape
    blk = lax.broadcasted_iota(jnp.int32, (rows, n_col), 1)
    rank = jnp.zeros((rows, n_col), jnp.int32)
    for m in range(n_col):
        sm = gsc[:, m:m + 1]
        beats = jnp.where(sm > gsc, 1, jnp.where(sm == gsc, jnp.where(blk > m, 1, 0), 0))
        rank = rank + jnp.where(m < n_valid, beats, 0)
    return blk, rank


def _moba_prompt_kernel(q_ref, k_ref, v_ref, o_ref, kmean_ref, kbf_ref, vt_ref, osc_ref, perm_ref):
    n = pl.program_id(1)
    n_b = kmean_ref.shape[1]
    ps = k_ref.shape[2]
    ppb = MOBA_BLOCK // ps
    scale = HEAD_DIM ** -0.5
    heads = range(N_HEADS)

    @pl.when(n == 0)
    def _():
        for hd in heads:
            for m in range(n_b):
                kb = k_ref[m * ppb:(m + 1) * ppb, hd].reshape(MOBA_BLOCK, HEAD_DIM)
                kmean_ref[hd, m:m + 1, :] = jnp.mean(kb, axis=0, keepdims=True)
                kbf_ref[hd, m] = kb.astype(BF16)
                for pp in range(ppb):
                    vt_ref[hd, m, :, pp * ps:(pp + 1) * ps] = v_ref[m * ppb + pp, hd].T.astype(BF16)

    def block(m, hd):
        return kbf_ref[hd, m], vt_ref[hd, m]

    key_i = lax.broadcasted_iota(jnp.int32, (MOBA_BLOCK, MOBA_BLOCK), 0)
    qry_i = lax.broadcasted_iota(jnp.int32, (MOBA_BLOCK, MOBA_BLOCK), 1)
    blk = lax.broadcasted_iota(jnp.int32, (n_b, MOBA_BLOCK), 0)
    qbs, sels, init = [], [], []
    for hd in heads:
        q = q_ref[:, hd * HEAD_DIM:(hd + 1) * HEAD_DIM]
        gsc = lax.dot_general(kmean_ref[hd], q, (((1,), (1,)), ((), ())), precision=HIGHEST,
                              preferred_element_type=F32)
        rank = jnp.zeros((n_b, MOBA_BLOCK), jnp.int32)
        for m in range(n_b):
            sm = gsc[m:m + 1, :]
            beats = jnp.where(sm > gsc, 1, jnp.where(sm == gsc, jnp.where(blk > m, 1, 0), 0))
            rank = rank + jnp.where(m < n, beats, 0)
        sels.append(jnp.where((blk < n) & (rank < MOBA_TOPK), 1.0, 0.0))
        qbs.append((q * scale).astype(BF16))
    raw = [_dot_nt(kbf_ref[hd, n], qbs[hd]) for hd in heads]
    mid = []
    for hd in heads:
        s = jnp.where(key_i <= qry_i, raw[hd], NEG)
        m0 = jnp.max(s, axis=0, keepdims=True)
        p = jnp.exp(s - m0)
        mid.append((m0, jnp.sum(p, axis=0, keepdims=True), p.astype(BF16)))
    for hd in heads:
        m0, l0, p = mid[hd]
        init.append((m0, l0, _dot(vt_ref[hd, n], p)))

    def body(m, carry):
        raw = [_dot_nt(kbf_ref[hd, m], qbs[hd]) for hd in heads]
        mid = []
        for hd in heads:
            m_i, l_i, _ = carry[hd]
            picked = jnp.sum(jnp.where(blk == m, sels[hd], 0.0), axis=0, keepdims=True)
            s = jnp.where(picked > 0.0, raw[hd], NEG)
            m_new = jnp.maximum(m_i, jnp.max(s, axis=0, keepdims=True))
            alpha = jnp.exp(m_i - m_new)
            p = jnp.exp(s - m_new)
            mid.append((m_new, alpha, alpha * l_i + jnp.sum(p, axis=0, keepdims=True),
                        p.astype(BF16)))
        out = []
        for hd in heads:
            m_new, alpha, l_new, p = mid[hd]
            out.append((m_new, l_new, alpha * carry[hd][2] + _dot(vt_ref[hd, m], p)))
        return tuple(out)

    final = lax.fori_loop(0, n, body, tuple(init))
    n_grp = MOBA_BLOCK // MOBA_QGROUP
    for hd in heads:
        _, l_f, acc_f = final[hd]
        osc_ref[hd] = (acc_f / l_f).T
        for qg in range(MOBA_QGROUP):
            perm_ref[qg, pl.ds(hd, n_grp, stride=N_HEADS), :] = (
                osc_ref[hd, pl.ds(qg, n_grp, stride=MOBA_QGROUP), :])
    for qg in range(MOBA_QGROUP):
        qh, ql = divmod(qg, N_HEADS)
        o_ref[ql, :, qh * HEAD_DIM:(qh + 1) * HEAD_DIM] = perm_ref[qg]


def _moba_prompt(q, k_pages, v_pages, n_seq, seq_len):
    assert seq_len % MOBA_BLOCK == 0 and MOBA_QGROUP == N_HEADS * N_HEADS
    n_b = seq_len // MOBA_BLOCK
    ps = k_pages.shape[2]
    assert MOBA_BLOCK % ps == 0
    pps = seq_len // ps
    hw = N_HEADS * HEAD_DIM
    n_grp = MOBA_BLOCK // MOBA_QGROUP
    qspec = pl.BlockSpec((MOBA_BLOCK, hw), lambda b, n: (b * n_b + n, 0))
    kvspec = pl.BlockSpec((pps, N_HEADS, ps, HEAD_DIM), lambda b, n: (b, 0, 0, 0))
    out = pl.pallas_call(
        _moba_prompt_kernel,
        grid=(n_seq, n_b),
        in_specs=[qspec, kvspec, kvspec],
        out_specs=pl.BlockSpec((None, N_HEADS, n_grp * N_HEADS, hw), lambda b, n: (b, 0, n, 0)),
        out_shape=jax.ShapeDtypeStruct((n_seq, N_HEADS, seq_len // N_HEADS, hw), F32),
        scratch_shapes=[pltpu.VMEM((N_HEADS, n_b, HEAD_DIM), F32),
                        pltpu.VMEM((N_HEADS, n_b, MOBA_BLOCK, HEAD_DIM), BF16),
                        pltpu.VMEM((N_HEADS, n_b, HEAD_DIM, MOBA_BLOCK), BF16),
                        pltpu.VMEM((N_HEADS, MOBA_BLOCK, HEAD_DIM), F32),
                        pltpu.VMEM((MOBA_QGROUP, n_grp * N_HEADS, HEAD_DIM), F32)],
        compiler_params=_params("parallel", "arbitrary"),
        name="moba_prompt",
    )(q, k_pages, v_pages)
    return out.reshape(n_seq * seq_len, hw)


PAGES_PER_STEP = 16
PREFETCH_DEPTH = 3


def _moba_select_kernel(pt_ref, q_ref, ck_hbm, sel_ref, sc_ref, kbuf, kmean_ref, qbd_ref, sem,
                        *, n_j, n_g):
    g = pl.program_id(0)
    j = g % n_j
    ps = kbuf.shape[3]
    ppb = MOBA_BLOCK // ps
    bps = PAGES_PER_STEP // ppb
    n_blk = kmean_ref.shape[1]
    t_new = q_ref.shape[0]
    scale = HEAD_DIM ** -0.5

    def copies(step, slot):
        return [pltpu.make_async_copy(ck_hbm.at[0, pt_ref[step * PAGES_PER_STEP + i]],
                                      kbuf.at[slot, i], sem.at[slot])
                for i in range(PAGES_PER_STEP)]

    @pl.when(g == 0)
    def _():
        for ahead in range(min(PREFETCH_DEPTH - 1, n_g)):
            for i, c in enumerate(copies(ahead, ahead)):
                c.start(priority=i % 2)

    nxt = g + PREFETCH_DEPTH - 1

    @pl.when(nxt < n_g)
    def _():
        for i, c in enumerate(copies(nxt, nxt % PREFETCH_DEPTH)):
            c.start(priority=i % 2)

    slot = g % PREFETCH_DEPTH
    for c in copies(g, slot):
        c.wait()
    page_refs = [kbuf.at[slot, i] for i in range(PAGES_PER_STEP)]

    @pl.when(j == 0)
    def _():
        zero = jnp.zeros((t_new, HEAD_DIM), F32)
        for p in range(N_HEADS // 2):
            q0 = q_ref[:, (2 * p) * HEAD_DIM:(2 * p + 1) * HEAD_DIM] * scale
            q1 = q_ref[:, (2 * p + 1) * HEAD_DIM:(2 * p + 2) * HEAD_DIM] * scale
            qbd_ref[p] = jnp.concatenate([jnp.concatenate([q0, zero], axis=-1),
                                          jnp.concatenate([zero, q1], axis=-1)], axis=0).astype(BF16)

    for i in range(PAGES_PER_STEP):
        for p in range(N_HEADS // 2):
            kcat = jnp.concatenate([page_refs[i][2 * p], page_refs[i][2 * p + 1]], axis=-1)
            sc_ref[p, i // ppb, :, (i % ppb) * ps:(i % ppb + 1) * ps] = _dot_nt(qbd_ref[p], kcat)

    for i in range(bps):
        for hd in range(N_HEADS):
            tot = jnp.zeros((1, HEAD_DIM), F32)
            for pp in range(ppb):
                tot = tot + jnp.sum(page_refs[i * ppb + pp][hd], axis=0, keepdims=True)
            kmean_ref[hd, pl.ds(j * bps + i, 1), :] = tot / float(MOBA_BLOCK)

    @pl.when(j == n_j - 1)
    def _():
        for hd in range(N_HEADS):
            q = q_ref[:, hd * HEAD_DIM:(hd + 1) * HEAD_DIM]
            gsc = lax.dot_general(q, kmean_ref[hd], (((1,), (1,)), ((), ())), precision=HIGHEST,
                                  preferred_element_type=F32)
            blk, rank = _topk_rank(gsc, n_blk)
            lane = lax.broadcasted_iota(jnp.int32, (t_new, LANES), 1)
            out = jnp.zeros((t_new, LANES), jnp.int32)
            for r in range(MOBA_TOPK):
                idx = jnp.sum(jnp.where(rank == r, blk.astype(F32), 0.0), axis=-1, keepdims=True)
                out = jnp.where(lane == r, idx.astype(jnp.int32), out)
            sel_ref[0, hd] = out


def _moba_select(q_s, cache_k, page_table, t_new):
    n_seq, n_pg = page_table.shape
    _, _, _, ps, _ = cache_k.shape
    assert (n_pg * ps) % MOBA_BLOCK == 0, "cached rows must end on a MoBA block boundary"
    assert n_pg % PAGES_PER_STEP == 0 and MOBA_BLOCK % ps == 0
    n_blk = n_pg * ps // MOBA_BLOCK
    assert n_blk >= MOBA_TOPK
    hw = N_HEADS * HEAD_DIM
    bps = PAGES_PER_STEP * ps // MOBA_BLOCK

    n_j = n_pg // PAGES_PER_STEP
    grid_spec = pltpu.PrefetchScalarGridSpec(
        num_scalar_prefetch=1,
        grid=(n_seq * n_j,),
        in_specs=[pl.BlockSpec((t_new, hw), lambda g, pt: (g // n_j, 0)),
                  pl.BlockSpec(memory_space=pl.ANY)],
        out_specs=(
            pl.BlockSpec((1, N_HEADS, t_new, LANES), lambda g, pt: (g // n_j, 0, 0, 0)),
            pl.BlockSpec((None, N_HEADS // 2, bps, 2 * t_new, MOBA_BLOCK),
                         lambda g, pt: (g // n_j, 0, g % n_j, 0, 0))),
        scratch_shapes=[pltpu.VMEM((PREFETCH_DEPTH, PAGES_PER_STEP, N_HEADS, ps, HEAD_DIM), F32),
                        pltpu.VMEM((N_HEADS, n_blk, HEAD_DIM), F32),
                        pltpu.VMEM((N_HEADS // 2, 2 * t_new, 2 * HEAD_DIM), BF16),
                        pltpu.SemaphoreType.DMA((PREFETCH_DEPTH,))],
    )
    return pl.pallas_call(
        functools.partial(_moba_select_kernel, n_j=n_j, n_g=n_seq * n_j),
        grid_spec=grid_spec,
        out_shape=(jax.ShapeDtypeStruct((n_seq, N_HEADS, t_new, LANES), jnp.int32),
                   jax.ShapeDtypeStruct((n_seq, N_HEADS // 2, n_blk, 2 * t_new, MOBA_BLOCK), F32)),
        compiler_params=_params("arbitrary"),
        name="moba_select",
    )(page_table.reshape(-1), q_s, cache_k)


def _moba_sample_kernel(pages_ref, blk_ref, q_ref, kn_ref, vn_ref, sc_ref, cv_hbm, o_ref,
                        vbuf, ssel_ref, sem, *, n_fetch, n_g):
    g = pl.program_id(0)
    ps = vbuf.shape[2]
    t_new = q_ref.shape[0]
    per_q = n_fetch // t_new * ps
    scale = HEAD_DIM ** -0.5

    def copies(step, slot):
        hd = step % N_HEADS
        return [pltpu.make_async_copy(cv_hbm.at[0, pages_ref[step * n_fetch + i], hd],
                                      vbuf.at[slot, i], sem.at[slot]) for i in range(n_fetch)]

    @pl.when(g == 0)
    def _():
        for ahead in range(min(PREFETCH_DEPTH - 1, n_g)):
            for i, c in enumerate(copies(ahead, ahead)):
                c.start(priority=i % 2)

    nxt = g + PREFETCH_DEPTH - 1

    @pl.when(nxt < n_g)
    def _():
        for i, c in enumerate(copies(nxt, nxt % PREFETCH_DEPTH)):
            c.start(priority=i % 2)

    for t in range(t_new):
        for r in range(MOBA_TOPK):
            blk = blk_ref[(g * t_new + t) * MOBA_TOPK + r]
            ssel_ref[t:t + 1, r * MOBA_BLOCK:(r + 1) * MOBA_BLOCK] = sc_ref[blk, t:t + 1, :]
    s = ssel_ref[...]

    pad = jnp.zeros((2 * SUBLANES - t_new, HEAD_DIM), F32)
    qp = jnp.concatenate([q_ref[...] * scale, pad], axis=0)
    kn = jnp.concatenate([kn_ref[...], pad], axis=0)
    vn = jnp.concatenate([vn_ref[...], pad], axis=0)
    s_new = _dot_nt(qp, kn)[:t_new]
    rn = lax.broadcasted_iota(jnp.int32, s_new.shape, 0)
    cn = lax.broadcasted_iota(jnp.int32, s_new.shape, 1)
    s_new = jnp.where((cn <= rn) & (cn < t_new), s_new, NEG)
    m = jnp.maximum(jnp.max(s, axis=-1, keepdims=True), jnp.max(s_new, axis=-1, keepdims=True))
    p = jnp.exp(s - m)
    p_new = jnp.exp(s_new - m)
    den = jnp.sum(p, axis=-1, keepdims=True) + jnp.sum(p_new, axis=-1, keepdims=True)

    row = lax.broadcasted_iota(jnp.int32, p.shape, 0)
    zero_rows = jnp.zeros((2 * SUBLANES - t_new, t_new * per_q), F32)
    p_wide = jnp.concatenate([jnp.where(row == t, p, 0.0) for t in range(t_new)], axis=-1)
    p_wide = jnp.concatenate([p_wide, zero_rows], axis=0)
    p_new = jnp.concatenate([p_new, jnp.zeros((2 * SUBLANES - t_new, p_new.shape[1]), F32)], axis=0)

    slot = g % PREFETCH_DEPTH
    for c in copies(g, slot):
        c.wait()
    vc = vbuf[slot].reshape(n_fetch * ps, HEAD_DIM)
    o = _dot(p_wide, vc) + _dot(p_new, vn)
    o_ref[...] = o[:t_new] / den


def _moba_sample(q_s, k_new, v_new, scores, cache_v, phys_pages, sel_blocks, n_seq, t_new):
    ps = cache_v.shape[3]
    n_fetch = phys_pages.shape[0] // (n_seq * N_HEADS)
    n_blk = scores.shape[2]
    head = lambda g, *_: (g // N_HEADS, g % N_HEADS)
    grid_spec = pltpu.PrefetchScalarGridSpec(
        num_scalar_prefetch=2,
        grid=(n_seq * N_HEADS,),
        in_specs=[
            pl.BlockSpec((t_new, HEAD_DIM), head),
            pl.BlockSpec((None, None, t_new, HEAD_DIM), lambda g, *_: head(g) + (0, 0)),
            pl.BlockSpec((None, None, t_new, HEAD_DIM), lambda g, *_: head(g) + (0, 0)),
            pl.BlockSpec((None, None, n_blk, t_new, MOBA_BLOCK),
                         lambda g, *_: (g // N_HEADS, (g % N_HEADS) // 2, 0, g % 2, 0)),
            pl.BlockSpec(memory_space=pl.ANY),
        ],
        out_specs=pl.BlockSpec((t_new, HEAD_DIM), head),
        scratch_shapes=[
            pltpu.VMEM((PREFETCH_DEPTH, n_fetch, ps, HEAD_DIM), F32),
            pltpu.VMEM((t_new, MOBA_TOPK * MOBA_BLOCK), F32),
            pltpu.SemaphoreType.DMA((PREFETCH_DEPTH,)),
        ],
    )
    return pl.pallas_call(
        functools.partial(_moba_sample_kernel, n_fetch=n_fetch, n_g=n_seq * N_HEADS),
        grid_spec=grid_spec,
        out_shape=jax.ShapeDtypeStruct(q_s.shape, F32),
        compiler_params=_params("arbitrary"),
        name="moba_sample",
    )(phys_pages, sel_blocks, q_s, k_new, v_new, scores, cache_v)


def _merge_kernel(x_ref, bh_ref, bm_ref, gate_ref, wlh_ref, wlm_ref, wout_ref, gffn_ref,
                  wr_ref, br_ref, cnt0_ref, x1_ref, h2_ref, ri_ref, rw_ref, cnt_ref, run_ref):
    i = pl.program_id(0)
    tm, d = x_ref.shape

    @pl.when(i == 0)
    def _():
        run_ref[...] = cnt0_ref[...]

    mixed = (gate_ref[:, 0:d] * _dot(bh_ref[...], wlh_ref[...])
             + gate_ref[:, d:2 * d] * _dot(bm_ref[...], wlm_ref[...]))
    x1 = x_ref[...] + _dot(mixed, wout_ref[...])
    x1_ref[...] = x1
    h2 = _rms(x1, gffn_ref[...])
    n_chunk = d // LANES
    for j in range(n_chunk):
        h2_ref[pl.ds(j, tm, stride=n_chunk), :] = h2[:, j * LANES:(j + 1) * LANES]

    logits = lax.dot_general(wr_ref[...], h2, (((1,), (1,)), ((), ())), precision=HIGHEST,
                             preferred_element_type=F32) + br_ref[...]
    sub_e = lax.broadcasted_iota(jnp.int32, (N_EXPERTS, tm), 0).astype(F32)
    lg = logits
    vals, ids, hots = [], [], []
    for _ in range(TOP_K):
        mx = jnp.max(lg, axis=0, keepdims=True)
        idx = jnp.min(jnp.where(lg == mx, sub_e, float(N_EXPERTS)), axis=0, keepdims=True)
        hot = sub_e == idx
        vals.append(mx)
        ids.append(idx)
        hots.append(hot)
        lg = jnp.where(hot, -jnp.inf, lg)
    exps = [jnp.exp(v - vals[0]) for v in vals]
    den = exps[0] + exps[1] + exps[2] + exps[3]

    chosen = jnp.zeros((N_EXPERTS, tm), F32)
    for hot in hots:
        chosen = chosen + jnp.where(hot, 1.0, 0.0)
    r_i = lax.broadcasted_iota(jnp.int32, (tm, tm), 0)
    c_i = lax.broadcasted_iota(jnp.int32, (tm, tm), 1)
    earlier = jnp.where(r_i < c_i, 1.0, 0.0)
    base = _dot(chosen, earlier) + run_ref[:, 0:1]
    run_ref[...] = run_ref[...] + jnp.sum(chosen, axis=1, keepdims=True)
    cnt_ref[...] = run_ref[...]

    sub_i = lax.broadcasted_iota(jnp.int32, (2 * TOP_K, tm), 0)
    sub_w = lax.broadcasted_iota(jnp.int32, (LANES, tm), 0)
    out_i = jnp.zeros((2 * TOP_K, tm), jnp.int32)
    out_w = jnp.zeros((LANES, tm), F32)
    for k in range(TOP_K):
        r_k = jnp.sum(jnp.where(hots[k], base, 0.0), axis=0, keepdims=True)
        out_i = jnp.where(sub_i == k, ids[k].astype(jnp.int32), out_i)
        out_i = jnp.where(sub_i == TOP_K + k, r_k.astype(jnp.int32), out_i)
        out_w = jnp.where(sub_w == k, exps[k] / den, out_w)
    ri_ref[...] = out_i
    rw_ref[...] = out_w.T


def _merge(x2d, bh, bm, gates, wlh, wlm, wout, g_ffn, w_router, b_router, cnt0):
    n, d = x2d.shape
    hw = N_HEADS * HEAD_DIM
    tm = MERGE_TILE if n % MERGE_TILE == 0 else n
    n_chunk = d // LANES
    row = lambda w: pl.BlockSpec((tm, w), lambda i: (i, 0))
    return pl.pallas_call(
        _merge_kernel,
        grid=(n // tm,),
        in_specs=[row(d), row(hw), row(hw), row(2 * d), _full((hw, d)), _full((hw, d)),
                  _full((d, d)), _full((1, d)), _full((N_EXPERTS, d)), _full((N_EXPERTS, 1)),
                  _full((N_EXPERTS, LANES))],
        out_specs=(row(d), pl.BlockSpec((tm * n_chunk, LANES), lambda i: (i, 0)),
                   pl.BlockSpec((2 * TOP_K, tm), lambda i: (0, i)), row(LANES),
                   _full((N_EXPERTS, LANES))),
        out_shape=(jax.ShapeDtypeStruct((n, d), F32),
                   jax.ShapeDtypeStruct((n * n_chunk, LANES), F32),
                   jax.ShapeDtypeStruct((2 * TOP_K, n), jnp.int32),
                   jax.ShapeDtypeStruct((n, LANES), F32),
                   jax.ShapeDtypeStruct((N_EXPERTS, LANES), F32)),
        scratch_shapes=[pltpu.VMEM((N_EXPERTS, LANES), F32)],
        compiler_params=_params("arbitrary"),
        name="merge_router",
    )(x2d, bh, bm, gates, wlh, wlm, wout, g_ffn.reshape(1, d), w_router.T,
      b_router.reshape(N_EXPERTS, 1), cnt0)


ZERO_ROWS = 64


def _dispatch_kernel(zlo_ref, zhi_ref, dest_p_ref, dest_s_ref, h2p_ref, h2s_ref, xs_hbm,
                     zbuf, sem, *, n_p, rows_per_tok):
    i = pl.program_id(0)
    r = rows_per_tok
    zrows = ZERO_ROWS * r

    def zero_copy(s):
        return pltpu.make_async_copy(zbuf, xs_hbm.at[pl.ds(pl.multiple_of(s * zrows, zrows), zrows)],
                                     sem.at[0])

    @pl.when(i == 0)
    def _():
        zbuf[...] = jnp.zeros(zbuf.shape, F32)

        def each_range(fn):
            def outer(e, carry):
                def inner(s, c):
                    fn(s)
                    return c
                return lax.fori_loop(zlo_ref[e], zhi_ref[e], inner, carry)
            lax.fori_loop(0, zlo_ref.shape[0], outer, 0)

        each_range(lambda s: zero_copy(s).start())
        each_range(lambda s: zero_copy(s).wait())

    def scatter(h2_ref, dest_ref):
        tm = h2_ref.shape[0] // r

        unroll = 8 if tm % 8 == 0 else 1

        def body(tu, carry):
            for u in range(unroll):
                t = tu * unroll + u
                src = h2_ref.at[pl.ds(pl.multiple_of(t * r, r), r)]
                for k in range(TOP_K):
                    dst = xs_hbm.at[pl.ds(pl.multiple_of(dest_ref[k * tm + t] * r, r), r)]
                    pltpu.make_async_copy(src, dst, sem.at[1]).start(priority=k % 2)
            return carry

        lax.fori_loop(0, tm // unroll, body, 0)
        n_rows = tm * TOP_K * r
        pltpu.make_async_copy(xs_hbm.at[pl.ds(0, n_rows)], xs_hbm.at[pl.ds(0, n_rows)],
                              sem.at[1]).wait()

    @pl.when(i < n_p)
    def _():
        scatter(h2p_ref, dest_p_ref)

    @pl.when(i >= n_p)
    def _():
        scatter(h2s_ref, dest_s_ref)


def _dispatch(zlo, zhi, dest_p, dest_s, h2_p, h2_s, n_slots, rows_per_tok):
    r = rows_per_tok
    tm_p = _token_tile(h2_p.shape[0] // r)
    tm_s = _token_tile(h2_s.shape[0] // r)
    n_p = h2_p.shape[0] // (tm_p * r)
    n_s = h2_s.shape[0] // (tm_s * r)
    first = lambda i, *_: (jnp.minimum(i, n_p - 1),)
    second = lambda i, *_: (jnp.maximum(i - n_p, 0),)
    grid_spec = pltpu.PrefetchScalarGridSpec(
        num_scalar_prefetch=2,
        grid=(n_p + n_s,),
        in_specs=[
            pl.BlockSpec((tm_p * TOP_K,), first, memory_space=pltpu.SMEM),
            pl.BlockSpec((tm_s * TOP_K,), second, memory_space=pltpu.SMEM),
            pl.BlockSpec((tm_p * r, LANES), lambda i, *_: first(i) + (0,)),
            pl.BlockSpec((tm_s * r, LANES), lambda i, *_: second(i) + (0,)),
        ],
        out_specs=pl.BlockSpec(memory_space=pl.ANY),
        scratch_shapes=[pltpu.VMEM((ZERO_ROWS * r, LANES), F32), pltpu.SemaphoreType.DMA((2,))],
    )
    return pl.pallas_call(
        functools.partial(_dispatch_kernel, n_p=n_p, rows_per_tok=r),
        grid_spec=grid_spec,
        out_shape=jax.ShapeDtypeStruct((n_slots * r, LANES), F32),
        compiler_params=_params("arbitrary"),
        name="moe_dispatch",
    )(zlo, zhi, dest_p, dest_s, h2_p, h2_s)


def _moe_kernel(te_ref, nu_ref, xs_ref, wgu_f32, bgu_ref, wdn_f32, bdn_ref, ys_ref,
                wgu_ref, wdn_ref):
    s = pl.program_id(0)
    i = s - 1
    n_chunk = xs_ref.shape[0] // MOE_TILE
    d_ff = wdn_ref.shape[0]
    used = (i >= 0) & (i < nu_ref[0])

    @pl.when(used)
    def _():
        x = jnp.concatenate(
            [xs_ref[pl.ds(j, MOE_TILE, stride=n_chunk), :] for j in range(n_chunk)], axis=-1)
        gu = _dot(x, wgu_ref[...]) + bgu_ref[...]
        g_ = jnp.minimum(gu[:, :d_ff], SWIGLU_LIMIT)
        u_ = jnp.clip(gu[:, d_ff:], -SWIGLU_LIMIT, SWIGLU_LIMIT)
        act = (u_ + 1.0) * g_ * _sigmoid(SWIGLU_ALPHA * g_)
        y = _dot(act, wdn_ref[...]) + bdn_ref[...]
        for j in range(n_chunk):
            ys_ref[pl.ds(j, MOE_TILE, stride=n_chunk), :] = y[:, j * LANES:(j + 1) * LANES]

    @pl.when(i >= nu_ref[0])
    def _():
        ys_ref[...] = jnp.zeros(ys_ref.shape, F32)

    nxt = jnp.minimum(s, te_ref.shape[0] - 1)
    new_expert = (s == 0) | (te_ref[nxt] != te_ref[jnp.maximum(s - 1, 0)])

    @pl.when((s < nu_ref[0]) & new_expert)
    def _():
        rows = 128
        for c in range(wgu_ref.shape[0] // rows):
            wgu_ref[c * rows:(c + 1) * rows, :] = wgu_f32[c * rows:(c + 1) * rows, :].astype(BF16)
        for c in range(wdn_ref.shape[0] // rows):
            wdn_ref[c * rows:(c + 1) * rows, :] = wdn_f32[c * rows:(c + 1) * rows, :].astype(BF16)


def _moe(tile_e, n_used, xs, w_gu, b_gu, w_dn, b_dn, rows_per_tok):
    n_tiles = xs.shape[0] // (MOE_TILE * rows_per_tok)
    d, two_ff = w_gu.shape[1:]
    d_ff = w_dn.shape[1]
    tile = lambda s, nu: jnp.clip(s - 1, 0, nu[0] - 1)
    ahead = lambda s, nu: jnp.minimum(s, nu[0] - 1)
    grid_spec = pltpu.PrefetchScalarGridSpec(
        num_scalar_prefetch=2,
        grid=(n_tiles + 1,),
        in_specs=[
            pl.BlockSpec((MOE_TILE * rows_per_tok, LANES), lambda s, te, nu: (tile(s, nu), 0)),
            pl.BlockSpec((None, d, two_ff), lambda s, te, nu: (te[ahead(s, nu)], 0, 0)),
            pl.BlockSpec((None, 1, two_ff), lambda s, te, nu: (te[tile(s, nu)], 0, 0)),
            pl.BlockSpec((None, d_ff, d), lambda s, te, nu: (te[ahead(s, nu)], 0, 0)),
            pl.BlockSpec((None, 1, d), lambda s, te, nu: (te[tile(s, nu)], 0, 0)),
        ],
        out_specs=pl.BlockSpec((MOE_TILE * rows_per_tok, LANES),
                               lambda s, te, nu: (jnp.maximum(s - 1, 0), 0)),
        scratch_shapes=[pltpu.VMEM((d, two_ff), BF16), pltpu.VMEM((d_ff, d), BF16)],
    )
    return pl.pallas_call(
        _moe_kernel,
        grid_spec=grid_spec,
        out_shape=jax.ShapeDtypeStruct(xs.shape, F32),
        compiler_params=_params("arbitrary"),
        name="moe_ffn",
    )(tile_e, n_used, xs, w_gu, b_gu.reshape(N_EXPERTS, 1, two_ff), w_dn,
      b_dn.reshape(N_EXPERTS, 1, d))


def _combine_kernel(dest_ref, dest_next_ref, ys_hbm, x1_ref, rw_ref, p_ref, gple_ref, wple_ref,
                    wgate_ref, y_ref, buf, sem, *, rows_per_tok):
    i = pl.program_id(0)
    n_i = pl.num_programs(0)
    tm, d = x1_ref.shape
    r = rows_per_tok

    def row_copy(d_ref, slot, t, k):
        src = ys_hbm.at[pl.ds(pl.multiple_of(d_ref[k * tm + t] * r, r), r)]
        dst = buf.at[slot, pl.ds(pl.multiple_of((k * tm + t) * r, r), r)]
        return pltpu.make_async_copy(src, dst, sem.at[slot])

    def wait_slot(slot):
        pltpu.make_async_copy(buf.at[slot], buf.at[slot], sem.at[slot]).wait()

    slot = i % 2

    @pl.when(i == 0)
    def _():
        def body(t, carry):
            for k in range(TOP_K):
                row_copy(dest_ref, 0, t, k).start(priority=k % 2)
            return carry
        lax.fori_loop(0, tm, body, 0)

    for t in range(tm):
        for k in range(TOP_K):
            row_copy(dest_next_ref, 1 - slot, t, k).start(priority=k % 2)

    pe = _dot(p_ref[...], wple_ref[...])
    wait_slot(slot)

    rw = rw_ref[...]
    cols = []
    for j in range(r):
        acc = jnp.zeros((tm, LANES), F32)
        for k in range(TOP_K):
            acc = acc + rw[:, k:k + 1] * buf[slot, pl.ds(k * tm * r + j, tm, stride=r), :]
        cols.append(acc)
    x2 = x1_ref[...] + jnp.concatenate(cols, axis=-1)
    gate = _sigmoid(_dot(_rms(x2, gple_ref[...]), wgate_ref[...]))
    y_ref[...] = x2 + pe * gate

    @pl.when(i == n_i - 1)
    def _():
        wait_slot(1 - slot)


def _combine(dest_flat, ys, x1, rw, p2d, g_ple, wple_bf, wgate_bf, rows_per_tok):
    n, d = x1.shape
    tm = _token_tile(n)
    n_i = n // tm
    ple = p2d.shape[1]
    row = lambda w: pl.BlockSpec((tm, w), lambda i: (i, 0))
    return pl.pallas_call(
        functools.partial(_combine_kernel, rows_per_tok=rows_per_tok),
        grid=(n_i,),
        in_specs=[pl.BlockSpec((tm * TOP_K,), lambda i: (i,), memory_space=pltpu.SMEM),
                  pl.BlockSpec((tm * TOP_K,), lambda i: (jnp.minimum(i + 1, n_i - 1),),
                               memory_space=pltpu.SMEM),
                  pl.BlockSpec(memory_space=pl.ANY), row(d), row(LANES), row(ple),
                  _full((1, d)), _full((ple, d)), _full((d, d))],
        out_specs=row(d),
        out_shape=jax.ShapeDtypeStruct((n, d), F32),
        scratch_shapes=[pltpu.VMEM((2, TOP_K * tm * rows_per_tok, LANES), F32),
                        pltpu.SemaphoreType.DMA((2,))],
        compiler_params=_params("arbitrary"),
        name="combine_ple",
    )(dest_flat, dest_flat, ys, x1, rw, p2d, g_ple.reshape(1, d), wple_bf, wgate_bf)


def kernel(x_prompt, x_sample, cache_k, cache_v, state_hgrn, page_table, p_prompt, p_sample,
           g_mix, w_in, lb_logits, g_hg_out, g_q, g_k, w_lift_hgrn, w_lift_moba, w_out, g_ffn,
           w_router, b_router, w_gate_up, b_gate_up, w_down, b_down, g_ple, w_ple, w_ple_gate):
    depth = g_mix.shape[0]
    assert depth == 1, "single-layer step"
    li = 0
    n_b, seq, d = x_prompt.shape
    n_db, t_new, _ = x_sample.shape
    ps = cache_k.shape[3]
    rows_per_tok = d // LANES
    assert rows_per_tok == SUBLANES and seq % ps == 0

    lb = jnp.cumsum(jax.nn.softmax(lb_logits.astype(F32), axis=0), axis=0)[li]
    w_in_bf = w_in[li].astype(BF16)
    wlh, wlm, wout = (w_lift_hgrn[li].astype(BF16), w_lift_moba[li].astype(BF16),
                      w_out[li].astype(BF16))
    wple_bf, wgate_bf = w_ple[li].astype(BF16), w_ple_gate[li].astype(BF16)

    xp = x_prompt.reshape(n_b * seq, d)
    xs_ = x_sample.reshape(n_db * t_new, d)

    zh_p, q_p, k_p, v_p, gates_p = _inproj(xp, g_mix[li], w_in_bf, lb, g_q[li], g_k[li], ps)
    bh_p, st_p = _hgrn(zh_p, None, g_hg_out[li], n_b, seq)
    bm_p = _moba_prompt(q_p, k_p, v_p, n_b, seq)

    zh_s, q_s, k_s, v_s, gates_s = _inproj(xs_, g_mix[li], w_in_bf, lb, g_q[li], g_k[li], t_new)
    bh_s, st_s = _hgrn(zh_s, state_hgrn[li], g_hg_out[li], n_db, t_new)
    sel, scores = _moba_select(q_s, cache_k, page_table, t_new)
    sel = sel[..., :MOBA_TOPK]
    ppb = MOBA_BLOCK // ps
    logical = (sel[..., None] * ppb + jnp.arange(ppb, dtype=jnp.int32)).reshape(n_db, -1)
    n_pg = page_table.shape[1]
    hit = logical[:, :, None] == jnp.arange(n_pg, dtype=jnp.int32)
    phys = jnp.sum(jnp.where(hit, page_table[:, None, :], 0), axis=-1).reshape(-1)
    bm_s = _moba_sample(q_s, k_s, v_s, scores, cache_v, phys, sel.reshape(-1), n_db, t_new)

    cnt0 = jnp.zeros((N_EXPERTS, LANES), F32)
    x1_p, h2_p, ri_p, rw_p, cnt_p = _merge(xp, bh_p, bm_p, gates_p, wlh, wlm, wout, g_ffn[li],
                                           w_router[li], b_router[li], cnt0)
    x1_s, h2_s, ri_s, rw_s, cnt_all = _merge(xs_, bh_s, bm_s, gates_s, wlh, wlm, wout, g_ffn[li],
                                             w_router[li], b_router[li], cnt_p)

    counts = cnt_all[:, 0].astype(jnp.int32)
    padded = (counts + MOE_TILE - 1) // MOE_TILE * MOE_TILE
    pad_end = jnp.cumsum(padded)
    pad_start = pad_end - padded
    n_assign = (n_b * seq + n_db * t_new) * TOP_K
    n_tiles = -(-(n_assign + N_EXPERTS * (MOE_TILE - 1)) // MOE_TILE)
    tile_lo = jnp.arange(n_tiles, dtype=jnp.int32) * MOE_TILE
    tile_e = jnp.minimum(jnp.sum(pad_end[None, :] <= tile_lo[:, None], axis=1),
                         N_EXPERTS - 1).astype(jnp.int32)
    n_used = (pad_end[-1:] // MOE_TILE).astype(jnp.int32)
    zlo = jnp.concatenate([(pad_start + counts) // ZERO_ROWS, pad_end[-1:] // ZERO_ROWS])
    zhi = jnp.concatenate([pad_end // ZERO_ROWS,
                           jnp.full((1,), n_tiles * MOE_TILE // ZERO_ROWS, jnp.int32)])

    def dest_of(ri):
        n = ri.shape[1]
        tm = _token_tile(n)
        e, rank = ri[:TOP_K], ri[TOP_K:]
        start = jnp.sum(jnp.where(e[:, :, None] == jnp.arange(N_EXPERTS, dtype=jnp.int32),
                                  pad_start, 0), axis=-1)
        return (start + rank).reshape(TOP_K, n // tm, tm).transpose(1, 0, 2).reshape(-1)

    dest_p, dest_s = dest_of(ri_p), dest_of(ri_s)
    xs = _dispatch(zlo.astype(jnp.int32), zhi.astype(jnp.int32), dest_p, dest_s, h2_p, h2_s,
                   n_tiles * MOE_TILE, rows_per_tok)
    ys = _moe(tile_e, n_used, xs, w_gate_up[li], b_gate_up[li], w_down[li], b_down[li],
              rows_per_tok)

    y_p = _combine(dest_p, ys, x1_p, rw_p, p_prompt[li].reshape(n_b * seq, -1), g_ple[li],
                   wple_bf, wgate_bf, rows_per_tok)
    y_s = _combine(dest_s, ys, x1_s, rw_s, p_sample[li].reshape(n_db * t_new, -1), g_ple[li],
                   wple_bf, wgate_bf, rows_per_tok)

    return (y_p.reshape(n_b, seq, d), y_s.reshape(n_db, t_new, d),
            k_p.reshape(1, n_b, seq // ps, N_HEADS, ps, HEAD_DIM),
            v_p.reshape(1, n_b, seq // ps, N_HEADS, ps, HEAD_DIM),
            k_s[None], v_s[None], st_p[None], st_s[None])
```

```python
import functools

import jax
import jax.numpy as jnp
from jax import lax
from jax.experimental import pallas as pl
from jax.experimental.pallas import tpu as pltpu

F32 = jnp.float32
BF16 = jnp.bfloat16
HIGHEST = lax.Precision.HIGHEST

EPS = 1e-6
N_HEADS = 4
HEAD_DIM = 128
HG_CHUNK = 64
MOBA_BLOCK = 256
MOBA_TOPK = 3
MOBA_QGROUP = 16
N_EXPERTS = 32
TOP_K = 4
SWIGLU_LIMIT = 7.0
SWIGLU_ALPHA = 1.702
NEG = -1e30

LANES = 128
SUBLANES = 8
TOKEN_TILE = 256
INPROJ_TILE = 256
HGRN_TILE = 512
MERGE_TILE = 512
MOE_TILE = 512
VMEM_LIMIT = 56 * 1024 * 1024


def _params(*sem):
    return pltpu.CompilerParams(dimension_semantics=sem, vmem_limit_bytes=VMEM_LIMIT)


def _sigmoid(x):
    return 1.0 / (1.0 + jnp.exp(-x))


def _silu(x):
    return x * _sigmoid(x)


def _rms(x, g):
    return x * lax.rsqrt(jnp.mean(x * x, axis=-1, keepdims=True) + EPS) * g


def _dot(a, b):
    return jnp.dot(a.astype(BF16), b.astype(BF16), preferred_element_type=F32)


def _dot_nt(a, b):
    return lax.dot_general(a.astype(BF16), b.astype(BF16), (((1,), (1,)), ((), ())),
                           preferred_element_type=F32)


def _dot_tn(a, b):
    return lax.dot_general(a.astype(BF16), b.astype(BF16), (((0,), (0,)), ((), ())),
                           preferred_element_type=F32)


def _token_tile(n):
    return TOKEN_TILE if n % TOKEN_TILE == 0 else n


def _full(shape):
    return pl.BlockSpec(shape, lambda *_: (0,) * len(shape))


def _inproj_kernel(x_ref, gmix_ref, w_ref, lb_ref, gq_ref, gk_ref,
                   zh_ref, q_ref, k_ref, v_ref, gate_ref):
    x = x_ref[...]
    h = _rms(x, gmix_ref[...]).astype(BF16)
    hw = N_HEADS * HEAD_DIM
    d = x.shape[-1]
    tm = x.shape[0]

    def proj(c0, width):
        return jnp.dot(h, w_ref[:, c0:c0 + width], preferred_element_type=F32)

    lb = lb_ref[...]
    zh_ref[:, 0:hw] = _silu(proj(0, hw))
    fa = proj(hw, hw)
    zh_ref[:, hw:2 * hw] = jnp.log(lb + (1.0 - lb) * _sigmoid(fa))
    zh_ref[:, 2 * hw:3 * hw] = (1.0 - lb) * _sigmoid(-fa)
    zh_ref[:, 3 * hw:4 * hw] = proj(2 * hw, hw)
    zh_ref[:, 4 * hw:5 * hw] = _silu(proj(3 * hw, hw))

    qb = proj(4 * hw, hw)
    kb = proj(5 * hw, hw)
    vb = proj(6 * hw, hw)
    rows = k_ref.shape[2]
    for hd in range(N_HEADS):
        sl = slice(hd * HEAD_DIM, (hd + 1) * HEAD_DIM)
        q_ref[:, sl] = _rms(qb[:, sl], gq_ref[...])
        k_ref[:, hd, :, :] = _rms(kb[:, sl], gk_ref[...]).reshape(tm // rows, rows, HEAD_DIM)
        v_ref[:, hd, :, :] = vb[:, sl].reshape(tm // rows, rows, HEAD_DIM)

    gate_ref[:, 0:d] = _sigmoid(proj(7 * hw, d))
    gate_ref[:, d:2 * d] = _sigmoid(proj(7 * hw + d, d))


def _inproj(x2d, g_mix, w_in_bf, lb, g_q, g_k, rows_per_page):
    n, d = x2d.shape
    hw = N_HEADS * HEAD_DIM
    tm = INPROJ_TILE if n % INPROJ_TILE == 0 else _token_tile(n)
    r = rows_per_page
    n_in = w_in_bf.shape[1]
    out_shape = (
        jax.ShapeDtypeStruct((n, 5 * hw), F32),
        jax.ShapeDtypeStruct((n, hw), F32),
        jax.ShapeDtypeStruct((n // r, N_HEADS, r, HEAD_DIM), F32),
        jax.ShapeDtypeStruct((n // r, N_HEADS, r, HEAD_DIM), F32),
        jax.ShapeDtypeStruct((n, 2 * d), F32),
    )
    row = lambda w: pl.BlockSpec((tm, w), lambda i: (i, 0))
    page = pl.BlockSpec((tm // r, N_HEADS, r, HEAD_DIM), lambda i: (i, 0, 0, 0))
    return pl.pallas_call(
        _inproj_kernel,
        grid=(n // tm,),
        in_specs=[row(d), _full((1, d)),
                  pl.BlockSpec((d, n_in), lambda i: (0, 0), pipeline_mode=pl.Buffered(1)),
                  _full((1, hw)), _full((1, HEAD_DIM)), _full((1, HEAD_DIM))],
        out_specs=(row(5 * hw), row(hw), page, page, row(2 * d)),
        out_shape=out_shape,
        compiler_params=_params("parallel"),
        name="inproj",
    )(x2d, g_mix.reshape(1, d), w_in_bf, lb.reshape(1, hw), g_q.reshape(1, HEAD_DIM),
      g_k.reshape(1, HEAD_DIM))


def _hgrn_kernel(*refs, chunk, has_s0, exact_mxu):
    if has_s0:
        zh_ref, s0_ref, g_ref, o_ref, sout_ref, st_ref = refs
    else:
        zh_ref, g_ref, o_ref, sout_ref, st_ref = refs
        s0_ref = None
    t = pl.program_id(1)
    hw = N_HEADS * HEAD_DIM
    n_sub = sout_ref.shape[0]
    tb = zh_ref.shape[0] // n_sub

    @pl.when(t == 0)
    def _():
        for s in range(n_sub):
            for hd in range(N_HEADS):
                if has_s0:
                    st_ref[s * N_HEADS + hd] = s0_ref[s, hd].T
                else:
                    st_ref[s * N_HEADS + hd] = jnp.zeros((HEAD_DIM, HEAD_DIM), F32)

    if exact_mxu:
        mm_nt = lambda a, b: lax.dot_general(a, b, (((1,), (1,)), ((), ())), precision=HIGHEST,
                                             preferred_element_type=F32)
        mm = lambda a, b: jnp.dot(a, b, precision=HIGHEST, preferred_element_type=F32)
        mm_tn = lambda a, b: lax.dot_general(a, b, (((0,), (0,)), ((), ())), precision=HIGHEST,
                                             preferred_element_type=F32)
    else:
        mm_nt, mm, mm_tn = _dot_nt, _dot, _dot_tn

    ri = lax.broadcasted_iota(jnp.int32, (chunk, chunk), 0)
    ci = lax.broadcasted_iota(jnp.int32, (chunk, chunk), 1)
    causal = ci <= ri
    tril = jnp.where(causal, 1.0, 0.0).astype(F32)

    sls = [slice(hd * HEAD_DIM, (hd + 1) * HEAD_DIM) for hd in range(N_HEADS)]
    n_c = tb // chunk
    pre = []
    for sc in range(n_sub * n_c):
        rows = slice(sc * chunk, (sc + 1) * chunk)
        k_all = zh_ref[rows, 2 * hw:3 * hw]
        b_all = jnp.dot(tril, zh_ref[rows, hw:2 * hw], precision=HIGHEST,
                        preferred_element_type=F32)
        b_end = b_all[chunk - 1:chunk, :]
        pre.append((rows, zh_ref[rows, 0:hw] * jnp.exp(b_all), k_all * jnp.exp(-b_all),
                    k_all * jnp.exp(b_end - b_all), jnp.exp(b_end)))
    intra, upd = [], []
    for rows, qd_all, kd_all, kk_all, _ in pre:
        vs = [zh_ref[rows, 3 * hw + hd * HEAD_DIM:3 * hw + (hd + 1) * HEAD_DIM]
              for hd in range(N_HEADS)]
        att = [mm_nt(qd_all[:, sl], kd_all[:, sl]) for sl in sls]
        upd.append([mm_tn(vs[hd], kk_all[:, sl]) for hd, sl in enumerate(sls)])
        intra.append([mm(jnp.where(causal, att[hd], 0.0), vs[hd]) for hd in range(N_HEADS)])
    for s in range(n_sub):
        sts = [st_ref[s * N_HEADS + hd] for hd in range(N_HEADS)]
        for c in range(s * n_c, (s + 1) * n_c):
            rows, qd_all, _, _, decay = pre[c]
            inter = [mm_nt(qd_all[:, sl], sts[hd]) for hd, sl in enumerate(sls)]
            for hd, sl in enumerate(sls):
                sts[hd] = decay[:, sl] * sts[hd] + upd[c][hd]
                og = zh_ref[rows, 4 * hw + hd * HEAD_DIM:4 * hw + (hd + 1) * HEAD_DIM]
                o_ref[rows, sl] = _rms(intra[c][hd] + inter[hd], g_ref[...]) * og
        for hd in range(N_HEADS):
            st_ref[s * N_HEADS + hd] = sts[hd]

    @pl.when(t == pl.num_programs(1) - 1)
    def _():
        for s in range(n_sub):
            for hd in range(N_HEADS):
                sout_ref[s, hd] = st_ref[s * N_HEADS + hd].T


def _hgrn(zh, s0, g_hg_out, n_seq, seq_len):
    hw = N_HEADS * HEAD_DIM
    chunk = min(HG_CHUNK, seq_len)
    assert seq_len % chunk == 0
    tb = min(HGRN_TILE, seq_len)
    assert seq_len % tb == 0 and tb % chunk == 0
    n_t = seq_len // tb
    n_sub = 1
    if n_t == 1:
        n_sub = max(d for d in range(1, HGRN_TILE // (8 * tb) + 1) if n_seq % d == 0)
    n_steps, tb = n_seq // n_sub, tb * n_sub
    has_s0 = s0 is not None
    state_spec = pl.BlockSpec((n_sub, N_HEADS, HEAD_DIM, HEAD_DIM), lambda b, t: (b, 0, 0, 0))
    in_specs = [pl.BlockSpec((tb, 5 * hw), lambda b, t: (b * n_t + t, 0))]
    args = [zh]
    if has_s0:
        in_specs.append(state_spec)
        args.append(s0)
    in_specs.append(_full((1, HEAD_DIM)))
    args.append(g_hg_out.reshape(1, HEAD_DIM))
    return pl.pallas_call(
        functools.partial(_hgrn_kernel, chunk=chunk, has_s0=has_s0, exact_mxu=False),
        grid=(n_steps, n_t),
        in_specs=in_specs,
        out_specs=(pl.BlockSpec((tb, hw), lambda b, t: (b * n_t + t, 0)), state_spec),
        out_shape=(jax.ShapeDtypeStruct((n_seq * seq_len, hw), F32),
                   jax.ShapeDtypeStruct((n_seq, N_HEADS, HEAD_DIM, HEAD_DIM), F32)),
        scratch_shapes=[pltpu.VMEM((n_sub * N_HEADS, HEAD_DIM, HEAD_DIM), F32)],
        compiler_params=_params("parallel", "arbitrary"),
        name="hgrn_prompt" if not has_s0 else "hgrn_sample",
    )(*args)


def _topk_rank(gsc, n_valid):
    rows, n_col = gsc.shape
    blk = lax.broadcasted_iota(jnp.int32, (rows, n_col), 1)
    rank = jnp.zeros((rows, n_col), jnp.int32)
    for m in range(n_col):
        sm = gsc[:, m:m + 1]
        beats = jnp.where(sm > gsc, 1, jnp.where(sm == gsc, jnp.where(blk > m, 1, 0), 0))
        rank = rank + jnp.where(m < n_valid, beats, 0)
    return blk, rank


def _moba_prompt_kernel(q_ref, k_ref, v_ref, o_ref, kmean_ref, kbf_ref, vt_ref, osc_ref, perm_ref):
    n = pl.program_id(1)
    n_b = kmean_ref.shape[1]
    ps = k_ref.shape[2]
    ppb = MOBA_BLOCK // ps
    scale = HEAD_DIM ** -0.5
    heads = range(N_HEADS)

    @pl.when(n == 0)
    def _():
        for hd in heads:
            for m in range(n_b):
                kb = k_ref[m * ppb:(m + 1) * ppb, hd].reshape(MOBA_BLOCK, HEAD_DIM)
                kmean_ref[hd, m:m + 1, :] = jnp.mean(kb, axis=0, keepdims=True)
                kbf_ref[hd, m] = kb.astype(BF16)
                for pp in range(ppb):
                    vt_ref[hd, m, :, pp * ps:(pp + 1) * ps] = v_ref[m * ppb + pp, hd].T.astype(BF16)

    def block(m, hd):
        return kbf_ref[hd, m], vt_ref[hd, m]

    key_i = lax.broadcasted_iota(jnp.int32, (MOBA_BLOCK, MOBA_BLOCK), 0)
    qry_i = lax.broadcasted_iota(jnp.int32, (MOBA_BLOCK, MOBA_BLOCK), 1)
    blk = lax.broadcasted_iota(jnp.int32, (n_b, MOBA_BLOCK), 0)
    qbs, sels, init = [], [], []
    for hd in heads:
        q = q_ref[:, hd * HEAD_DIM:(hd + 1) * HEAD_DIM]
        gsc = lax.dot_general(kmean_ref[hd], q, (((1,), (1,)), ((), ())), precision=HIGHEST,
                              preferred_element_type=F32)
        rank = jnp.zeros((n_b, MOBA_BLOCK), jnp.int32)
        for m in range(n_b):
            sm = gsc[m:m + 1, :]
            beats = jnp.where(sm > gsc, 1, jnp.where(sm == gsc, jnp.where(blk > m, 1, 0), 0))
            rank = rank + jnp.where(m < n, beats, 0)
        sels.append(jnp.where((blk < n) & (rank < MOBA_TOPK), 1.0, 0.0))
        qbs.append((q * scale).astype(BF16))
    raw = [_dot_nt(kbf_ref[hd, n], qbs[hd]) for hd in heads]
    mid = []
    for hd in heads:
        s = jnp.where(key_i <= qry_i, raw[hd], NEG)
        m0 = jnp.max(s, axis=0, keepdims=True)
        p = jnp.exp(s - m0)
        mid.append((m0, jnp.sum(p, axis=0, keepdims=True), p.astype(BF16)))
    for hd in heads:
        m0, l0, p = mid[hd]
        init.append((m0, l0, _dot(vt_ref[hd, n], p)))

    def body(m, carry):
        raw = [_dot_nt(kbf_ref[hd, m], qbs[hd]) for hd in heads]
        mid = []
        for hd in heads:
            m_i, l_i, _ = carry[hd]
            picked = jnp.sum(jnp.where(blk == m, sels[hd], 0.0), axis=0, keepdims=True)
            s = jnp.where(picked > 0.0, raw[hd], NEG)
            m_new = jnp.maximum(m_i, jnp.max(s, axis=0, keepdims=True))
            alpha = jnp.exp(m_i - m_new)
            p = jnp.exp(s - m_new)
            mid.append((m_new, alpha, alpha * l_i + jnp.sum(p, axis=0, keepdims=True),
                        p.astype(BF16)))
        out = []
        for hd in heads:
            m_new, alpha, l_new, p = mid[hd]
            out.append((m_new, l_new, alpha * carry[hd][2] + _dot(vt_ref[hd, m], p)))
        return tuple(out)

    final = lax.fori_loop(0, n, body, tuple(init))
    n_grp = MOBA_BLOCK // MOBA_QGROUP
    for hd in heads:
        _, l_f, acc_f = final[hd]
        osc_ref[hd] = (acc_f / l_f).T
        for qg in range(MOBA_QGROUP):
            perm_ref[qg, pl.ds(hd, n_grp, stride=N_HEADS), :] = (
                osc_ref[hd, pl.ds(qg, n_grp, stride=MOBA_QGROUP), :])
    for qg in range(MOBA_QGROUP):
        qh, ql = divmod(qg, N_HEADS)
        o_ref[ql, :, qh * HEAD_DIM:(qh + 1) * HEAD_DIM] = perm_ref[qg]


def _moba_prompt(q, k_pages, v_pages, n_seq, seq_len):
    assert seq_len % MOBA_BLOCK == 0 and MOBA_QGROUP == N_HEADS * N_HEADS
    n_b = seq_len // MOBA_BLOCK
    ps = k_pages.shape[2]
    assert MOBA_BLOCK % ps == 0
    pps = seq_len // ps
    hw = N_HEADS * HEAD_DIM
    n_grp = MOBA_BLOCK // MOBA_QGROUP
    qspec = pl.BlockSpec((MOBA_BLOCK, hw), lambda b, n: (b * n_b + n, 0))
    kvspec = pl.BlockSpec((pps, N_HEADS, ps, HEAD_DIM), lambda b, n: (b, 0, 0, 0))
    out = pl.pallas_call(
        _moba_prompt_kernel,
        grid=(n_seq, n_b),
        in_specs=[qspec, kvspec, kvspec],
        out_specs=pl.BlockSpec((None, N_HEADS, n_grp * N_HEADS, hw), lambda b, n: (b, 0, n, 0)),
        out_shape=jax.ShapeDtypeStruct((n_seq, N_HEADS, seq_len // N_HEADS, hw), F32),
        scratch_shapes=[pltpu.VMEM((N_HEADS, n_b, HEAD_DIM), F32),
                        pltpu.VMEM((N_HEADS, n_b, MOBA_BLOCK, HEAD_DIM), BF16),
                        pltpu.VMEM((N_HEADS, n_b, HEAD_DIM, MOBA_BLOCK), BF16),
                        pltpu.VMEM((N_HEADS, MOBA_BLOCK, HEAD_DIM), F32),
                        pltpu.VMEM((MOBA_QGROUP, n_grp * N_HEADS, HEAD_DIM), F32)],
        compiler_params=_params("parallel", "arbitrary"),
        name="moba_prompt",
    )(q, k_pages, v_pages)
    return out.reshape(n_seq * seq_len, hw)


PAGES_PER_STEP = 16
PREFETCH_DEPTH = 3


def _moba_select_kernel(pt_ref, q_ref, ck_hbm, sel_ref, sc_ref, kbuf, kmean_ref, qbd_ref, sem,
                        *, n_j, n_g):
    g = pl.program_id(0)
    j = g % n_j
    ps = kbuf.shape[3]
    ppb = MOBA_BLOCK // ps
    bps = PAGES_PER_STEP // ppb
    n_blk = kmean_ref.shape[1]
    t_new = q_ref.shape[0]
    scale = HEAD_DIM ** -0.5

    def copies(step, slot):
        return [pltpu.make_async_copy(ck_hbm.at[0, pt_ref[step * PAGES_PER_STEP + i]],
                                      kbuf.at[slot, i], sem.at[slot])
                for i in range(PAGES_PER_STEP)]

    @pl.when(g == 0)
    def _():
        for ahead in range(min(PREFETCH_DEPTH - 1, n_g)):
            for i, c in enumerate(copies(ahead, ahead)):
                c.start(priority=i % 2)

    nxt = g + PREFETCH_DEPTH - 1

    @pl.when(nxt < n_g)
    def _():
        for i, c in enumerate(copies(nxt, nxt % PREFETCH_DEPTH)):
            c.start(priority=i % 2)

    slot = g % PREFETCH_DEPTH
    for c in copies(g, slot):
        c.wait()
    page_refs = [kbuf.at[slot, i] for i in range(PAGES_PER_STEP)]

    @pl.when(j == 0)
    def _():
        zero = jnp.zeros((t_new, HEAD_DIM), F32)
        for p in range(N_HEADS // 2):
            q0 = q_ref[:, (2 * p) * HEAD_DIM:(2 * p + 1) * HEAD_DIM] * scale
            q1 = q_ref[:, (2 * p + 1) * HEAD_DIM:(2 * p + 2) * HEAD_DIM] * scale
            qbd_ref[p] = jnp.concatenate([jnp.concatenate([q0, zero], axis=-1),
                                          jnp.concatenate([zero, q1], axis=-1)], axis=0).astype(BF16)

    for i in range(PAGES_PER_STEP):
        for p in range(N_HEADS // 2):
            kcat = jnp.concatenate([page_refs[i][2 * p], page_refs[i][2 * p + 1]], axis=-1)
            sc_ref[p, i // ppb, :, (i % ppb) * ps:(i % ppb + 1) * ps] = _dot_nt(qbd_ref[p], kcat)

    for i in range(bps):
        for hd in range(N_HEADS):
            tot = jnp.zeros((1, HEAD_DIM), F32)
            for pp in range(ppb):
                tot = tot + jnp.sum(page_refs[i * ppb + pp][hd], axis=0, keepdims=True)
            kmean_ref[hd, pl.ds(j * bps + i, 1), :] = tot / float(MOBA_BLOCK)

    @pl.when(j == n_j - 1)
    def _():
        for hd in range(N_HEADS):
            q = q_ref[:, hd * HEAD_DIM:(hd + 1) * HEAD_DIM]
            gsc = lax.dot_general(q, kmean_ref[hd], (((1,), (1,)), ((), ())), precision=HIGHEST,
                                  preferred_element_type=F32)
            blk, rank = _topk_rank(gsc, n_blk)
            lane = lax.broadcasted_iota(jnp.int32, (t_new, LANES), 1)
            out = jnp.zeros((t_new, LANES), jnp.int32)
            for r in range(MOBA_TOPK):
                idx = jnp.sum(jnp.where(rank == r, blk.astype(F32), 0.0), axis=-1, keepdims=True)
                out = jnp.where(lane == r, idx.astype(jnp.int32), out)
            sel_ref[0, hd] = out


def _moba_select(q_s, cache_k, page_table, t_new):
    n_seq, n_pg = page_table.shape
    _, _, _, ps, _ = cache_k.shape
    assert (n_pg * ps) % MOBA_BLOCK == 0, "cached rows must end on a MoBA block boundary"
    assert n_pg % PAGES_PER_STEP == 0 and MOBA_BLOCK % ps == 0
    n_blk = n_pg * ps // MOBA_BLOCK
    assert n_blk >= MOBA_TOPK
    hw = N_HEADS * HEAD_DIM
    bps = PAGES_PER_STEP * ps // MOBA_BLOCK

    n_j = n_pg // PAGES_PER_STEP
    grid_spec = pltpu.PrefetchScalarGridSpec(
        num_scalar_prefetch=1,
        grid=(n_seq * n_j,),
        in_specs=[pl.BlockSpec((t_new, hw), lambda g, pt: (g // n_j, 0)),
                  pl.BlockSpec(memory_space=pl.ANY)],
        out_specs=(
            pl.BlockSpec((1, N_HEADS, t_new, LANES), lambda g, pt: (g // n_j, 0, 0, 0)),
            pl.BlockSpec((None, N_HEADS // 2, bps, 2 * t_new, MOBA_BLOCK),
                         lambda g, pt: (g // n_j, 0, g % n_j, 0, 0))),
        scratch_shapes=[pltpu.VMEM((PREFETCH_DEPTH, PAGES_PER_STEP, N_HEADS, ps, HEAD_DIM), F32),
                        pltpu.VMEM((N_HEADS, n_blk, HEAD_DIM), F32),
                        pltpu.VMEM((N_HEADS // 2, 2 * t_new, 2 * HEAD_DIM), BF16),
                        pltpu.SemaphoreType.DMA((PREFETCH_DEPTH,))],
    )
    return pl.pallas_call(
        functools.partial(_moba_select_kernel, n_j=n_j, n_g=n_seq * n_j),
        grid_spec=grid_spec,
        out_shape=(jax.ShapeDtypeStruct((n_seq, N_HEADS, t_new, LANES), jnp.int32),
                   jax.ShapeDtypeStruct((n_seq, N_HEADS // 2, n_blk, 2 * t_new, MOBA_BLOCK), F32)),
        compiler_params=_params("arbitrary"),
        name="moba_select",
    )(page_table.reshape(-1), q_s, cache_k)


def _moba_sample_kernel(pages_ref, blk_ref, nblk_ref, q_ref, kn_ref, vn_ref, sc_ref, cp_ref,
                        cv_hbm, o_ref, vbuf, ssel_ref, sem, *, n_fetch, n_g):
    g = pl.program_id(0)
    ps = vbuf.shape[2]
    t_new = q_ref.shape[0]
    ppb = MOBA_BLOCK // ps
    n_list = n_fetch // ppb
    scale = HEAD_DIM ** -0.5

    def fetch(step, slot, wait):
        hd = step % N_HEADS

        def body(j, carry):
            for pp in range(ppb):
                i = j * ppb + pp
                c = pltpu.make_async_copy(cv_hbm.at[0, pages_ref[step * n_fetch + i], hd],
                                          vbuf.at[slot, i], sem.at[slot])
                if wait:
                    c.wait()
                else:
                    c.start(priority=pp % 2)
            return carry

        lax.fori_loop(0, nblk_ref[step], body, 0)

    @pl.when(g == 0)
    def _():
        vbuf[...] = jnp.zeros(vbuf.shape, F32)
        for ahead in range(min(PREFETCH_DEPTH - 1, n_g)):
            fetch(ahead, ahead, False)

    nxt = g + PREFETCH_DEPTH - 1

    @pl.when(nxt < n_g)
    def _():
        fetch(nxt, nxt % PREFETCH_DEPTH, False)

    for t in range(t_new):
        for r in range(MOBA_TOPK):
            blk = blk_ref[(g * t_new + t) * MOBA_TOPK + r]
            ssel_ref[t:t + 1, r * MOBA_BLOCK:(r + 1) * MOBA_BLOCK] = sc_ref[blk, t:t + 1, :]
    s = ssel_ref[...]

    pad = jnp.zeros((2 * SUBLANES - t_new, HEAD_DIM), F32)
    qp = jnp.concatenate([q_ref[...] * scale, pad], axis=0)
    kn = jnp.concatenate([kn_ref[...], pad], axis=0)
    vn = jnp.concatenate([vn_ref[...], pad], axis=0)
    s_new = _dot_nt(qp, kn)[:t_new]
    rn = lax.broadcasted_iota(jnp.int32, s_new.shape, 0)
    cn = lax.broadcasted_iota(jnp.int32, s_new.shape, 1)
    s_new = jnp.where((cn <= rn) & (cn < t_new), s_new, NEG)
    m = jnp.maximum(jnp.max(s, axis=-1, keepdims=True), jnp.max(s_new, axis=-1, keepdims=True))
    p = jnp.exp(s - m)
    p_new = jnp.exp(s_new - m)
    den = jnp.sum(p, axis=-1, keepdims=True) + jnp.sum(p_new, axis=-1, keepdims=True)

    cp = cp_ref[...]
    cols = []
    for j in range(n_list):
        col = jnp.zeros((t_new, MOBA_BLOCK), F32)
        for r in range(MOBA_TOPK):
            col = col + jnp.where(cp[:, r:r + 1] == j, p[:, r * MOBA_BLOCK:(r + 1) * MOBA_BLOCK], 0.0)
        cols.append(col)
    p_wide = jnp.concatenate(cols, axis=-1)
    zero_rows = jnp.zeros((2 * SUBLANES - t_new, n_list * MOBA_BLOCK), F32)
    p_wide = jnp.concatenate([p_wide, zero_rows], axis=0)
    p_new = jnp.concatenate([p_new, jnp.zeros((2 * SUBLANES - t_new, p_new.shape[1]), F32)], axis=0)

    slot = g % PREFETCH_DEPTH
    fetch(g, slot, True)
    vc = vbuf[slot].reshape(n_fetch * ps, HEAD_DIM)
    o = _dot(p_wide, vc) + _dot(p_new, vn)
    o_ref[...] = o[:t_new] / den


def _moba_sample(q_s, k_new, v_new, scores, cache_v, phys_pages, sel_blocks, n_listed, pick_entry,
                 n_seq, t_new):
    ps = cache_v.shape[3]
    n_fetch = phys_pages.shape[0] // (n_seq * N_HEADS)
    n_blk = scores.shape[2]
    head = lambda g, *_: (g // N_HEADS, g % N_HEADS)
    grid_spec = pltpu.PrefetchScalarGridSpec(
        num_scalar_prefetch=3,
        grid=(n_seq * N_HEADS,),
        in_specs=[
            pl.BlockSpec((t_new, HEAD_DIM), head),
            pl.BlockSpec((None, None, t_new, HEAD_DIM), lambda g, *_: head(g) + (0, 0)),
            pl.BlockSpec((None, None, t_new, HEAD_DIM), lambda g, *_: head(g) + (0, 0)),
            pl.BlockSpec((None, None, n_blk, t_new, MOBA_BLOCK),
                         lambda g, *_: (g // N_HEADS, (g % N_HEADS) // 2, 0, g % 2, 0)),
            pl.BlockSpec((None, t_new, LANES), lambda g, *_: (g, 0, 0)),
            pl.BlockSpec(memory_space=pl.ANY),
        ],
        out_specs=pl.BlockSpec((t_new, HEAD_DIM), head),
        scratch_shapes=[
            pltpu.VMEM((PREFETCH_DEPTH, n_fetch, ps, HEAD_DIM), F32),
            pltpu.VMEM((t_new, MOBA_TOPK * MOBA_BLOCK), F32),
            pltpu.SemaphoreType.DMA((PREFETCH_DEPTH,)),
        ],
    )
    return pl.pallas_call(
        functools.partial(_moba_sample_kernel, n_fetch=n_fetch, n_g=n_seq * N_HEADS),
        grid_spec=grid_spec,
        out_shape=jax.ShapeDtypeStruct(q_s.shape, F32),
        compiler_params=_params("arbitrary"),
        name="moba_sample",
    )(phys_pages, sel_blocks, n_listed, q_s, k_new, v_new, scores, pick_entry, cache_v)


def _merge_kernel(x_ref, bh_ref, bm_ref, gate_ref, wlh_ref, wlm_ref, wout_ref, gffn_ref,
                  wr_ref, br_ref, cnt0_ref, x1_ref, h2_ref, ri_ref, rw_ref, cnt_ref, run_ref):
    i = pl.program_id(0)
    tm, d = x_ref.shape

    @pl.when(i == 0)
    def _():
        run_ref[...] = cnt0_ref[...]

    mixed = (gate_ref[:, 0:d] * _dot(bh_ref[...], wlh_ref[...])
             + gate_ref[:, d:2 * d] * _dot(bm_ref[...], wlm_ref[...]))
    x1 = x_ref[...] + _dot(mixed, wout_ref[...])
    x1_ref[...] = x1
    h2 = _rms(x1, gffn_ref[...])
    n_chunk = d // LANES
    for j in range(n_chunk):
        h2_ref[pl.ds(j, tm, stride=n_chunk), :] = h2[:, j * LANES:(j + 1) * LANES]

    logits = lax.dot_general(wr_ref[...], h2, (((1,), (1,)), ((), ())), precision=HIGHEST,
                             preferred_element_type=F32) + br_ref[...]
    sub_e = lax.broadcasted_iota(jnp.int32, (N_EXPERTS, tm), 0).astype(F32)
    lg = logits
    vals, ids, hots = [], [], []
    for _ in range(TOP_K):
        mx = jnp.max(lg, axis=0, keepdims=True)
        idx = jnp.min(jnp.where(lg == mx, sub_e, float(N_EXPERTS)), axis=0, keepdims=True)
        hot = sub_e == idx
        vals.append(mx)
        ids.append(idx)
        hots.append(hot)
        lg = jnp.where(hot, -jnp.inf, lg)
    exps = [jnp.exp(v - vals[0]) for v in vals]
    den = exps[0] + exps[1] + exps[2] + exps[3]

    chosen = jnp.zeros((N_EXPERTS, tm), F32)
    for hot in hots:
        chosen = chosen + jnp.where(hot, 1.0, 0.0)
    r_i = lax.broadcasted_iota(jnp.int32, (tm, tm), 0)
    c_i = lax.broadcasted_iota(jnp.int32, (tm, tm), 1)
    earlier = jnp.where(r_i < c_i, 1.0, 0.0)
    base = _dot(chosen, earlier) + run_ref[:, 0:1]
    run_ref[...] = run_ref[...] + jnp.sum(chosen, axis=1, keepdims=True)
    cnt_ref[...] = run_ref[...]

    sub_i = lax.broadcasted_iota(jnp.int32, (2 * TOP_K, tm), 0)
    sub_w = lax.broadcasted_iota(jnp.int32, (LANES, tm), 0)
    out_i = jnp.zeros((2 * TOP_K, tm), jnp.int32)
    out_w = jnp.zeros((LANES, tm), F32)
    for k in range(TOP_K):
        r_k = jnp.sum(jnp.where(hots[k], base, 0.0), axis=0, keepdims=True)
        out_i = jnp.where(sub_i == k, ids[k].astype(jnp.int32), out_i)
        out_i = jnp.where(sub_i == TOP_K + k, r_k.astype(jnp.int32), out_i)
        out_w = jnp.where(sub_w == k, exps[k] / den, out_w)
    ri_ref[...] = out_i
    rw_ref[...] = out_w.T


def _merge(x2d, bh, bm, gates, wlh, wlm, wout, g_ffn, w_router, b_router, cnt0):
    n, d = x2d.shape
    hw = N_HEADS * HEAD_DIM
    tm = MERGE_TILE if n % MERGE_TILE == 0 else n
    n_chunk = d // LANES
    row = lambda w: pl.BlockSpec((tm, w), lambda i: (i, 0))
    return pl.pallas_call(
        _merge_kernel,
        grid=(n // tm,),
        in_specs=[row(d), row(hw), row(hw), row(2 * d), _full((hw, d)), _full((hw, d)),
                  _full((d, d)), _full((1, d)), _full((N_EXPERTS, d)), _full((N_EXPERTS, 1)),
                  _full((N_EXPERTS, LANES))],
        out_specs=(row(d), pl.BlockSpec((tm * n_chunk, LANES), lambda i: (i, 0)),
                   pl.BlockSpec((2 * TOP_K, tm), lambda i: (0, i)), row(LANES),
                   _full((N_EXPERTS, LANES))),
        out_shape=(jax.ShapeDtypeStruct((n, d), F32),
                   jax.ShapeDtypeStruct((n * n_chunk, LANES), F32),
                   jax.ShapeDtypeStruct((2 * TOP_K, n), jnp.int32),
                   jax.ShapeDtypeStruct((n, LANES), F32),
                   jax.ShapeDtypeStruct((N_EXPERTS, LANES), F32)),
        scratch_shapes=[pltpu.VMEM((N_EXPERTS, LANES), F32)],
        compiler_params=_params("arbitrary"),
        name="merge_router",
    )(x2d, bh, bm, gates, wlh, wlm, wout, g_ffn.reshape(1, d), w_router.T,
      b_router.reshape(N_EXPERTS, 1), cnt0)


ZERO_ROWS = 64


def _dispatch_kernel(zlo_ref, zhi_ref, dest_p_ref, dest_s_ref, h2p_ref, h2s_ref, xs_hbm,
                     zbuf, sem, *, n_p, rows_per_tok):
    i = pl.program_id(0)
    r = rows_per_tok
    zrows = ZERO_ROWS * r

    def zero_copy(s):
        return pltpu.make_async_copy(zbuf, xs_hbm.at[pl.ds(pl.multiple_of(s * zrows, zrows), zrows)],
                                     sem.at[0])

    @pl.when(i == 0)
    def _():
        zbuf[...] = jnp.zeros(zbuf.shape, F32)

        def each_range(fn):
            def outer(e, carry):
                def inner(s, c):
                    fn(s)
                    return c
                return lax.fori_loop(zlo_ref[e], zhi_ref[e], inner, carry)
            lax.fori_loop(0, zlo_ref.shape[0], outer, 0)

        each_range(lambda s: zero_copy(s).start())
        each_range(lambda s: zero_copy(s).wait())

    def scatter(h2_ref, dest_ref):
        tm = h2_ref.shape[0] // r

        unroll = 8 if tm % 8 == 0 else 1

        def body(tu, carry):
            for u in range(unroll):
                t = tu * unroll + u
                src = h2_ref.at[pl.ds(pl.multiple_of(t * r, r), r)]
                for k in range(TOP_K):
                    dst = xs_hbm.at[pl.ds(pl.multiple_of(dest_ref[k * tm + t] * r, r), r)]
                    pltpu.make_async_copy(src, dst, sem.at[1]).start(priority=k % 2)
            return carry

        lax.fori_loop(0, tm // unroll, body, 0)
        n_rows = tm * TOP_K * r
        pltpu.make_async_copy(xs_hbm.at[pl.ds(0, n_rows)], xs_hbm.at[pl.ds(0, n_rows)],
                              sem.at[1]).wait()

    @pl.when(i < n_p)
    def _():
        scatter(h2p_ref, dest_p_ref)

    @pl.when(i >= n_p)
    def _():
        scatter(h2s_ref, dest_s_ref)


def _dispatch(zlo, zhi, dest_p, dest_s, h2_p, h2_s, n_slots, rows_per_tok):
    r = rows_per_tok
    tm_p = _token_tile(h2_p.shape[0] // r)
    tm_s = _token_tile(h2_s.shape[0] // r)
    n_p = h2_p.shape[0] // (tm_p * r)
    n_s = h2_s.shape[0] // (tm_s * r)
    first = lambda i, *_: (jnp.minimum(i, n_p - 1),)
    second = lambda i, *_: (jnp.maximum(i - n_p, 0),)
    grid_spec = pltpu.PrefetchScalarGridSpec(
        num_scalar_prefetch=2,
        grid=(n_p + n_s,),
        in_specs=[
            pl.BlockSpec((tm_p * TOP_K,), first, memory_space=pltpu.SMEM),
            pl.BlockSpec((tm_s * TOP_K,), second, memory_space=pltpu.SMEM),
            pl.BlockSpec((tm_p * r, LANES), lambda i, *_: first(i) + (0,)),
            pl.BlockSpec((tm_s * r, LANES), lambda i, *_: second(i) + (0,)),
        ],
        out_specs=pl.BlockSpec(memory_space=pl.ANY),
        scratch_shapes=[pltpu.VMEM((ZERO_ROWS * r, LANES), F32), pltpu.SemaphoreType.DMA((2,))],
    )
    return pl.pallas_call(
        functools.partial(_dispatch_kernel, n_p=n_p, rows_per_tok=r),
        grid_spec=grid_spec,
        out_shape=jax.ShapeDtypeStruct((n_slots * r, LANES), F32),
        compiler_params=_params("arbitrary"),
        name="moe_dispatch",
    )(zlo, zhi, dest_p, dest_s, h2_p, h2_s)


def _moe_kernel(te_ref, nu_ref, xs_ref, wgu_f32, bgu_ref, wdn_f32, bdn_ref, ys_ref,
                wgu_ref, wdn_ref):
    s = pl.program_id(0)
    i = s - 1
    n_chunk = xs_ref.shape[0] // MOE_TILE
    d_ff = wdn_ref.shape[0]
    used = (i >= 0) & (i < nu_ref[0])

    @pl.when(used)
    def _():
        x = jnp.concatenate(
            [xs_ref[pl.ds(j, MOE_TILE, stride=n_chunk), :] for j in range(n_chunk)], axis=-1)
        gu = _dot(x, wgu_ref[...]) + bgu_ref[...]
        g_ = jnp.minimum(gu[:, :d_ff], SWIGLU_LIMIT)
        u_ = jnp.clip(gu[:, d_ff:], -SWIGLU_LIMIT, SWIGLU_LIMIT)
        act = (u_ + 1.0) * g_ * _sigmoid(SWIGLU_ALPHA * g_)
        y = _dot(act, wdn_ref[...]) + bdn_ref[...]
        for j in range(n_chunk):
            ys_ref[pl.ds(j, MOE_TILE, stride=n_chunk), :] = y[:, j * LANES:(j + 1) * LANES]

    @pl.when(i >= nu_ref[0])
    def _():
        ys_ref[...] = jnp.zeros(ys_ref.shape, F32)

    nxt = jnp.minimum(s, te_ref.shape[0] - 1)
    new_expert = (s == 0) | (te_ref[nxt] != te_ref[jnp.maximum(s - 1, 0)])

    @pl.when((s < nu_ref[0]) & new_expert)
    def _():
        rows = 128
        for c in range(wgu_ref.shape[0] // rows):
            wgu_ref[c * rows:(c + 1) * rows, :] = wgu_f32[c * rows:(c + 1) * rows, :].astype(BF16)
        for c in range(wdn_ref.shape[0] // rows):
            wdn_ref[c * rows:(c + 1) * rows, :] = wdn_f32[c * rows:(c + 1) * rows, :].astype(BF16)


def _moe(tile_e, n_used, xs, w_gu, b_gu, w_dn, b_dn, rows_per_tok):
    n_tiles = xs.shape[0] // (MOE_TILE * rows_per_tok)
    d, two_ff = w_gu.shape[1:]
    d_ff = w_dn.shape[1]
    tile = lambda s, nu: jnp.clip(s - 1, 0, nu[0] - 1)
    ahead = lambda s, nu: jnp.minimum(s, nu[0] - 1)
    grid_spec = pltpu.PrefetchScalarGridSpec(
        num_scalar_prefetch=2,
        grid=(n_tiles + 1,),
        in_specs=[
            pl.BlockSpec((MOE_TILE * rows_per_tok, LANES), lambda s, te, nu: (tile(s, nu), 0)),
            pl.BlockSpec((None, d, two_ff), lambda s, te, nu: (te[ahead(s, nu)], 0, 0)),
            pl.BlockSpec((None, 1, two_ff), lambda s, te, nu: (te[tile(s, nu)], 0, 0)),
            pl.BlockSpec((None, d_ff, d), lambda s, te, nu: (te[ahead(s, nu)], 0, 0)),
            pl.BlockSpec((None, 1, d), lambda s, te, nu: (te[tile(s, nu)], 0, 0)),
        ],
        out_specs=pl.BlockSpec((MOE_TILE * rows_per_tok, LANES),
                               lambda s, te, nu: (jnp.maximum(s - 1, 0), 0)),
        scratch_shapes=[pltpu.VMEM((d, two_ff), BF16), pltpu.VMEM((d_ff, d), BF16)],
    )
    return pl.pallas_call(
        _moe_kernel,
        grid_spec=grid_spec,
        out_shape=jax.ShapeDtypeStruct(xs.shape, F32),
        compiler_params=_params("arbitrary"),
        name="moe_ffn",
    )(tile_e, n_used, xs, w_gu, b_gu.reshape(N_EXPERTS, 1, two_ff), w_dn,
      b_dn.reshape(N_EXPERTS, 1, d))


def _combine_kernel(dest_ref, dest_next_ref, ys_hbm, x1_ref, rw_ref, p_ref, gple_ref, wple_ref,
                    wgate_ref, y_ref, buf, sem, *, rows_per_tok):
    i = pl.program_id(0)
    n_i = pl.num_programs(0)
    tm, d = x1_ref.shape
    r = rows_per_tok

    def row_copy(d_ref, slot, t, k):
        src = ys_hbm.at[pl.ds(pl.multiple_of(d_ref[k * tm + t] * r, r), r)]
        dst = buf.at[slot, pl.ds(pl.multiple_of((k * tm + t) * r, r), r)]
        return pltpu.make_async_copy(src, dst, sem.at[slot])

    def wait_slot(slot):
        pltpu.make_async_copy(buf.at[slot], buf.at[slot], sem.at[slot]).wait()

    slot = i % 2

    @pl.when(i == 0)
    def _():
        def body(t, carry):
            for k in range(TOP_K):
                row_copy(dest_ref, 0, t, k).start(priority=k % 2)
            return carry
        lax.fori_loop(0, tm, body, 0)

    for t in range(tm):
        for k in range(TOP_K):
            row_copy(dest_next_ref, 1 - slot, t, k).start(priority=k % 2)

    pe = _dot(p_ref[...], wple_ref[...])
    wait_slot(slot)

    rw = rw_ref[...]
    cols = []
    for j in range(r):
        acc = jnp.zeros((tm, LANES), F32)
        for k in range(TOP_K):
            acc = acc + rw[:, k:k + 1] * buf[slot, pl.ds(k * tm * r + j, tm, stride=r), :]
        cols.append(acc)
    x2 = x1_ref[...] + jnp.concatenate(cols, axis=-1)
    gate = _sigmoid(_dot(_rms(x2, gple_ref[...]), wgate_ref[...]))
    y_ref[...] = x2 + pe * gate

    @pl.when(i == n_i - 1)
    def _():
        wait_slot(1 - slot)


def _combine(dest_flat, ys, x1, rw, p2d, g_ple, wple_bf, wgate_bf, rows_per_tok):
    n, d = x1.shape
    tm = _token_tile(n)
    n_i = n // tm
    ple = p2d.shape[1]
    row = lambda w: pl.BlockSpec((tm, w), lambda i: (i, 0))
    return pl.pallas_call(
        functools.partial(_combine_kernel, rows_per_tok=rows_per_tok),
        grid=(n_i,),
        in_specs=[pl.BlockSpec((tm * TOP_K,), lambda i: (i,), memory_space=pltpu.SMEM),
                  pl.BlockSpec((tm * TOP_K,), lambda i: (jnp.minimum(i + 1, n_i - 1),),
                               memory_space=pltpu.SMEM),
                  pl.BlockSpec(memory_space=pl.ANY), row(d), row(LANES), row(ple),
                  _full((1, d)), _full((ple, d)), _full((d, d))],
        out_specs=row(d),
        out_shape=jax.ShapeDtypeStruct((n, d), F32),
        scratch_shapes=[pltpu.VMEM((2, TOP_K * tm * rows_per_tok, LANES), F32),
                        pltpu.SemaphoreType.DMA((2,))],
        compiler_params=_params("arbitrary"),
        name="combine_ple",
    )(dest_flat, dest_flat, ys, x1, rw, p2d, g_ple.reshape(1, d), wple_bf, wgate_bf)


def kernel(x_prompt, x_sample, cache_k, cache_v, state_hgrn, page_table, p_prompt, p_sample,
           g_mix, w_in, lb_logits, g_hg_out, g_q, g_k, w_lift_hgrn, w_lift_moba, w_out, g_ffn,
           w_router, b_router, w_gate_up, b_gate_up, w_down, b_down, g_ple, w_ple, w_ple_gate):
    depth = g_mix.shape[0]
    assert depth == 1, "single-layer step"
    li = 0
    n_b, seq, d = x_prompt.shape
    n_db, t_new, _ = x_sample.shape
    ps = cache_k.shape[3]
    rows_per_tok = d // LANES
    assert rows_per_tok == SUBLANES and seq % ps == 0

    lb = jnp.cumsum(jax.nn.softmax(lb_logits.astype(F32), axis=0), axis=0)[li]
    w_in_bf = w_in[li].astype(BF16)
    wlh, wlm, wout = (w_lift_hgrn[li].astype(BF16), w_lift_moba[li].astype(BF16),
                      w_out[li].astype(BF16))
    wple_bf, wgate_bf = w_ple[li].astype(BF16), w_ple_gate[li].astype(BF16)

    xp = x_prompt.reshape(n_b * seq, d)
    xs_ = x_sample.reshape(n_db * t_new, d)

    zh_p, q_p, k_p, v_p, gates_p = _inproj(xp, g_mix[li], w_in_bf, lb, g_q[li], g_k[li], ps)
    bh_p, st_p = _hgrn(zh_p, None, g_hg_out[li], n_b, seq)
    bm_p = _moba_prompt(q_p, k_p, v_p, n_b, seq)

    zh_s, q_s, k_s, v_s, gates_s = _inproj(xs_, g_mix[li], w_in_bf, lb, g_q[li], g_k[li], t_new)
    bh_s, st_s = _hgrn(zh_s, state_hgrn[li], g_hg_out[li], n_db, t_new)
    sel, scores = _moba_select(q_s, cache_k, page_table, t_new)
    sel = sel[..., :MOBA_TOPK]
    ppb = MOBA_BLOCK // ps
    n_pg = page_table.shape[1]
    n_blk = n_pg // ppb
    n_pick = t_new * MOBA_TOPK
    blk_ids = jnp.arange(n_blk, dtype=jnp.int32)
    picks = sel.reshape(n_db * N_HEADS, n_pick)
    member = jnp.any(picks[:, :, None] == blk_ids, axis=1)
    entry_of_blk = jnp.cumsum(member.astype(jnp.int32), axis=1) - 1
    n_listed = jnp.sum(member.astype(jnp.int32), axis=1)
    listed = jnp.sum(jnp.where(member[:, None, :] & (entry_of_blk[:, None, :]
                                                     == jnp.arange(n_pick, dtype=jnp.int32)[:, None]),
                               blk_ids, 0), axis=-1)
    pick_entry = jnp.sum(jnp.where(picks[:, :, None] == blk_ids, entry_of_blk[:, None, :], 0),
                         axis=-1).reshape(n_db * N_HEADS, t_new, MOBA_TOPK)
    pick_entry = jnp.pad(pick_entry, ((0, 0), (0, 0), (0, LANES - MOBA_TOPK)))
    logical = (listed[..., None] * ppb + jnp.arange(ppb, dtype=jnp.int32)).reshape(n_db, -1)
    hit = logical[:, :, None] == jnp.arange(n_pg, dtype=jnp.int32)
    phys = jnp.sum(jnp.where(hit, page_table[:, None, :], 0), axis=-1).reshape(-1)
    bm_s = _moba_sample(q_s, k_s, v_s, scores, cache_v, phys, sel.reshape(-1), n_listed,
                        pick_entry, n_db, t_new)

    cnt0 = jnp.zeros((N_EXPERTS, LANES), F32)
    x1_p, h2_p, ri_p, rw_p, cnt_p = _merge(xp, bh_p, bm_p, gates_p, wlh, wlm, wout, g_ffn[li],
                                           w_router[li], b_router[li], cnt0)
    x1_s, h2_s, ri_s, rw_s, cnt_all = _merge(xs_, bh_s, bm_s, gates_s, wlh, wlm, wout, g_ffn[li],
                                             w_router[li], b_router[li], cnt_p)

    counts = cnt_all[:, 0].astype(jnp.int32)
    padded = (counts + MOE_TILE - 1) // MOE_TILE * MOE_TILE
    pad_end = jnp.cumsum(padded)
    pad_start = pad_end - padded
    n_assign = (n_b * seq + n_db * t_new) * TOP_K
    n_tiles = -(-(n_assign + N_EXPERTS * (MOE_TILE - 1)) // MOE_TILE)
    tile_lo = jnp.arange(n_tiles, dtype=jnp.int32) * MOE_TILE
    tile_e = jnp.minimum(jnp.sum(pad_end[None, :] <= tile_lo[:, None], axis=1),
                         N_EXPERTS - 1).astype(jnp.int32)
    n_used = (pad_end[-1:] // MOE_TILE).astype(jnp.int32)
    zlo = jnp.concatenate([(pad_start + counts) // ZERO_ROWS, pad_end[-1:] // ZERO_ROWS])
    zhi = jnp.concatenate([pad_end // ZERO_ROWS,
                           jnp.full((1,), n_tiles * MOE_TILE // ZERO_ROWS, jnp.int32)])

    def dest_of(ri):
        n = ri.shape[1]
        tm = _token_tile(n)
        e, rank = ri[:TOP_K], ri[TOP_K:]
        start = jnp.sum(jnp.where(e[:, :, None] == jnp.arange(N_EXPERTS, dtype=jnp.int32),
                                  pad_start, 0), axis=-1)
        return (start + rank).reshape(TOP_K, n // tm, tm).transpose(1, 0, 2).reshape(-1)

    dest_p, dest_s = dest_of(ri_p), dest_of(ri_s)
    xs = _dispatch(zlo.astype(jnp.int32), zhi.astype(jnp.int32), dest_p, dest_s, h2_p, h2_s,
                   n_tiles * MOE_TILE, rows_per_tok)
    ys = _moe(tile_e, n_used, xs, w_gate_up[li], b_gate_up[li], w_down[li], b_down[li],
              rows_per_tok)

    y_p = _combine(dest_p, ys, x1_p, rw_p, p_prompt[li].reshape(n_b * seq, -1), g_ple[li],
                   wple_bf, wgate_bf, rows_per_tok)
    y_s = _combine(dest_s, ys, x1_s, rw_s, p_sample[li].reshape(n_db * t_new, -1), g_ple[li],
                   wple_bf, wgate_bf, rows_per_tok)

    return (y_p.reshape(n_b, seq, d), y_s.reshape(n_db, t_new, d),
            k_p.reshape(1, n_b, seq // ps, N_HEADS, ps, HEAD_DIM),
            v_p.reshape(1, n_b, seq // ps, N_HEADS, ps, HEAD_DIM),
            k_s[None], v_s[None], st_p[None], st_s[None])
```

```python
import functools

import jax
import jax.numpy as jnp
from jax import lax
from jax.experimental import pallas as pl
from jax.experimental.pallas import tpu as pltpu

F32 = jnp.float32
BF16 = jnp.bfloat16
HIGHEST = lax.Precision.HIGHEST

EPS = 1e-6
N_HEADS = 4
HEAD_DIM = 128
HG_CHUNK = 64
MOBA_BLOCK = 256
MOBA_TOPK = 3
MOBA_QGROUP = 16
N_EXPERTS = 32
TOP_K = 4
SWIGLU_LIMIT = 7.0
SWIGLU_ALPHA = 1.702
NEG = -1e30

LANES = 128
SUBLANES = 8
TOKEN_TILE = 512
INPROJ_TILE = 256
HGRN_TILE = 512
MERGE_TILE = 512
MOE_TILE = 512
VMEM_LIMIT = 56 * 1024 * 1024


def _params(*sem):
    return pltpu.CompilerParams(dimension_semantics=sem, vmem_limit_bytes=VMEM_LIMIT)


def _sigmoid(x):
    return 1.0 / (1.0 + jnp.exp(-x))


def _silu(x):
    return x * _sigmoid(x)


def _rms(x, g):
    return x * lax.rsqrt(jnp.mean(x * x, axis=-1, keepdims=True) + EPS) * g


def _dot(a, b):
    return jnp.dot(a.astype(BF16), b.astype(BF16), preferred_element_type=F32)


def _dot_nt(a, b):
    return lax.dot_general(a.astype(BF16), b.astype(BF16), (((1,), (1,)), ((), ())),
                           preferred_element_type=F32)


def _dot_tn(a, b):
    return lax.dot_general(a.astype(BF16), b.astype(BF16), (((0,), (0,)), ((), ())),
                           preferred_element_type=F32)


def _token_tile(n):
    return TOKEN_TILE if n % TOKEN_TILE == 0 else n


def _full(shape):
    return pl.BlockSpec(shape, lambda *_: (0,) * len(shape))


def _inproj_kernel(x_ref, gmix_ref, w_ref, lb_ref, gq_ref, gk_ref,
                   zh_ref, q_ref, k_ref, v_ref, gate_ref):
    x = x_ref[...]
    h = _rms(x, gmix_ref[...]).astype(BF16)
    hw = N_HEADS * HEAD_DIM
    d = x.shape[-1]
    tm = x.shape[0]

    def proj(c0, width):
        return jnp.dot(h, w_ref[:, c0:c0 + width], preferred_element_type=F32)

    lb = lb_ref[...]
    zh_ref[:, 0:hw] = _silu(proj(0, hw))
    fa = proj(hw, hw)
    zh_ref[:, hw:2 * hw] = jnp.log(lb + (1.0 - lb) * _sigmoid(fa))
    zh_ref[:, 2 * hw:3 * hw] = (1.0 - lb) * _sigmoid(-fa)
    zh_ref[:, 3 * hw:4 * hw] = proj(2 * hw, hw)
    zh_ref[:, 4 * hw:5 * hw] = _silu(proj(3 * hw, hw))

    qb = proj(4 * hw, hw)
    kb = proj(5 * hw, hw)
    vb = proj(6 * hw, hw)
    rows = k_ref.shape[2]
    for hd in range(N_HEADS):
        sl = slice(hd * HEAD_DIM, (hd + 1) * HEAD_DIM)
        q_ref[:, sl] = _rms(qb[:, sl], gq_ref[...])
        k_ref[:, hd, :, :] = _rms(kb[:, sl], gk_ref[...]).reshape(tm // rows, rows, HEAD_DIM)
        v_ref[:, hd, :, :] = vb[:, sl].reshape(tm // rows, rows, HEAD_DIM)

    gate_ref[:, 0:d] = _sigmoid(proj(7 * hw, d))
    gate_ref[:, d:2 * d] = _sigmoid(proj(7 * hw + d, d))


def _inproj(x2d, g_mix, w_in_bf, lb, g_q, g_k, rows_per_page):
    n, d = x2d.shape
    hw = N_HEADS * HEAD_DIM
    tm = INPROJ_TILE if n % INPROJ_TILE == 0 else _token_tile(n)
    r = rows_per_page
    n_in = w_in_bf.shape[1]
    out_shape = (
        jax.ShapeDtypeStruct((n, 5 * hw), F32),
        jax.ShapeDtypeStruct((n, hw), F32),
        jax.ShapeDtypeStruct((n // r, N_HEADS, r, HEAD_DIM), F32),
        jax.ShapeDtypeStruct((n // r, N_HEADS, r, HEAD_DIM), F32),
        jax.ShapeDtypeStruct((n, 2 * d), F32),
    )
    row = lambda w: pl.BlockSpec((tm, w), lambda i: (i, 0))
    page = pl.BlockSpec((tm // r, N_HEADS, r, HEAD_DIM), lambda i: (i, 0, 0, 0))
    return pl.pallas_call(
        _inproj_kernel,
        grid=(n // tm,),
        in_specs=[row(d), _full((1, d)),
                  pl.BlockSpec((d, n_in), lambda i: (0, 0), pipeline_mode=pl.Buffered(1)),
                  _full((1, hw)), _full((1, HEAD_DIM)), _full((1, HEAD_DIM))],
        out_specs=(row(5 * hw), row(hw), page, page, row(2 * d)),
        out_shape=out_shape,
        compiler_params=_params("parallel"),
        name="inproj",
    )(x2d, g_mix.reshape(1, d), w_in_bf, lb.reshape(1, hw), g_q.reshape(1, HEAD_DIM),
      g_k.reshape(1, HEAD_DIM))


def _hgrn_kernel(*refs, chunk, has_s0, exact_mxu):
    if has_s0:
        zh_ref, s0_ref, g_ref, o_ref, sout_ref, st_ref = refs
    else:
        zh_ref, g_ref, o_ref, sout_ref, st_ref = refs
        s0_ref = None
    t = pl.program_id(1)
    hw = N_HEADS * HEAD_DIM
    n_sub = sout_ref.shape[0]
    tb = zh_ref.shape[0] // n_sub

    @pl.when(t == 0)
    def _():
        for s in range(n_sub):
            for hd in range(N_HEADS):
                if has_s0:
                    st_ref[s * N_HEADS + hd] = s0_ref[s, hd].T
                else:
                    st_ref[s * N_HEADS + hd] = jnp.zeros((HEAD_DIM, HEAD_DIM), F32)

    if exact_mxu:
        mm_nt = lambda a, b: lax.dot_general(a, b, (((1,), (1,)), ((), ())), precision=HIGHEST,
                                             preferred_element_type=F32)
        mm = lambda a, b: jnp.dot(a, b, precision=HIGHEST, preferred_element_type=F32)
        mm_tn = lambda a, b: lax.dot_general(a, b, (((0,), (0,)), ((), ())), precision=HIGHEST,
                                             preferred_element_type=F32)
    else:
        mm_nt, mm, mm_tn = _dot_nt, _dot, _dot_tn

    ri = lax.broadcasted_iota(jnp.int32, (chunk, chunk), 0)
    ci = lax.broadcasted_iota(jnp.int32, (chunk, chunk), 1)
    causal = ci <= ri
    tril = jnp.where(causal, 1.0, 0.0).astype(F32)

    sls = [slice(hd * HEAD_DIM, (hd + 1) * HEAD_DIM) for hd in range(N_HEADS)]
    n_c = tb // chunk
    pre = []
    for sc in range(n_sub * n_c):
        rows = slice(sc * chunk, (sc + 1) * chunk)
        k_all = zh_ref[rows, 2 * hw:3 * hw]
        b_all = jnp.dot(tril, zh_ref[rows, hw:2 * hw], precision=HIGHEST,
                        preferred_element_type=F32)
        b_end = b_all[chunk - 1:chunk, :]
        pre.append((rows, zh_ref[rows, 0:hw] * jnp.exp(b_all), k_all * jnp.exp(-b_all),
                    k_all * jnp.exp(b_end - b_all), jnp.exp(b_end)))
    intra, upd = [], []
    for rows, qd_all, kd_all, kk_all, _ in pre:
        vs = [zh_ref[rows, 3 * hw + hd * HEAD_DIM:3 * hw + (hd + 1) * HEAD_DIM]
              for hd in range(N_HEADS)]
        att = [mm_nt(qd_all[:, sl], kd_all[:, sl]) for sl in sls]
        upd.append([mm_tn(vs[hd], kk_all[:, sl]) for hd, sl in enumerate(sls)])
        intra.append([mm(jnp.where(causal, att[hd], 0.0), vs[hd]) for hd in range(N_HEADS)])
    for s in range(n_sub):
        sts = [st_ref[s * N_HEADS + hd] for hd in range(N_HEADS)]
        for c in range(s * n_c, (s + 1) * n_c):
            rows, qd_all, _, _, decay = pre[c]
            inter = [mm_nt(qd_all[:, sl], sts[hd]) for hd, sl in enumerate(sls)]
            for hd, sl in enumerate(sls):
                sts[hd] = decay[:, sl] * sts[hd] + upd[c][hd]
                og = zh_ref[rows, 4 * hw + hd * HEAD_DIM:4 * hw + (hd + 1) * HEAD_DIM]
                o_ref[rows, sl] = _rms(intra[c][hd] + inter[hd], g_ref[...]) * og
        for hd in range(N_HEADS):
            st_ref[s * N_HEADS + hd] = sts[hd]

    @pl.when(t == pl.num_programs(1) - 1)
    def _():
        for s in range(n_sub):
            for hd in range(N_HEADS):
                sout_ref[s, hd] = st_ref[s * N_HEADS + hd].T


def _hgrn(zh, s0, g_hg_out, n_seq, seq_len):
    hw = N_HEADS * HEAD_DIM
    chunk = min(HG_CHUNK, seq_len)
    assert seq_len % chunk == 0
    tb = min(HGRN_TILE, seq_len)
    assert seq_len % tb == 0 and tb % chunk == 0
    n_t = seq_len // tb
    n_sub = 1
    if n_t == 1:
        n_sub = max(d for d in range(1, HGRN_TILE // (8 * tb) + 1) if n_seq % d == 0)
    n_steps, tb = n_seq // n_sub, tb * n_sub
    has_s0 = s0 is not None
    state_spec = pl.BlockSpec((n_sub, N_HEADS, HEAD_DIM, HEAD_DIM), lambda b, t: (b, 0, 0, 0))
    in_specs = [pl.BlockSpec((tb, 5 * hw), lambda b, t: (b * n_t + t, 0))]
    args = [zh]
    if has_s0:
        in_specs.append(state_spec)
        args.append(s0)
    in_specs.append(_full((1, HEAD_DIM)))
    args.append(g_hg_out.reshape(1, HEAD_DIM))
    return pl.pallas_call(
        functools.partial(_hgrn_kernel, chunk=chunk, has_s0=has_s0, exact_mxu=False),
        grid=(n_steps, n_t),
        in_specs=in_specs,
        out_specs=(pl.BlockSpec((tb, hw), lambda b, t: (b * n_t + t, 0)), state_spec),
        out_shape=(jax.ShapeDtypeStruct((n_seq * seq_len, hw), F32),
                   jax.ShapeDtypeStruct((n_seq, N_HEADS, HEAD_DIM, HEAD_DIM), F32)),
        scratch_shapes=[pltpu.VMEM((n_sub * N_HEADS, HEAD_DIM, HEAD_DIM), F32)],
        compiler_params=_params("parallel", "arbitrary"),
        name="hgrn_prompt" if not has_s0 else "hgrn_sample",
    )(*args)


def _topk_rank(gsc, n_valid):
    rows, n_col = gsc.shape
    blk = lax.broadcasted_iota(jnp.int32, (rows, n_col), 1)
    rank = jnp.zeros((rows, n_col), jnp.int32)
    for m in range(n_col):
        sm = gsc[:, m:m + 1]
        beats = jnp.where(sm > gsc, 1, jnp.where(sm == gsc, jnp.where(blk > m, 1, 0), 0))
        rank = rank + jnp.where(m < n_valid, beats, 0)
    return blk, rank


def _moba_prompt_kernel(q_ref, k_ref, v_ref, o_ref, kmean_ref, kbf_ref, vt_ref, osc_ref, perm_ref):
    n = pl.program_id(1)
    n_b = kmean_ref.shape[1]
    ps = k_ref.shape[2]
    ppb = MOBA_BLOCK // ps
    scale = HEAD_DIM ** -0.5
    heads = range(N_HEADS)

    @pl.when(n == 0)
    def _():
        for hd in heads:
            for m in range(n_b):
                kb = k_ref[m * ppb:(m + 1) * ppb, hd].reshape(MOBA_BLOCK, HEAD_DIM)
                kmean_ref[hd, m:m + 1, :] = jnp.mean(kb, axis=0, keepdims=True)
                kbf_ref[hd, m] = kb.astype(BF16)
                for pp in range(ppb):
                    vt_ref[hd, m, :, pp * ps:(pp + 1) * ps] = v_ref[m * ppb + pp, hd].T.astype(BF16)

    def block(m, hd):
        return kbf_ref[hd, m], vt_ref[hd, m]

    key_i = lax.broadcasted_iota(jnp.int32, (MOBA_BLOCK, MOBA_BLOCK), 0)
    qry_i = lax.broadcasted_iota(jnp.int32, (MOBA_BLOCK, MOBA_BLOCK), 1)
    blk = lax.broadcasted_iota(jnp.int32, (n_b, MOBA_BLOCK), 0)
    qbs, sels, init = [], [], []
    for hd in heads:
        q = q_ref[:, hd * HEAD_DIM:(hd + 1) * HEAD_DIM]
        gsc = lax.dot_general(kmean_ref[hd], q, (((1,), (1,)), ((), ())), precision=HIGHEST,
                              preferred_element_type=F32)
        rank = jnp.zeros((n_b, MOBA_BLOCK), jnp.int32)
        for m in range(n_b):
            sm = gsc[m:m + 1, :]
            beats = jnp.where(sm > gsc, 1, jnp.where(sm == gsc, jnp.where(blk > m, 1, 0), 0))
            rank = rank + jnp.where(m < n, beats, 0)
        sels.append(jnp.where((blk < n) & (rank < MOBA_TOPK), 1.0, 0.0))
        qbs.append((q * scale).astype(BF16))
    raw = [_dot_nt(kbf_ref[hd, n], qbs[hd]) for hd in heads]
    mid = []
    for hd in heads:
        s = jnp.where(key_i <= qry_i, raw[hd], NEG)
        m0 = jnp.max(s, axis=0, keepdims=True)
        p = jnp.exp(s - m0)
        mid.append((m0, jnp.sum(p, axis=0, keepdims=True), p.astype(BF16)))
    for hd in heads:
        m0, l0, p = mid[hd]
        init.append((m0, l0, _dot(vt_ref[hd, n], p)))

    def body(m, carry):
        raw = [_dot_nt(kbf_ref[hd, m], qbs[hd]) for hd in heads]
        mid = []
        for hd in heads:
            m_i, l_i, _ = carry[hd]
            picked = jnp.sum(jnp.where(blk == m, sels[hd], 0.0), axis=0, keepdims=True)
            s = jnp.where(picked > 0.0, raw[hd], NEG)
            m_new = jnp.maximum(m_i, jnp.max(s, axis=0, keepdims=True))
            alpha = jnp.exp(m_i - m_new)
            p = jnp.exp(s - m_new)
            mid.append((m_new, alpha, alpha * l_i + jnp.sum(p, axis=0, keepdims=True),
                        p.astype(BF16)))
        out = []
        for hd in heads:
            m_new, alpha, l_new, p = mid[hd]
            out.append((m_new, l_new, alpha * carry[hd][2] + _dot(vt_ref[hd, m], p)))
        return tuple(out)

    final = lax.fori_loop(0, n, body, tuple(init))
    n_grp = MOBA_BLOCK // MOBA_QGROUP
    for hd in heads:
        _, l_f, acc_f = final[hd]
        osc_ref[hd] = (acc_f / l_f).T
        for qg in range(MOBA_QGROUP):
            perm_ref[qg, pl.ds(hd, n_grp, stride=N_HEADS), :] = (
                osc_ref[hd, pl.ds(qg, n_grp, stride=MOBA_QGROUP), :])
    for qg in range(MOBA_QGROUP):
        qh, ql = divmod(qg, N_HEADS)
        o_ref[ql, :, qh * HEAD_DIM:(qh + 1) * HEAD_DIM] = perm_ref[qg]


def _moba_prompt(q, k_pages, v_pages, n_seq, seq_len):
    assert seq_len % MOBA_BLOCK == 0 and MOBA_QGROUP == N_HEADS * N_HEADS
    n_b = seq_len // MOBA_BLOCK
    ps = k_pages.shape[2]
    assert MOBA_BLOCK % ps == 0
    pps = seq_len // ps
    hw = N_HEADS * HEAD_DIM
    n_grp = MOBA_BLOCK // MOBA_QGROUP
    qspec = pl.BlockSpec((MOBA_BLOCK, hw), lambda b, n: (b * n_b + n, 0))
    kvspec = pl.BlockSpec((pps, N_HEADS, ps, HEAD_DIM), lambda b, n: (b, 0, 0, 0))
    out = pl.pallas_call(
        _moba_prompt_kernel,
        grid=(n_seq, n_b),
        in_specs=[qspec, kvspec, kvspec],
        out_specs=pl.BlockSpec((None, N_HEADS, n_grp * N_HEADS, hw), lambda b, n: (b, 0, n, 0)),
        out_shape=jax.ShapeDtypeStruct((n_seq, N_HEADS, seq_len // N_HEADS, hw), F32),
        scratch_shapes=[pltpu.VMEM((N_HEADS, n_b, HEAD_DIM), F32),
                        pltpu.VMEM((N_HEADS, n_b, MOBA_BLOCK, HEAD_DIM), BF16),
                        pltpu.VMEM((N_HEADS, n_b, HEAD_DIM, MOBA_BLOCK), BF16),
                        pltpu.VMEM((N_HEADS, MOBA_BLOCK, HEAD_DIM), F32),
                        pltpu.VMEM((MOBA_QGROUP, n_grp * N_HEADS, HEAD_DIM), F32)],
        compiler_params=_params("parallel", "arbitrary"),
        name="moba_prompt",
    )(q, k_pages, v_pages)
    return out.reshape(n_seq * seq_len, hw)


PAGES_PER_STEP = 16
PREFETCH_DEPTH = 3


def _moba_select_kernel(pt_ref, q_ref, ck_hbm, sel_ref, sc_ref, kbuf, kmean_ref, qbd_ref, sem,
                        *, n_j, n_g):
    g = pl.program_id(0)
    j = g % n_j
    ps = kbuf.shape[3]
    ppb = MOBA_BLOCK // ps
    bps = PAGES_PER_STEP // ppb
    n_blk = kmean_ref.shape[1]
    t_new = q_ref.shape[0]
    scale = HEAD_DIM ** -0.5

    def copies(step, slot):
        return [pltpu.make_async_copy(ck_hbm.at[0, pt_ref[step * PAGES_PER_STEP + i]],
                                      kbuf.at[slot, i], sem.at[slot])
                for i in range(PAGES_PER_STEP)]

    @pl.when(g == 0)
    def _():
        for ahead in range(min(PREFETCH_DEPTH - 1, n_g)):
            for i, c in enumerate(copies(ahead, ahead)):
                c.start(priority=i % 2)

    nxt = g + PREFETCH_DEPTH - 1

    @pl.when(nxt < n_g)
    def _():
        for i, c in enumerate(copies(nxt, nxt % PREFETCH_DEPTH)):
            c.start(priority=i % 2)

    slot = g % PREFETCH_DEPTH
    for c in copies(g, slot):
        c.wait()
    page_refs = [kbuf.at[slot, i] for i in range(PAGES_PER_STEP)]

    @pl.when(j == 0)
    def _():
        zero = jnp.zeros((t_new, HEAD_DIM), F32)
        for p in range(N_HEADS // 2):
            q0 = q_ref[:, (2 * p) * HEAD_DIM:(2 * p + 1) * HEAD_DIM] * scale
            q1 = q_ref[:, (2 * p + 1) * HEAD_DIM:(2 * p + 2) * HEAD_DIM] * scale
            qbd_ref[p] = jnp.concatenate([jnp.concatenate([q0, zero], axis=-1),
                                          jnp.concatenate([zero, q1], axis=-1)], axis=0).astype(BF16)

    for i in range(PAGES_PER_STEP):
        for p in range(N_HEADS // 2):
            kcat = jnp.concatenate([page_refs[i][2 * p], page_refs[i][2 * p + 1]], axis=-1)
            sc_ref[p, i // ppb, :, (i % ppb) * ps:(i % ppb + 1) * ps] = _dot_nt(qbd_ref[p], kcat)

    for i in range(bps):
        for hd in range(N_HEADS):
            tot = jnp.zeros((1, HEAD_DIM), F32)
            for pp in range(ppb):
                tot = tot + jnp.sum(page_refs[i * ppb + pp][hd], axis=0, keepdims=True)
            kmean_ref[hd, pl.ds(j * bps + i, 1), :] = tot / float(MOBA_BLOCK)

    @pl.when(j == n_j - 1)
    def _():
        for hd in range(N_HEADS):
            q = q_ref[:, hd * HEAD_DIM:(hd + 1) * HEAD_DIM]
            gsc = lax.dot_general(q, kmean_ref[hd], (((1,), (1,)), ((), ())), precision=HIGHEST,
                                  preferred_element_type=F32)
            blk, rank = _topk_rank(gsc, n_blk)
            lane = lax.broadcasted_iota(jnp.int32, (t_new, LANES), 1)
            out = jnp.zeros((t_new, LANES), jnp.int32)
            for r in range(MOBA_TOPK):
                idx = jnp.sum(jnp.where(rank == r, blk.astype(F32), 0.0), axis=-1, keepdims=True)
                out = jnp.where(lane == r, idx.astype(jnp.int32), out)
            sel_ref[0, hd] = out


def _moba_select(q_s, cache_k, page_table, t_new):
    n_seq, n_pg = page_table.shape
    _, _, _, ps, _ = cache_k.shape
    assert (n_pg * ps) % MOBA_BLOCK == 0, "cached rows must end on a MoBA block boundary"
    assert n_pg % PAGES_PER_STEP == 0 and MOBA_BLOCK % ps == 0
    n_blk = n_pg * ps // MOBA_BLOCK
    assert n_blk >= MOBA_TOPK
    hw = N_HEADS * HEAD_DIM
    bps = PAGES_PER_STEP * ps // MOBA_BLOCK

    n_j = n_pg // PAGES_PER_STEP
    grid_spec = pltpu.PrefetchScalarGridSpec(
        num_scalar_prefetch=1,
        grid=(n_seq * n_j,),
        in_specs=[pl.BlockSpec((t_new, hw), lambda g, pt: (g // n_j, 0)),
                  pl.BlockSpec(memory_space=pl.ANY)],
        out_specs=(
            pl.BlockSpec((1, N_HEADS, t_new, LANES), lambda g, pt: (g // n_j, 0, 0, 0)),
            pl.BlockSpec((None, N_HEADS // 2, bps, 2 * t_new, MOBA_BLOCK),
                         lambda g, pt: (g // n_j, 0, g % n_j, 0, 0))),
        scratch_shapes=[pltpu.VMEM((PREFETCH_DEPTH, PAGES_PER_STEP, N_HEADS, ps, HEAD_DIM), F32),
                        pltpu.VMEM((N_HEADS, n_blk, HEAD_DIM), F32),
                        pltpu.VMEM((N_HEADS // 2, 2 * t_new, 2 * HEAD_DIM), BF16),
                        pltpu.SemaphoreType.DMA((PREFETCH_DEPTH,))],
    )
    return pl.pallas_call(
        functools.partial(_moba_select_kernel, n_j=n_j, n_g=n_seq * n_j),
        grid_spec=grid_spec,
        out_shape=(jax.ShapeDtypeStruct((n_seq, N_HEADS, t_new, LANES), jnp.int32),
                   jax.ShapeDtypeStruct((n_seq, N_HEADS // 2, n_blk, 2 * t_new, MOBA_BLOCK), F32)),
        compiler_params=_params("arbitrary"),
        name="moba_select",
    )(page_table.reshape(-1), q_s, cache_k)


def _moba_sample_kernel(pages_ref, blk_ref, nblk_ref, q_ref, kn_ref, vn_ref, sc_ref, cp_ref,
                        cv_hbm, o_ref, vbuf, ssel_ref, sem, *, n_fetch, n_g):
    g = pl.program_id(0)
    ps = vbuf.shape[2]
    t_new = q_ref.shape[0]
    ppb = MOBA_BLOCK // ps
    n_list = n_fetch // ppb
    scale = HEAD_DIM ** -0.5

    def fetch(step, slot, wait):
        hd = step % N_HEADS

        def body(j, carry):
            for pp in range(ppb):
                i = j * ppb + pp
                c = pltpu.make_async_copy(cv_hbm.at[0, pages_ref[step * n_fetch + i], hd],
                                          vbuf.at[slot, i], sem.at[slot])
                if wait:
                    c.wait()
                else:
                    c.start(priority=pp % 2)
            return carry

        lax.fori_loop(0, nblk_ref[step], body, 0)

    @pl.when(g == 0)
    def _():
        vbuf[...] = jnp.zeros(vbuf.shape, F32)
        for ahead in range(min(PREFETCH_DEPTH - 1, n_g)):
            fetch(ahead, ahead, False)

    nxt = g + PREFETCH_DEPTH - 1

    @pl.when(nxt < n_g)
    def _():
        fetch(nxt, nxt % PREFETCH_DEPTH, False)

    for t in range(t_new):
        for r in range(MOBA_TOPK):
            blk = blk_ref[(g * t_new + t) * MOBA_TOPK + r]
            ssel_ref[t:t + 1, r * MOBA_BLOCK:(r + 1) * MOBA_BLOCK] = sc_ref[blk, t:t + 1, :]
    s = ssel_ref[...]

    pad = jnp.zeros((2 * SUBLANES - t_new, HEAD_DIM), F32)
    qp = jnp.concatenate([q_ref[...] * scale, pad], axis=0)
    kn = jnp.concatenate([kn_ref[...], pad], axis=0)
    vn = jnp.concatenate([vn_ref[...], pad], axis=0)
    s_new = _dot_nt(qp, kn)[:t_new]
    rn = lax.broadcasted_iota(jnp.int32, s_new.shape, 0)
    cn = lax.broadcasted_iota(jnp.int32, s_new.shape, 1)
    s_new = jnp.where((cn <= rn) & (cn < t_new), s_new, NEG)
    m = jnp.maximum(jnp.max(s, axis=-1, keepdims=True), jnp.max(s_new, axis=-1, keepdims=True))
    p = jnp.exp(s - m)
    p_new = jnp.exp(s_new - m)
    den = jnp.sum(p, axis=-1, keepdims=True) + jnp.sum(p_new, axis=-1, keepdims=True)

    cp = cp_ref[...]
    cols = []
    for j in range(n_list):
        col = jnp.zeros((t_new, MOBA_BLOCK), F32)
        for r in range(MOBA_TOPK):
            col = col + jnp.where(cp[:, r:r + 1] == j, p[:, r * MOBA_BLOCK:(r + 1) * MOBA_BLOCK], 0.0)
        cols.append(col)
    p_wide = jnp.concatenate(cols, axis=-1)
    zero_rows = jnp.zeros((2 * SUBLANES - t_new, n_list * MOBA_BLOCK), F32)
    p_wide = jnp.concatenate([p_wide, zero_rows], axis=0)
    p_new = jnp.concatenate([p_new, jnp.zeros((2 * SUBLANES - t_new, p_new.shape[1]), F32)], axis=0)

    slot = g % PREFETCH_DEPTH
    fetch(g, slot, True)
    vc = vbuf[slot].reshape(n_fetch * ps, HEAD_DIM)
    o = _dot(p_wide, vc) + _dot(p_new, vn)
    o_ref[...] = o[:t_new] / den


def _moba_sample(q_s, k_new, v_new, scores, cache_v, phys_pages, sel_blocks, n_listed, pick_entry,
                 n_seq, t_new):
    ps = cache_v.shape[3]
    n_fetch = phys_pages.shape[0] // (n_seq * N_HEADS)
    n_blk = scores.shape[2]
    head = lambda g, *_: (g // N_HEADS, g % N_HEADS)
    grid_spec = pltpu.PrefetchScalarGridSpec(
        num_scalar_prefetch=3,
        grid=(n_seq * N_HEADS,),
        in_specs=[
            pl.BlockSpec((t_new, HEAD_DIM), head),
            pl.BlockSpec((None, None, t_new, HEAD_DIM), lambda g, *_: head(g) + (0, 0)),
            pl.BlockSpec((None, None, t_new, HEAD_DIM), lambda g, *_: head(g) + (0, 0)),
            pl.BlockSpec((None, None, n_blk, t_new, MOBA_BLOCK),
                         lambda g, *_: (g // N_HEADS, (g % N_HEADS) // 2, 0, g % 2, 0)),
            pl.BlockSpec((None, t_new, LANES), lambda g, *_: (g, 0, 0)),
            pl.BlockSpec(memory_space=pl.ANY),
        ],
        out_specs=pl.BlockSpec((t_new, HEAD_DIM), head),
        scratch_shapes=[
            pltpu.VMEM((PREFETCH_DEPTH, n_fetch, ps, HEAD_DIM), F32),
            pltpu.VMEM((t_new, MOBA_TOPK * MOBA_BLOCK), F32),
            pltpu.SemaphoreType.DMA((PREFETCH_DEPTH,)),
        ],
    )
    return pl.pallas_call(
        functools.partial(_moba_sample_kernel, n_fetch=n_fetch, n_g=n_seq * N_HEADS),
        grid_spec=grid_spec,
        out_shape=jax.ShapeDtypeStruct(q_s.shape, F32),
        compiler_params=_params("arbitrary"),
        name="moba_sample",
    )(phys_pages, sel_blocks, n_listed, q_s, k_new, v_new, scores, pick_entry, cache_v)


def _merge_kernel(x_ref, bh_ref, bm_ref, gate_ref, wlh_ref, wlm_ref, wout_ref, gffn_ref,
                  wr_ref, br_ref, cnt0_ref, x1_ref, h2_ref, ri_ref, rw_ref, cnt_ref, run_ref):
    i = pl.program_id(0)
    tm, d = x_ref.shape

    @pl.when(i == 0)
    def _():
        run_ref[...] = cnt0_ref[...]

    mixed = (gate_ref[:, 0:d] * _dot(bh_ref[...], wlh_ref[...])
             + gate_ref[:, d:2 * d] * _dot(bm_ref[...], wlm_ref[...]))
    x1 = x_ref[...] + _dot(mixed, wout_ref[...])
    x1_ref[...] = x1
    h2 = _rms(x1, gffn_ref[...])
    n_chunk = d // LANES
    for j in range(n_chunk):
        h2_ref[pl.ds(j, tm, stride=n_chunk), :] = h2[:, j * LANES:(j + 1) * LANES]

    logits = lax.dot_general(wr_ref[...], h2, (((1,), (1,)), ((), ())), precision=HIGHEST,
                             preferred_element_type=F32) + br_ref[...]
    sub_e = lax.broadcasted_iota(jnp.int32, (N_EXPERTS, tm), 0).astype(F32)
    lg = logits
    vals, ids, hots = [], [], []
    for _ in range(TOP_K):
        mx = jnp.max(lg, axis=0, keepdims=True)
        idx = jnp.min(jnp.where(lg == mx, sub_e, float(N_EXPERTS)), axis=0, keepdims=True)
        hot = sub_e == idx
        vals.append(mx)
        ids.append(idx)
        hots.append(hot)
        lg = jnp.where(hot, -jnp.inf, lg)
    exps = [jnp.exp(v - vals[0]) for v in vals]
    den = exps[0] + exps[1] + exps[2] + exps[3]

    chosen = jnp.zeros((N_EXPERTS, tm), F32)
    for hot in hots:
        chosen = chosen + jnp.where(hot, 1.0, 0.0)
    r_i = lax.broadcasted_iota(jnp.int32, (tm, tm), 0)
    c_i = lax.broadcasted_iota(jnp.int32, (tm, tm), 1)
    earlier = jnp.where(r_i < c_i, 1.0, 0.0)
    base = _dot(chosen, earlier) + run_ref[:, 0:1]
    run_ref[...] = run_ref[...] + jnp.sum(chosen, axis=1, keepdims=True)
    cnt_ref[...] = run_ref[...]

    sub_i = lax.broadcasted_iota(jnp.int32, (2 * TOP_K, tm), 0)
    sub_w = lax.broadcasted_iota(jnp.int32, (LANES, tm), 0)
    out_i = jnp.zeros((2 * TOP_K, tm), jnp.int32)
    out_w = jnp.zeros((LANES, tm), F32)
    for k in range(TOP_K):
        r_k = jnp.sum(jnp.where(hots[k], base, 0.0), axis=0, keepdims=True)
        out_i = jnp.where(sub_i == k, ids[k].astype(jnp.int32), out_i)
        out_i = jnp.where(sub_i == TOP_K + k, r_k.astype(jnp.int32), out_i)
        out_w = jnp.where(sub_w == k, exps[k] / den, out_w)
    ri_ref[...] = out_i
    rw_ref[...] = out_w.T


def _merge(x2d, bh, bm, gates, wlh, wlm, wout, g_ffn, w_router, b_router, cnt0):
    n, d = x2d.shape
    hw = N_HEADS * HEAD_DIM
    tm = MERGE_TILE if n % MERGE_TILE == 0 else n
    n_chunk = d // LANES
    row = lambda w: pl.BlockSpec((tm, w), lambda i: (i, 0))
    return pl.pallas_call(
        _merge_kernel,
        grid=(n // tm,),
        in_specs=[row(d), row(hw), row(hw), row(2 * d), _full((hw, d)), _full((hw, d)),
                  _full((d, d)), _full((1, d)), _full((N_EXPERTS, d)), _full((N_EXPERTS, 1)),
                  _full((N_EXPERTS, LANES))],
        out_specs=(row(d), pl.BlockSpec((tm * n_chunk, LANES), lambda i: (i, 0)),
                   pl.BlockSpec((2 * TOP_K, tm), lambda i: (0, i)), row(LANES),
                   _full((N_EXPERTS, LANES))),
        out_shape=(jax.ShapeDtypeStruct((n, d), F32),
                   jax.ShapeDtypeStruct((n * n_chunk, LANES), F32),
                   jax.ShapeDtypeStruct((2 * TOP_K, n), jnp.int32),
                   jax.ShapeDtypeStruct((n, LANES), F32),
                   jax.ShapeDtypeStruct((N_EXPERTS, LANES), F32)),
        scratch_shapes=[pltpu.VMEM((N_EXPERTS, LANES), F32)],
        compiler_params=_params("arbitrary"),
        name="merge_router",
    )(x2d, bh, bm, gates, wlh, wlm, wout, g_ffn.reshape(1, d), w_router.T,
      b_router.reshape(N_EXPERTS, 1), cnt0)


ZERO_ROWS = 64


def _dispatch_kernel(zlo_ref, zhi_ref, dest_p_ref, dest_s_ref, h2p_ref, h2s_ref, xs_hbm,
                     zbuf, sem, *, n_p, rows_per_tok):
    i = pl.program_id(0)
    r = rows_per_tok
    zrows = ZERO_ROWS * r

    def zero_copy(s):
        return pltpu.make_async_copy(zbuf, xs_hbm.at[pl.ds(pl.multiple_of(s * zrows, zrows), zrows)],
                                     sem.at[0])

    @pl.when(i == 0)
    def _():
        zbuf[...] = jnp.zeros(zbuf.shape, F32)

        def each_range(fn):
            def outer(e, carry):
                def inner(s, c):
                    fn(s)
                    return c
                return lax.fori_loop(zlo_ref[e], zhi_ref[e], inner, carry)
            lax.fori_loop(0, zlo_ref.shape[0], outer, 0)

        each_range(lambda s: zero_copy(s).start())
        each_range(lambda s: zero_copy(s).wait())

    def scatter(h2_ref, dest_ref):
        tm = h2_ref.shape[0] // r

        unroll = 8 if tm % 8 == 0 else 1

        def body(tu, carry):
            for u in range(unroll):
                t = tu * unroll + u
                src = h2_ref.at[pl.ds(pl.multiple_of(t * r, r), r)]
                for k in range(TOP_K):
                    dst = xs_hbm.at[pl.ds(pl.multiple_of(dest_ref[k * tm + t] * r, r), r)]
                    pltpu.make_async_copy(src, dst, sem.at[1]).start(priority=k % 2)
            return carry

        lax.fori_loop(0, tm // unroll, body, 0)
        n_rows = tm * TOP_K * r
        pltpu.make_async_copy(xs_hbm.at[pl.ds(0, n_rows)], xs_hbm.at[pl.ds(0, n_rows)],
                              sem.at[1]).wait()

    @pl.when(i < n_p)
    def _():
        scatter(h2p_ref, dest_p_ref)

    @pl.when(i >= n_p)
    def _():
        scatter(h2s_ref, dest_s_ref)


def _dispatch(zlo, zhi, dest_p, dest_s, h2_p, h2_s, n_slots, rows_per_tok):
    r = rows_per_tok
    tm_p = _token_tile(h2_p.shape[0] // r)
    tm_s = _token_tile(h2_s.shape[0] // r)
    n_p = h2_p.shape[0] // (tm_p * r)
    n_s = h2_s.shape[0] // (tm_s * r)
    first = lambda i, *_: (jnp.minimum(i, n_p - 1),)
    second = lambda i, *_: (jnp.maximum(i - n_p, 0),)
    grid_spec = pltpu.PrefetchScalarGridSpec(
        num_scalar_prefetch=2,
        grid=(n_p + n_s,),
        in_specs=[
            pl.BlockSpec((tm_p * TOP_K,), first, memory_space=pltpu.SMEM),
            pl.BlockSpec((tm_s * TOP_K,), second, memory_space=pltpu.SMEM),
            pl.BlockSpec((tm_p * r, LANES), lambda i, *_: first(i) + (0,)),
            pl.BlockSpec((tm_s * r, LANES), lambda i, *_: second(i) + (0,)),
        ],
        out_specs=pl.BlockSpec(memory_space=pl.ANY),
        scratch_shapes=[pltpu.VMEM((ZERO_ROWS * r, LANES), F32), pltpu.SemaphoreType.DMA((2,))],
    )
    return pl.pallas_call(
        functools.partial(_dispatch_kernel, n_p=n_p, rows_per_tok=r),
        grid_spec=grid_spec,
        out_shape=jax.ShapeDtypeStruct((n_slots * r, LANES), F32),
        compiler_params=_params("arbitrary"),
        name="moe_dispatch",
    )(zlo, zhi, dest_p, dest_s, h2_p, h2_s)


def _moe_kernel(te_ref, nu_ref, xs_ref, wgu_f32, bgu_ref, wdn_f32, bdn_ref, ys_ref,
                wgu_ref, wdn_ref):
    s = pl.program_id(0)
    i = s - 1
    n_chunk = xs_ref.shape[0] // MOE_TILE
    d_ff = wdn_ref.shape[0]
    used = (i >= 0) & (i < nu_ref[0])

    @pl.when(used)
    def _():
        x = jnp.concatenate(
            [xs_ref[pl.ds(j, MOE_TILE, stride=n_chunk), :] for j in range(n_chunk)], axis=-1)
        gu = _dot(x, wgu_ref[...]) + bgu_ref[...]
        g_ = jnp.minimum(gu[:, :d_ff], SWIGLU_LIMIT)
        u_ = jnp.clip(gu[:, d_ff:], -SWIGLU_LIMIT, SWIGLU_LIMIT)
        act = (u_ + 1.0) * g_ * _sigmoid(SWIGLU_ALPHA * g_)
        y = _dot(act, wdn_ref[...]) + bdn_ref[...]
        for j in range(n_chunk):
            ys_ref[pl.ds(j, MOE_TILE, stride=n_chunk), :] = y[:, j * LANES:(j + 1) * LANES]

    @pl.when(i >= nu_ref[0])
    def _():
        ys_ref[...] = jnp.zeros(ys_ref.shape, F32)

    nxt = jnp.minimum(s, te_ref.shape[0] - 1)
    new_expert = (s == 0) | (te_ref[nxt] != te_ref[jnp.maximum(s - 1, 0)])

    @pl.when((s < nu_ref[0]) & new_expert)
    def _():
        rows = 128
        for c in range(wgu_ref.shape[0] // rows):
            wgu_ref[c * rows:(c + 1) * rows, :] = wgu_f32[c * rows:(c + 1) * rows, :].astype(BF16)
        for c in range(wdn_ref.shape[0] // rows):
            wdn_ref[c * rows:(c + 1) * rows, :] = wdn_f32[c * rows:(c + 1) * rows, :].astype(BF16)


def _moe(tile_e, n_used, xs, w_gu, b_gu, w_dn, b_dn, rows_per_tok):
    n_tiles = xs.shape[0] // (MOE_TILE * rows_per_tok)
    d, two_ff = w_gu.shape[1:]
    d_ff = w_dn.shape[1]
    tile = lambda s, nu: jnp.clip(s - 1, 0, nu[0] - 1)
    ahead = lambda s, nu: jnp.minimum(s, nu[0] - 1)
    grid_spec = pltpu.PrefetchScalarGridSpec(
        num_scalar_prefetch=2,
        grid=(n_tiles + 1,),
        in_specs=[
            pl.BlockSpec((MOE_TILE * rows_per_tok, LANES), lambda s, te, nu: (tile(s, nu), 0)),
            pl.BlockSpec((None, d, two_ff), lambda s, te, nu: (te[ahead(s, nu)], 0, 0)),
            pl.BlockSpec((None, 1, two_ff), lambda s, te, nu: (te[tile(s, nu)], 0, 0)),
            pl.BlockSpec((None, d_ff, d), lambda s, te, nu: (te[ahead(s, nu)], 0, 0)),
            pl.BlockSpec((None, 1, d), lambda s, te, nu: (te[tile(s, nu)], 0, 0)),
        ],
        out_specs=pl.BlockSpec((MOE_TILE * rows_per_tok, LANES),
                               lambda s, te, nu: (jnp.maximum(s - 1, 0), 0)),
        scratch_shapes=[pltpu.VMEM((d, two_ff), BF16), pltpu.VMEM((d_ff, d), BF16)],
    )
    return pl.pallas_call(
        _moe_kernel,
        grid_spec=grid_spec,
        out_shape=jax.ShapeDtypeStruct(xs.shape, F32),
        compiler_params=_params("arbitrary"),
        name="moe_ffn",
    )(tile_e, n_used, xs, w_gu, b_gu.reshape(N_EXPERTS, 1, two_ff), w_dn,
      b_dn.reshape(N_EXPERTS, 1, d))


def _combine_kernel(dest_ref, dest_next_ref, ys_hbm, x1_ref, rw_ref, p_ref, gple_ref, wple_ref,
                    wgate_ref, y_ref, buf, sem, *, rows_per_tok):
    i = pl.program_id(0)
    n_i = pl.num_programs(0)
    tm, d = x1_ref.shape
    r = rows_per_tok

    def row_copy(d_ref, slot, t, k):
        src = ys_hbm.at[pl.ds(pl.multiple_of(d_ref[k * tm + t] * r, r), r)]
        dst = buf.at[slot, pl.ds(pl.multiple_of((k * tm + t) * r, r), r)]
        return pltpu.make_async_copy(src, dst, sem.at[slot])

    def wait_slot(slot):
        pltpu.make_async_copy(buf.at[slot], buf.at[slot], sem.at[slot]).wait()

    slot = i % 2

    @pl.when(i == 0)
    def _():
        def body(t, carry):
            for k in range(TOP_K):
                row_copy(dest_ref, 0, t, k).start(priority=k % 2)
            return carry
        lax.fori_loop(0, tm, body, 0)

    for t in range(tm):
        for k in range(TOP_K):
            row_copy(dest_next_ref, 1 - slot, t, k).start(priority=k % 2)

    pe = _dot(p_ref[...], wple_ref[...])
    wait_slot(slot)

    rw = rw_ref[...]
    cols = []
    for j in range(r):
        acc = jnp.zeros((tm, LANES), F32)
        for k in range(TOP_K):
            acc = acc + rw[:, k:k + 1] * buf[slot, pl.ds(k * tm * r + j, tm, stride=r), :]
        cols.append(acc)
    x2 = x1_ref[...] + jnp.concatenate(cols, axis=-1)
    gate = _sigmoid(_dot(_rms(x2, gple_ref[...]), wgate_ref[...]))
    y_ref[...] = x2 + pe * gate

    @pl.when(i == n_i - 1)
    def _():
        wait_slot(1 - slot)


def _combine(dest_flat, ys, x1, rw, p2d, g_ple, wple_bf, wgate_bf, rows_per_tok):
    n, d = x1.shape
    tm = _token_tile(n)
    n_i = n // tm
    ple = p2d.shape[1]
    row = lambda w: pl.BlockSpec((tm, w), lambda i: (i, 0))
    return pl.pallas_call(
        functools.partial(_combine_kernel, rows_per_tok=rows_per_tok),
        grid=(n_i,),
        in_specs=[pl.BlockSpec((tm * TOP_K,), lambda i: (i,), memory_space=pltpu.SMEM),
                  pl.BlockSpec((tm * TOP_K,), lambda i: (jnp.minimum(i + 1, n_i - 1),),
                               memory_space=pltpu.SMEM),
                  pl.BlockSpec(memory_space=pl.ANY), row(d), row(LANES), row(ple),
                  _full((1, d)), _full((ple, d)), _full((d, d))],
        out_specs=row(d),
        out_shape=jax.ShapeDtypeStruct((n, d), F32),
        scratch_shapes=[pltpu.VMEM((2, TOP_K * tm * rows_per_tok, LANES), F32),
                        pltpu.SemaphoreType.DMA((2,))],
        compiler_params=_params("arbitrary"),
        name="combine_ple",
    )(dest_flat, dest_flat, ys, x1, rw, p2d, g_ple.reshape(1, d), wple_bf, wgate_bf)


def kernel(x_prompt, x_sample, cache_k, cache_v, state_hgrn, page_table, p_prompt, p_sample,
           g_mix, w_in, lb_logits, g_hg_out, g_q, g_k, w_lift_hgrn, w_lift_moba, w_out, g_ffn,
           w_router, b_router, w_gate_up, b_gate_up, w_down, b_down, g_ple, w_ple, w_ple_gate):
    depth = g_mix.shape[0]
    assert depth == 1, "single-layer step"
    li = 0
    n_b, seq, d = x_prompt.shape
    n_db, t_new, _ = x_sample.shape
    ps = cache_k.shape[3]
    rows_per_tok = d // LANES
    assert rows_per_tok == SUBLANES and seq % ps == 0

    lb = jnp.cumsum(jax.nn.softmax(lb_logits.astype(F32), axis=0), axis=0)[li]
    w_in_bf = w_in[li].astype(BF16)
    wlh, wlm, wout = (w_lift_hgrn[li].astype(BF16), w_lift_moba[li].astype(BF16),
                      w_out[li].astype(BF16))
    wple_bf, wgate_bf = w_ple[li].astype(BF16), w_ple_gate[li].astype(BF16)

    xp = x_prompt.reshape(n_b * seq, d)
    xs_ = x_sample.reshape(n_db * t_new, d)

    zh_p, q_p, k_p, v_p, gates_p = _inproj(xp, g_mix[li], w_in_bf, lb, g_q[li], g_k[li], ps)
    bh_p, st_p = _hgrn(zh_p, None, g_hg_out[li], n_b, seq)
    bm_p = _moba_prompt(q_p, k_p, v_p, n_b, seq)

    zh_s, q_s, k_s, v_s, gates_s = _inproj(xs_, g_mix[li], w_in_bf, lb, g_q[li], g_k[li], t_new)
    bh_s, st_s = _hgrn(zh_s, state_hgrn[li], g_hg_out[li], n_db, t_new)
    sel, scores = _moba_select(q_s, cache_k, page_table, t_new)
    sel = sel[..., :MOBA_TOPK]
    ppb = MOBA_BLOCK // ps
    n_pg = page_table.shape[1]
    n_blk = n_pg // ppb
    n_pick = t_new * MOBA_TOPK
    blk_ids = jnp.arange(n_blk, dtype=jnp.int32)
    picks = sel.reshape(n_db * N_HEADS, n_pick)
    member = jnp.any(picks[:, :, None] == blk_ids, axis=1)
    entry_of_blk = jnp.cumsum(member.astype(jnp.int32), axis=1) - 1
    n_listed = jnp.sum(member.astype(jnp.int32), axis=1)
    listed = jnp.sum(jnp.where(member[:, None, :] & (entry_of_blk[:, None, :]
                                                     == jnp.arange(n_pick, dtype=jnp.int32)[:, None]),
                               blk_ids, 0), axis=-1)
    pick_entry = jnp.sum(jnp.where(picks[:, :, None] == blk_ids, entry_of_blk[:, None, :], 0),
                         axis=-1).reshape(n_db * N_HEADS, t_new, MOBA_TOPK)
    pick_entry = jnp.pad(pick_entry, ((0, 0), (0, 0), (0, LANES - MOBA_TOPK)))
    logical = (listed[..., None] * ppb + jnp.arange(ppb, dtype=jnp.int32)).reshape(n_db, -1)
    hit = logical[:, :, None] == jnp.arange(n_pg, dtype=jnp.int32)
    phys = jnp.sum(jnp.where(hit, page_table[:, None, :], 0), axis=-1).reshape(-1)
    bm_s = _moba_sample(q_s, k_s, v_s, scores, cache_v, phys, sel.reshape(-1), n_listed,
                        pick_entry, n_db, t_new)

    cnt0 = jnp.zeros((N_EXPERTS, LANES), F32)
    x1_p, h2_p, ri_p, rw_p, cnt_p = _merge(xp, bh_p, bm_p, gates_p, wlh, wlm, wout, g_ffn[li],
                                           w_router[li], b_router[li], cnt0)
    x1_s, h2_s, ri_s, rw_s, cnt_all = _merge(xs_, bh_s, bm_s, gates_s, wlh, wlm, wout, g_ffn[li],
                                             w_router[li], b_router[li], cnt_p)

    counts = cnt_all[:, 0].astype(jnp.int32)
    padded = (counts + MOE_TILE - 1) // MOE_TILE * MOE_TILE
    pad_end = jnp.cumsum(padded)
    pad_start = pad_end - padded
    n_assign = (n_b * seq + n_db * t_new) * TOP_K
    n_tiles = -(-(n_assign + N_EXPERTS * (MOE_TILE - 1)) // MOE_TILE)
    tile_lo = jnp.arange(n_tiles, dtype=jnp.int32) * MOE_TILE
    tile_e = jnp.minimum(jnp.sum(pad_end[None, :] <= tile_lo[:, None], axis=1),
                         N_EXPERTS - 1).astype(jnp.int32)
    n_used = (pad_end[-1:] // MOE_TILE).astype(jnp.int32)
    zlo = jnp.concatenate([(pad_start + counts) // ZERO_ROWS, pad_end[-1:] // ZERO_ROWS])
    zhi = jnp.concatenate([pad_end // ZERO_ROWS,
                           jnp.full((1,), n_tiles * MOE_TILE // ZERO_ROWS, jnp.int32)])

    def dest_of(ri):
        n = ri.shape[1]
        tm = _token_tile(n)
        e, rank = ri[:TOP_K], ri[TOP_K:]
        start = jnp.sum(jnp.where(e[:, :, None] == jnp.arange(N_EXPERTS, dtype=jnp.int32),
                                  pad_start, 0), axis=-1)
        return (start + rank).reshape(TOP_K, n // tm, tm).transpose(1, 0, 2).reshape(-1)

    dest_p, dest_s = dest_of(ri_p), dest_of(ri_s)
    xs = _dispatch(zlo.astype(jnp.int32), zhi.astype(jnp.int32), dest_p, dest_s, h2_p, h2_s,
                   n_tiles * MOE_TILE, rows_per_tok)
    ys = _moe(tile_e, n_used, xs, w_gate_up[li], b_gate_up[li], w_down[li], b_down[li],
              rows_per_tok)

    y_p = _combine(dest_p, ys, x1_p, rw_p, p_prompt[li].reshape(n_b * seq, -1), g_ple[li],
                   wple_bf, wgate_bf, rows_per_tok)
    y_s = _combine(dest_s, ys, x1_s, rw_s, p_sample[li].reshape(n_db * t_new, -1), g_ple[li],
                   wple_bf, wgate_bf, rows_per_tok)

    return (y_p.reshape(n_b, seq, d), y_s.reshape(n_db, t_new, d),
            k_p.reshape(1, n_b, seq // ps, N_HEADS, ps, HEAD_DIM),
            v_p.reshape(1, n_b, seq // ps, N_HEADS, ps, HEAD_DIM),
            k_s[None], v_s[None], st_p[None], st_s[None])
```
